```python
import math, functools
import jax, jax.numpy as jnp
from jax import lax
import numpy as np

D_MODEL = 1024
BATCH = 8
SEQ = 2048
DEPTH = 2
DEC_BATCH = 32
DEC_SEQ = 64
PAST_LEN = 2048

CHUNK = 64
N_A_LAYERS = DEPTH // 2
N_B_LAYERS = DEPTH - N_A_LAYERS
SSM_GROUP = 16
N_GROUPS = D_MODEL // SSM_GROUP
SSM_STATE = 64
HEAD_DIM = 64
N_HEADS = D_MODEL // HEAD_DIM
D_FF = 4 * D_MODEL
Q_BLOCK = 128
EPS = 1e-6
DT_MIN = 0.001
DT_MAX = 0.1
FORGET_BIAS = 5.0
ATTN_SCALE = 1.0 / math.sqrt(HEAD_DIM)

kernel_name = "s5_fox_yoco_macaron_stream_step"


def _rmsnorm(x, g):
    xf = x.astype(jnp.float32)
    xf = xf * lax.rsqrt(jnp.mean(xf * xf, axis=-1, keepdims=True) + EPS)
    return (xf * g.astype(jnp.float32)).astype(x.dtype)


def _swiglu(x, g, wg, wu, wd):
    h = _rmsnorm(x, g)
    return (jax.nn.silu(h @ wg) * (h @ wu)) @ wd


def _ssm_combine(e1, e2):
    a1, b1 = e1
    a2, b2 = e2
    return a1 * a2, a2 * b1 + b2


def _s5(u, h0_re, h0_im, a_re, a_im, log_dt, b_re, b_im, c_re, c_im, d_skip, w_glu_a, w_glu_b):
    f32 = jnp.float32
    bsz, L, _ = u.shape
    uf = u.astype(f32).reshape(bsz, L, N_GROUPS, SSM_GROUP)
    lam = lax.complex(a_re.astype(f32), a_im.astype(f32))
    dt = jnp.exp(log_dt.astype(f32))[:, None]
    abar = jnp.exp(lam * dt)
    bbar = ((abar - 1.0) / lam)[:, :, None] * lax.complex(b_re.astype(f32), b_im.astype(f32))
    cmat = lax.complex(c_re.astype(f32), c_im.astype(f32))
    bu = jnp.einsum("blgc,gpc->blgp", uf.astype(jnp.complex64), bbar)
    h0 = lax.complex(h0_re.astype(f32), h0_im.astype(f32))
    bu = bu.at[:, 0].add(abar * h0)
    a_el = jnp.broadcast_to(abar, (1, L) + abar.shape)
    _, h = lax.associative_scan(_ssm_combine, (a_el, bu), axis=1)
    y = jnp.real(jnp.einsum("blgp,gcp->blgc", h, cmat)) + d_skip.astype(f32).reshape(N_GROUPS, SSM_GROUP) * uf
    z = jax.nn.gelu(y.reshape(bsz, L, D_MODEL)).astype(u.dtype)
    out = (z @ w_glu_a) * jax.nn.sigmoid(z @ w_glu_b)
    h_last = h[:, -1]
    return out, jnp.real(h_last), jnp.imag(h_last)


def _shared_kv(h, kv_norm, w_kvf, b_f, k_norm):
    bsz, L, _ = h.shape
    p = _rmsnorm(h, kv_norm) @ w_kvf
    k = _rmsnorm(p[..., :D_MODEL].reshape(bsz, L, N_HEADS, HEAD_DIM), k_norm)
    v = p[..., D_MODEL:2 * D_MODEL].reshape(bsz, L, N_HEADS, HEAD_DIM)
    logf = jax.nn.log_sigmoid((p[..., 2 * D_MODEL:] + b_f).astype(jnp.float32))
    return k, v, logf


def _fox_block(q, k, v, cqT, ckT, qpos, kpos):
    s = jnp.einsum("bqhd,bkhd->bhqk", q, k).astype(jnp.float32) * ATTN_SCALE
    s = s + cqT[..., :, None] - ckT[..., None, :]
    s = jnp.where(kpos[None, :] <= qpos[:, None], s, -jnp.inf)
    p = jax.nn.softmax(s, axis=-1)
    return jnp.einsum("bhqk,bkhd->bqhd", p.astype(v.dtype), v)


def _fox_prompt(q, k, v, logf):
    bsz, S = q.shape[0], q.shape[1]
    cT = jnp.swapaxes(jnp.cumsum(logf, axis=1), 1, 2)
    kpos = jnp.arange(S)

    def one(i):
        start = i * Q_BLOCK
        qb = lax.dynamic_slice_in_dim(q, start, Q_BLOCK, axis=1)
        cqb = lax.dynamic_slice_in_dim(cT, start, Q_BLOCK, axis=2)
        return _fox_block(qb, k, v, cqb, cT, start + jnp.arange(Q_BLOCK), kpos)

    o = lax.map(one, jnp.arange(S // Q_BLOCK))
    return jnp.moveaxis(o, 0, 1).reshape(bsz, S, N_HEADS, HEAD_DIM)


def _fox_sample(q, k, v, logf, cache_k, cache_v, cache_logf):
    P, Lq = cache_k.shape[1], q.shape[1]
    c_past = jnp.cumsum(cache_logf.astype(jnp.float32), axis=1)
    c_new = c_past[:, -1:] + jnp.cumsum(logf, axis=1)
    k_all = jnp.concatenate([cache_k, k], axis=1)
    v_all = jnp.concatenate([cache_v, v], axis=1)
    c_allT = jnp.swapaxes(jnp.concatenate([c_past, c_new], axis=1), 1, 2)
    return _fox_block(q, k_all, v_all, jnp.swapaxes(c_new, 1, 2), c_allT, P + jnp.arange(Lq), jnp.arange(P + Lq))


def _trunk(x, h0_re, h0_im, attend, w):
    bsz, L, _ = x.shape
    ssm_re, ssm_im = [], []
    kv = None
    for l in range(DEPTH):
        x = x + 0.5 * _swiglu(x, w["ffn_norm"][l, 0], w["w_ffn_gate"][l, 0], w["w_ffn_up"][l, 0], w["w_ffn_down"][l, 0])
        u = _rmsnorm(x, w["mix_norm"][l])
        if l < N_A_LAYERS:
            y, hr, hi = _s5(u, h0_re[:, l], h0_im[:, l], w["ssm_a_re"][l], w["ssm_a_im"][l], w["ssm_log_dt"][l],
                            w["ssm_b_re"][l], w["ssm_b_im"][l], w["ssm_c_re"][l], w["ssm_c_im"][l],
                            w["ssm_d"][l], w["w_glu_a"][l], w["w_glu_b"][l])
            ssm_re.append(hr)
            ssm_im.append(hi)
        else:
            j = l - N_A_LAYERS
            q = _rmsnorm((u @ w["w_q"][j]).reshape(bsz, L, N_HEADS, HEAD_DIM), w["q_norm"][j])
            y = attend(q, kv[0], kv[1], kv[2]).reshape(bsz, L, D_MODEL) @ w["w_o"][j]
        x = x + y.astype(x.dtype)
        x = x + 0.5 * _swiglu(x, w["ffn_norm"][l, 1], w["w_ffn_gate"][l, 1], w["w_ffn_up"][l, 1], w["w_ffn_down"][l, 1])
        if l == N_A_LAYERS - 1:
            kv = _shared_kv(x, w["kv_norm"], w["w_kvf"], w["b_f"], w["k_norm"])
    return x, jnp.stack(ssm_re, axis=1), jnp.stack(ssm_im, axis=1), kv


def setup_inputs(seed: int = 0) -> dict:
    key = jax.random.key(seed)
    ks = jax.random.split(key, 32)
    f32 = jnp.float32

    def nrm(k, shape, scale=1.0):
        return scale * jax.random.normal(k, shape, f32)

    dsc = D_MODEL ** -0.5
    return {
        "x_prompt": nrm(ks[0], (BATCH, SEQ, D_MODEL)),
        "x_sample": nrm(ks[1], (DEC_BATCH, DEC_SEQ, D_MODEL)),
        "cache_k": nrm(ks[2], (DEC_BATCH, PAST_LEN, N_HEADS, HEAD_DIM)),
        "cache_v": nrm(ks[3], (DEC_BATCH, PAST_LEN, N_HEADS, HEAD_DIM)),
        "cache_logf": jax.nn.log_sigmoid(FORGET_BIAS + nrm(ks[4], (DEC_BATCH, PAST_LEN, N_HEADS), 0.5)),
        "state_ssm_re": nrm(ks[5], (DEC_BATCH, N_A_LAYERS, N_GROUPS, SSM_STATE), 0.1),
        "state_ssm_im": nrm(ks[6], (DEC_BATCH, N_A_LAYERS, N_GROUPS, SSM_STATE), 0.1),
        "ffn_norm": 1.0 + nrm(ks[7], (DEPTH, 2, D_MODEL), 0.01),
        "w_ffn_gate": nrm(ks[8], (DEPTH, 2, D_MODEL, D_FF), dsc),
        "w_ffn_up": nrm(ks[9], (DEPTH, 2, D_MODEL, D_FF), dsc),
        "w_ffn_down": nrm(ks[10], (DEPTH, 2, D_FF, D_MODEL), D_FF ** -0.5),
        "mix_norm": 1.0 + nrm(ks[11], (DEPTH, D_MODEL), 0.01),
        "ssm_a_re": -0.5 + nrm(ks[12], (N_A_LAYERS, N_GROUPS, SSM_STATE), 0.01),
        "ssm_a_im": jnp.pi * jnp.arange(SSM_STATE, dtype=f32) + nrm(ks[13], (N_A_LAYERS, N_GROUPS, SSM_STATE), 0.01),
        "ssm_log_dt": jax.random.uniform(ks[14], (N_A_LAYERS, N_GROUPS), f32, math.log(DT_MIN), math.log(DT_MAX)),
        "ssm_b_re": nrm(ks[15], (N_A_LAYERS, N_GROUPS, SSM_STATE, SSM_GROUP), (2 * SSM_GROUP) ** -0.5),
        "ssm_b_im": nrm(ks[16], (N_A_LAYERS, N_GROUPS, SSM_STATE, SSM_GROUP), (2 * SSM_GROUP) ** -0.5),
        "ssm_c_re": nrm(ks[17], (N_A_LAYERS, N_GROUPS, SSM_GROUP, SSM_STATE), SSM_STATE ** -0.5),
        "ssm_c_im": nrm(ks[18], (N_A_LAYERS, N_GROUPS, SSM_GROUP, SSM_STATE), SSM_STATE ** -0.5),
        "ssm_d": nrm(ks[19], (N_A_LAYERS, D_MODEL)),
        "w_glu_a": nrm(ks[20], (N_A_LAYERS, D_MODEL, D_MODEL), dsc),
        "w_glu_b": nrm(ks[21], (N_A_LAYERS, D_MODEL, D_MODEL), dsc),
        "kv_norm": 1.0 + nrm(ks[22], (D_MODEL,), 0.01),
        "w_kvf": jnp.concatenate([nrm(ks[23], (D_MODEL, 2 * D_MODEL), dsc),
                                  nrm(ks[24], (D_MODEL, N_HEADS), 0.1 * dsc)], axis=1),
        "b_f": FORGET_BIAS + nrm(ks[25], (N_HEADS,), 0.5),
        "k_norm": 1.0 + nrm(ks[26], (HEAD_DIM,), 0.01),
        "w_q": nrm(ks[27], (N_B_LAYERS, D_MODEL, D_MODEL), dsc),
        "q_norm": 1.0 + nrm(ks[28], (N_B_LAYERS, HEAD_DIM), 0.01),
        "w_o": nrm(ks[29], (N_B_LAYERS, D_MODEL, D_MODEL), dsc),
    }


def reference(x_prompt, x_sample, cache_k, cache_v, cache_logf, state_ssm_re, state_ssm_im,
              ffn_norm, w_ffn_gate, w_ffn_up, w_ffn_down, mix_norm,
              ssm_a_re, ssm_a_im, ssm_log_dt, ssm_b_re, ssm_b_im, ssm_c_re, ssm_c_im, ssm_d,
              w_glu_a, w_glu_b, kv_norm, w_kvf, b_f, k_norm, w_q, q_norm, w_o):
    w = dict(ffn_norm=ffn_norm, w_ffn_gate=w_ffn_gate, w_ffn_up=w_ffn_up, w_ffn_down=w_ffn_down,
             mix_norm=mix_norm, ssm_a_re=ssm_a_re, ssm_a_im=ssm_a_im, ssm_log_dt=ssm_log_dt,
             ssm_b_re=ssm_b_re, ssm_b_im=ssm_b_im, ssm_c_re=ssm_c_re, ssm_c_im=ssm_c_im, ssm_d=ssm_d,
             w_glu_a=w_glu_a, w_glu_b=w_glu_b, kv_norm=kv_norm, w_kvf=w_kvf, b_f=b_f, k_norm=k_norm,
             w_q=w_q, q_norm=q_norm, w_o=w_o)
    h0 = jnp.zeros((x_prompt.shape[0], N_A_LAYERS, N_GROUPS, SSM_STATE), jnp.float32)
    y_prompt, p_ssm_re, p_ssm_im, p_kv = _trunk(x_prompt, h0, h0, _fox_prompt, w)
    attend_sample = functools.partial(_fox_sample, cache_k=cache_k, cache_v=cache_v, cache_logf=cache_logf)
    y_sample, s_ssm_re, s_ssm_im, s_kv = _trunk(x_sample, state_ssm_re, state_ssm_im, attend_sample, w)
    return (y_prompt, y_sample, p_ssm_re, p_ssm_im, p_kv[0], p_kv[1], p_kv[2],
            s_ssm_re, s_ssm_im, s_kv[0], s_kv[1], s_kv[2])
```

```python
import functools
import math

import jax
import jax.numpy as jnp
import numpy as np
from jax import lax
from jax.experimental import pallas as pl
from jax.experimental.pallas import tpu as pltpu

F32 = jnp.float32
BF16 = jnp.bfloat16

EPS = 1e-6
HEAD_DIM = 64
SSM_GROUP = 16
S5_CHUNK = 16
LANES = 128
NEG_BIG = -1e30
VMEM_LIMIT = 52 * 1024 * 1024

FFN_TM = 1024
FFN_TF = 512
TOK_TM = 512
ATT_TQ = 256
ATT_TK = 256
DEC_TK = 512


def _cparams(sem):
    return pltpu.CompilerParams(dimension_semantics=sem, vmem_limit_bytes=VMEM_LIMIT)


def _rms_scale(x):
    return lax.rsqrt(jnp.mean(x * x, axis=-1, keepdims=True) + EPS)


def _dot(a, b):
    return jnp.dot(a, b, preferred_element_type=F32)


def _split2(x):
    hi = x.astype(BF16)
    lo = (x - hi.astype(F32)).astype(BF16)
    return hi, lo


def _ffn_kernel(x_ref, g_ref, wg_ref, wu_ref, wd_ref, g2_ref, o_ref, o2_ref, h_scr, acc_scr, *, nf):
    f = pl.program_id(1)

    @pl.when(f == 0)
    def _():
        x = x_ref[...]
        h_scr[...] = (x * _rms_scale(x) * g_ref[...]).astype(BF16)
        acc_scr[...] = jnp.zeros_like(acc_scr)

    h = h_scr[...]
    a = _dot(h, wg_ref[...])
    b = _dot(h, wu_ref[...])
    t = (a * jax.nn.sigmoid(a)) * b
    acc_scr[...] += _dot(t.astype(BF16), wd_ref[...])

    @pl.when(f == nf - 1)
    def _():
        y = x_ref[...] + 0.5 * acc_scr[...]
        o_ref[...] = y
        if o2_ref is not None:
            o2_ref[...] = (y * _rms_scale(y) * g2_ref[...]).astype(BF16)


def _ffn(x, g, wg, wu, wd, g2):
    n, d = x.shape
    dff = wg.shape[1]
    tm, tf = min(FFN_TM, n), min(FFN_TF, dff)
    nf = dff // tf
    with_norm = g2 is not None
    kern = functools.partial(_ffn_kernel, nf=nf)
    if not with_norm:
        kern = lambda x_ref, g_ref, wg_ref, wu_ref, wd_ref, o_ref, h_scr, acc_scr: _ffn_kernel(
            x_ref, g_ref, wg_ref, wu_ref, wd_ref, None, o_ref, None, h_scr, acc_scr, nf=nf)
    row = pl.BlockSpec((tm, d), lambda i, f: (i, 0))
    vec = pl.BlockSpec((1, d), lambda i, f: (0, 0))
    in_specs = [row, vec,
                pl.BlockSpec((d, tf), lambda i, f: (0, f)),
                pl.BlockSpec((d, tf), lambda i, f: (0, f)),
                pl.BlockSpec((tf, d), lambda i, f: (f, 0))]
    args = [x, g.reshape(1, d), wg, wu, wd]
    out_shape = [jax.ShapeDtypeStruct((n, d), F32)]
    out_specs = [row]
    if with_norm:
        in_specs.append(vec)
        args.append(g2.reshape(1, d))
        out_shape.append(jax.ShapeDtypeStruct((n, d), BF16))
        out_specs.append(row)
    res = pl.pallas_call(
        kern, grid=(n // tm, nf), in_specs=in_specs, out_specs=out_specs, out_shape=out_shape,
        scratch_shapes=[pltpu.VMEM((tm, d), BF16), pltpu.VMEM((tm, d), F32)],
        compiler_params=_cparams(("parallel", "arbitrary")), name="ffn")(*args)
    return (res[0], res[1]) if with_norm else (res[0], None)


def _s5_param_kernel(ldt_ref, arc_ref, aic_ref, arr_ref, air_ref, btr_ref, bti_ref, ctr_ref, cti_ref,
                     m_ref, sbr_ref, sbi_ref, car_ref, cai_ref, a16r_ref, a16i_ref):
    t, c = S5_CHUNK, SSM_GROUP
    dt = jnp.exp(ldt_ref[0])

    def cpow(ar, ai, n):
        mag = jnp.exp(ar * dt * n)
        ang = ai * dt * n
        return mag * jnp.cos(ang), mag * jnp.sin(ang)

    arc, aic = arc_ref[0], aic_ref[0]
    arr, air = arr_ref[0], air_ref[0]
    p = arc.shape[0]
    jidx = (lax.broadcasted_iota(jnp.int32, (p, t * c), 1) // c).astype(F32)
    cr, ci = ctr_ref[0], cti_ref[0]
    pr, pi = cpow(arc, aic, jidx + 1.0)
    car_ref[0] = (cr * pr - ci * pi).astype(BF16)
    cai_ref[0] = (-(cr * pi + ci * pr)).astype(BF16)
    pr, pi = cpow(arc, aic, jidx)
    csr = cr * pr - ci * pi
    csi = cr * pi + ci * pr

    abr, abi = cpow(arr, air, 1.0)
    xr, xi = abr - 1.0, abi
    den = arr * arr + air * air
    qr = (xr * arr + xi * air) / den
    qi = (xi * arr - xr * air) / den
    btr, bti = btr_ref[0], bti_ref[0]
    bbr = qr * btr - qi * bti
    bbi = qr * bti + qi * btr
    iidx = (lax.broadcasted_iota(jnp.int32, (t * c, p), 0) // c).astype(F32)
    pr, pi = cpow(arr, air, (t - 1.0) - iidx)
    sbr_ref[0] = (bbr * pr - bbi * pi).astype(BF16)
    sbi_ref[0] = (bbr * pi + bbi * pr).astype(BF16)
    pr, pi = cpow(arr, air, float(t))
    a16r_ref[0] = pr
    a16i_ref[0] = pi

    hp = lax.Precision.HIGHEST
    kt = (jnp.dot(bbr[:c], csr, precision=hp, preferred_element_type=F32)
          - jnp.dot(bbi[:c], csi, precision=hp, preferred_element_type=F32))
    lane = lax.broadcasted_iota(jnp.int32, (c, t * c), 1)
    for i in range(t):
        blk = kt if i == 0 else pltpu.roll(kt, i * c, 1)
        m_ref[0, i * c:(i + 1) * c, :] = jnp.where(lane >= i * c, blk, 0.0).astype(BF16)


def _s5_params(log_dt, a_re, a_im, b_re, b_im, c_re, c_im):
    g, p = a_re.shape
    t, c = S5_CHUNK, SSM_GROUP
    tc = t * c

    def tile_b(b):
        return jnp.broadcast_to(jnp.swapaxes(b, 1, 2)[:, None], (g, t, c, p)).reshape(g, tc, p)

    def tile_c(cm):
        return jnp.broadcast_to(jnp.swapaxes(cm, 1, 2)[:, :, None], (g, p, t, c)).reshape(g, p, tc)

    def spec(*shape):
        return pl.BlockSpec((1,) + shape, lambda i: (i, 0, 0))

    return pl.pallas_call(
        _s5_param_kernel, grid=(g,),
        in_specs=[spec(1, 1), spec(p, 1), spec(p, 1), spec(1, p), spec(1, p),
                  spec(tc, p), spec(tc, p), spec(p, tc), spec(p, tc)],
        out_specs=[spec(tc, tc), spec(tc, p), spec(tc, p), spec(p, tc), spec(p, tc), spec(1, p), spec(1, p)],
        out_shape=[jax.ShapeDtypeStruct((g, tc, tc), BF16),
                   jax.ShapeDtypeStruct((g, tc, p), BF16), jax.ShapeDtypeStruct((g, tc, p), BF16),
                   jax.ShapeDtypeStruct((g, p, tc), BF16), jax.ShapeDtypeStruct((g, p, tc), BF16),
                   jax.ShapeDtypeStruct((g, 1, p), F32), jax.ShapeDtypeStruct((g, 1, p), F32)],
        compiler_params=_cparams(("parallel",)), name="s5_params")(
            log_dt.reshape(g, 1, 1), a_re.reshape(g, p, 1), a_im.reshape(g, p, 1),
            a_re.reshape(g, 1, p), a_im.reshape(g, 1, p),
            tile_b(b_re), tile_b(b_im), tile_c(c_re), tile_c(c_im))


def _gelu_tanh(y):
    return 0.5 * y * (1.0 + jnp.tanh(math.sqrt(2.0 / math.pi) * (y + 0.044715 * (y * y * y))))


def _s5_main_kernel(u_ref, m_ref, sbr_ref, sbi_ref, car_ref, cai_ref, a16r_ref, a16i_ref,
                    h0r_ref, h0i_ref, d_ref, z_ref, hpr_ref, hpi_ref, hsr_ref, hsi_ref,
                    sr_scr, si_scr, hr_scr, hi_scr, *, bp, nkp, bs, nks):
    u = u_ref[0]
    sr_scr[...] = _dot(u, sbr_ref[0])
    si_scr[...] = _dot(u, sbi_ref[0])
    ar, ai = a16r_ref[0], a16i_ref[0]

    def step(off, nb, hr, hi):
        rows = pl.ds(off, nb)
        hr_scr[rows, :] = hr
        hi_scr[rows, :] = hi
        return (ar * hr - ai * hi + sr_scr[rows, :], ar * hi + ai * hr + si_scr[rows, :])

    def prompt_body(k, carry):
        return step(pl.multiple_of(k * bp, bp), bp, *carry)

    zero = jnp.zeros((bp, ar.shape[1]), F32)
    hr, hi = lax.fori_loop(0, nkp, prompt_body, (zero, zero), unroll=8)
    hpr_ref[0] = hr
    hpi_ref[0] = hi
    hr, hi = h0r_ref[0], h0i_ref[0]
    for k in range(nks):
        hr, hi = step(nkp * bp + k * bs, bs, hr, hi)
    hsr_ref[0] = hr
    hsi_ref[0] = hi

    y = (_dot(u, m_ref[0]) + _dot(hr_scr[...].astype(BF16), car_ref[0])
         + _dot(hi_scr[...].astype(BF16), cai_ref[0]) + d_ref[0] * u.astype(F32))
    z_ref[0] = _gelu_tanh(y).astype(BF16)


def _s5_layer(u, n_prompt, bp, bs, h0_re, h0_im, mats, d_skip):
    n, d = u.shape
    t, c = S5_CHUNK, SSM_GROUP
    g, tc = d // c, t * c
    p = h0_re.shape[-1]
    nkp = n_prompt // (bp * t)
    nks = (n - n_prompt) // (bs * t)
    rp, rs = nkp * bp, nks * bs
    r = rp + rs
    assert bp % 8 == 0 and bs % 8 == 0

    def to_rows(x, b, nk):
        return x.reshape(b, nk, t, g, c).transpose(3, 1, 0, 2, 4).reshape(g, nk * b, tc)

    def from_rows(x, b, nk):
        return x.reshape(g, nk, b, t, c).transpose(2, 1, 3, 0, 4).reshape(b * nk * t, d)

    u2 = jnp.concatenate([to_rows(u[:n_prompt], bp, nkp), to_rows(u[n_prompt:], bs, nks)], axis=1)
    m, sbr, sbi, car, cai, a16r, a16i = mats
    d_t = jnp.broadcast_to(d_skip.reshape(g, 1, 1, c), (g, 1, t, c)).reshape(g, 1, tc)
    h0r = jnp.swapaxes(h0_re, 0, 1)
    h0i = jnp.swapaxes(h0_im, 0, 1)

    def spec(*shape):
        return pl.BlockSpec((1,) + shape, lambda i: (i, 0, 0))

    kern = functools.partial(_s5_main_kernel, bp=bp, nkp=nkp, bs=bs, nks=nks)
    z2, hpr, hpi, hsr, hsi = pl.pallas_call(
        kern, grid=(g,),
        in_specs=[spec(r, tc), spec(tc, tc), spec(tc, p), spec(tc, p), spec(p, tc), spec(p, tc),
                  spec(1, p), spec(1, p), spec(bs, p), spec(bs, p), spec(1, tc)],
        out_specs=[spec(r, tc), spec(bp, p), spec(bp, p), spec(bs, p), spec(bs, p)],
        out_shape=[jax.ShapeDtypeStruct((g, r, tc), BF16),
                   jax.ShapeDtypeStruct((g, bp, p), F32), jax.ShapeDtypeStruct((g, bp, p), F32),
                   jax.ShapeDtypeStruct((g, bs, p), F32), jax.ShapeDtypeStruct((g, bs, p), F32)],
        scratch_shapes=[pltpu.VMEM((r, p), F32) for _ in range(4)],
        compiler_params=_cparams(("parallel",)), name="s5_main")(
            u2, m, sbr, sbi, car, cai, a16r, a16i, h0r, h0i, d_t)
    z = jnp.concatenate([from_rows(z2[:, :rp], bp, nkp), from_rows(z2[:, rp:], bs, nks)], axis=0)
    states = tuple(jnp.swapaxes(h, 0, 1)[:, None] for h in (hpr, hpi, hsr, hsi))
    return z, states


def _glu_kernel(x_ref, z_ref, wa_ref, wb_ref, o_ref):
    z = z_ref[...]
    o_ref[...] = x_ref[...] + _dot(z, wa_ref[...]) * jax.nn.sigmoid(_dot(z, wb_ref[...]))


def _glu(x, z, wa, wb):
    n, d = x.shape
    tm = min(TOK_TM, n)
    row = pl.BlockSpec((tm, d), lambda i: (i, 0))
    mat = pl.BlockSpec((d, d), lambda i: (0, 0))
    return pl.pallas_call(
        _glu_kernel, grid=(n // tm,), in_specs=[row, row, mat, mat], out_specs=row,
        out_shape=jax.ShapeDtypeStruct((n, d), F32),
        compiler_params=_cparams(("parallel",)), name="glu")(x, z, wa, wb)


def _head_norm(x, e_ref, et_ref):
    hi, lo = _split2(x * x)
    ms = (_dot(hi, e_ref[...]) + _dot(lo, e_ref[...])) * (1.0 / HEAD_DIM)
    hi, lo = _split2(lax.rsqrt(ms + EPS))
    return _dot(hi, et_ref[...]) + _dot(lo, et_ref[...])


def _head_indicator(d):
    nh = d // HEAD_DIM
    e = (np.arange(d)[:, None] // HEAD_DIM == np.arange(LANES)[None, :]).astype(np.float32)
    assert nh <= LANES
    return jnp.asarray(e, BF16), jnp.asarray(e.T, BF16)


def _kvf_kernel(h_ref, wkv_ref, wf_ref, bf_ref, kn_ref, e_ref, et_ref,
                k_ref, v_ref, kb_ref, vb_ref, lf_ref, *, d, nh):
    h = h_ref[...]
    pkv = _dot(h, wkv_ref[...])
    kraw, v = pkv[:, :d], pkv[:, d:]
    k = kraw * _head_norm(kraw, e_ref, et_ref) * kn_ref[...]
    k_ref[...] = k
    v_ref[...] = v
    kb_ref[...] = k.astype(BF16)
    vb_ref[...] = v.astype(BF16)
    pf = _dot(h, wf_ref[...]) + bf_ref[...]
    lf = jnp.minimum(pf, 0.0) - jnp.log1p(jnp.exp(-jnp.abs(pf)))
    lf_ref[...] = lf[:, :nh]


def _kvf(hn, w_kv, w_f, b_f, k_norm):
    n, d = hn.shape
    nh = d // HEAD_DIM
    tm = min(TOK_TM, n)
    e, et = _head_indicator(d)
    row = pl.BlockSpec((tm, d), lambda i: (i, 0))
    const = lambda *s: pl.BlockSpec(s, lambda i: (0, 0))
    kern = functools.partial(_kvf_kernel, d=d, nh=nh)
    return pl.pallas_call(
        kern, grid=(n // tm,),
        in_specs=[row, const(d, 2 * d), const(d, LANES), const(1, LANES), const(1, d), const(d, LANES), const(LANES, d)],
        out_specs=[row, row, row, row, pl.BlockSpec((tm, nh), lambda i: (i, 0))],
        out_shape=[jax.ShapeDtypeStruct((n, d), F32), jax.ShapeDtypeStruct((n, d), F32),
                   jax.ShapeDtypeStruct((n, d), BF16), jax.ShapeDtypeStruct((n, d), BF16),
                   jax.ShapeDtypeStruct((n, nh), F32)],
        compiler_params=_cparams(("parallel",)), name="kvf")(
            hn, w_kv, jnp.pad(w_f, ((0, 0), (0, LANES - nh))), jnp.pad(b_f, (0, LANES - nh)).reshape(1, LANES),
            jnp.tile(k_norm, nh).reshape(1, d), e, et)


def _qproj_kernel(u_ref, wq_ref, qn_ref, e_ref, et_ref, q_ref):
    q = _dot(u_ref[...], wq_ref[...])
    q_ref[...] = (q * _head_norm(q, e_ref, et_ref) * qn_ref[...]).astype(BF16)


def _qproj(un, wq, qn_scaled):
    n, d = un.shape
    tm = min(TOK_TM, n)
    e, et = _head_indicator(d)
    row = pl.BlockSpec((tm, d), lambda i: (i, 0))
    const = lambda *s: pl.BlockSpec(s, lambda i: (0, 0))
    return pl.pallas_call(
        _qproj_kernel, grid=(n // tm,),
        in_specs=[row, const(d, d), const(1, d), const(d, LANES), const(LANES, d)], out_specs=row,
        out_shape=jax.ShapeDtypeStruct((n, d), BF16),
        compiler_params=_cparams(("parallel",)), name="qproj")(un, wq, qn_scaled, e, et)


def _oproj_kernel(x_ref, a_ref, wo_ref, o_ref):
    o_ref[...] = x_ref[...] + _dot(a_ref[...], wo_ref[...])


def _oproj(x, a, wo):
    n, d = x.shape
    tm = min(TOK_TM, n)
    row = pl.BlockSpec((tm, d), lambda i: (i, 0))
    return pl.pallas_call(
        _oproj_kernel, grid=(n // tm,), in_specs=[row, row, pl.BlockSpec((d, d), lambda i: (0, 0))],
        out_specs=row, out_shape=jax.ShapeDtypeStruct((n, d), F32),
        compiler_params=_cparams(("parallel",)), name="oproj")(x, a, wo)


def _cumsum_kernel(x_ref, init_ref, o_ref, *, nblk, bw):
    tri = (lax.broadcasted_iota(jnp.int32, (bw, bw), 0) <= lax.broadcasted_iota(jnp.int32, (bw, bw), 1)).astype(BF16)
    carry = init_ref[...]
    for j in range(nblk):
        x = x_ref[:, j * bw:(j + 1) * bw]
        hi = x.astype(BF16)
        r1 = x - hi.astype(F32)
        mid = r1.astype(BF16)
        lo = (r1 - mid.astype(F32)).astype(BF16)
        cs = _dot(hi, tri) + _dot(mid, tri) + _dot(lo, tri) + carry
        o_ref[:, j * bw:(j + 1) * bw] = cs
        carry = cs[:, bw - 1:bw]


def _cumsum_rows(x, init):
    r, w = x.shape
    bw = min(256, w)
    kern = functools.partial(_cumsum_kernel, nblk=w // bw, bw=bw)
    return pl.pallas_call(
        kern, grid=(1,),
        in_specs=[pl.BlockSpec((r, w), lambda i: (0, 0)), pl.BlockSpec((r, 1), lambda i: (0, 0))],
        out_specs=pl.BlockSpec((r, w), lambda i: (0, 0)), out_shape=jax.ShapeDtypeStruct((r, w), F32),
        compiler_params=_cparams(("arbitrary",)), name="cumsum")(x, init)


def _flash_slab(hp, q2, ks, vs, cq, ck, causal, m_scr, l_scr, acc_scr):
    tq = q2.shape[0] // 2
    tk = ks.shape[0]
    s = lax.dot_general(q2, ks, (((1,), (1,)), ((), ())), preferred_element_type=F32)
    ha, hb = 2 * hp, 2 * hp + 1
    bias = jnp.concatenate([cq[:, ha:ha + 1] - ck[ha:ha + 1, :], cq[:, hb:hb + 1] - ck[hb:hb + 1, :]], axis=0)
    s = s + bias
    if causal:
        ri = lax.broadcasted_iota(jnp.int32, (2 * tq, tk), 0)
        ri = jnp.where(ri >= tq, ri - tq, ri)
        s = jnp.where(lax.broadcasted_iota(jnp.int32, (2 * tq, tk), 1) <= ri, s, NEG_BIG)
    m_prev = m_scr[hp]
    m_new = jnp.maximum(m_prev, jnp.max(s, axis=1, keepdims=True))
    alpha = jnp.exp(m_prev - m_new)
    pe = jnp.exp(s - m_new)
    l_scr[hp] = alpha * l_scr[hp] + jnp.sum(pe, axis=1, keepdims=True)
    acc_scr[hp] = alpha * acc_scr[hp] + _dot(pe.astype(BF16), vs)
    m_scr[hp] = m_new


def _stack_heads(qs):
    lo = lax.broadcasted_iota(jnp.int32, qs.shape, 1) < HEAD_DIM
    zero = jnp.zeros_like(qs)
    return jnp.concatenate([jnp.where(lo, qs, zero), jnp.where(lo, zero, qs)], axis=0)


def _flash_init(m_scr, l_scr, acc_scr):
    m_scr[...] = jnp.full_like(m_scr, NEG_BIG)
    l_scr[...] = jnp.zeros_like(l_scr)
    acc_scr[...] = jnp.zeros_like(acc_scr)


def _flash_finish(o_ref, nhp, m_scr, l_scr, acc_scr):
    tq = o_ref.shape[0]
    lo = lax.broadcasted_iota(jnp.int32, (tq, LANES), 1) < HEAD_DIM
    for hp in range(nhp):
        o = acc_scr[hp] / l_scr[hp]
        o_ref[:, hp * LANES:(hp + 1) * LANES] = jnp.where(lo, o[:tq], o[tq:]).astype(o_ref.dtype)


def _attn_prompt_kernel(qi_tab, ki_tab, q_ref, k_ref, v_ref, cq_ref, ck_ref, o_ref, m_scr, l_scr, acc_scr, *, nhp):
    step = pl.program_id(1)
    qi, ki = qi_tab[step], ki_tab[step]

    @pl.when(ki == 0)
    def _():
        _flash_init(m_scr, l_scr, acc_scr)

    def sweep(causal):
        cq, ck = cq_ref[...], ck_ref[0]
        for hp in range(nhp):
            sl = slice(hp * LANES, (hp + 1) * LANES)
            _flash_slab(hp, _stack_heads(q_ref[:, sl]), k_ref[:, sl], v_ref[:, sl], cq, ck, causal,
                        m_scr, l_scr, acc_scr)

    @pl.when(ki < qi)
    def _():
        sweep(False)

    @pl.when(ki == qi)
    def _():
        sweep(True)
        _flash_finish(o_ref, nhp, m_scr, l_scr, acc_scr)


def _attn_prompt(q, kb, vb, c_rows, c_t, nb, s):
    d = q.shape[1]
    nh = d // HEAD_DIM
    nhp = d // LANES
    tq = tk = min(ATT_TQ, s)
    nq = s // tq
    pairs = [(i, j) for i in range(nq) for j in range(i + 1)]
    qi_tab = jnp.asarray([p[0] for p in pairs], jnp.int32)
    ki_tab = jnp.asarray([p[1] for p in pairs], jnp.int32)
    grid_spec = pltpu.PrefetchScalarGridSpec(
        num_scalar_prefetch=2, grid=(nb, len(pairs)),
        in_specs=[pl.BlockSpec((tq, d), lambda b, p, qt, kt: (b * nq + qt[p], 0)),
                  pl.BlockSpec((tk, d), lambda b, p, qt, kt: (b * nq + kt[p], 0)),
                  pl.BlockSpec((tk, d), lambda b, p, qt, kt: (b * nq + kt[p], 0)),
                  pl.BlockSpec((tq, nh), lambda b, p, qt, kt: (b * nq + qt[p], 0)),
                  pl.BlockSpec((1, nh, tk), lambda b, p, qt, kt: (b, 0, kt[p]))],
        out_specs=pl.BlockSpec((tq, d), lambda b, p, qt, kt: (b * nq + qt[p], 0)),
        scratch_shapes=[pltpu.VMEM((nhp, 2 * tq, 1), F32), pltpu.VMEM((nhp, 2 * tq, 1), F32),
                        pltpu.VMEM((nhp, 2 * tq, LANES), F32)])
    return pl.pallas_call(
        functools.partial(_attn_prompt_kernel, nhp=nhp), grid_spec=grid_spec,
        out_shape=jax.ShapeDtypeStruct((nb * s, d), BF16),
        compiler_params=_cparams(("parallel", "arbitrary")), name="attn_prompt")(
            qi_tab, ki_tab, q, kb, vb, c_rows, c_t)


def _attn_sample_kernel(q_ref, ck_ref, cv_ref, kn_ref, vn_ref, cq_ref, ckc_ref, ckn_ref, o_ref,
                        m_scr, l_scr, acc_scr, *, nhp, nkb):
    j = pl.program_id(1)

    @pl.when(j == 0)
    def _():
        _flash_init(m_scr, l_scr, acc_scr)

    @pl.when(j < nkb)
    def _():
        cq, ck = cq_ref[0], ckc_ref[0]
        for hp in range(nhp):
            sl = slice(hp * LANES, (hp + 1) * LANES)
            _flash_slab(hp, _stack_heads(q_ref[:, sl]), ck_ref[0, :, sl].astype(BF16), cv_ref[0, :, sl].astype(BF16),
                        cq, ck, False, m_scr, l_scr, acc_scr)

    @pl.when(j == nkb)
    def _():
        cq, ck = cq_ref[0], ckn_ref[0]
        for hp in range(nhp):
            sl = slice(hp * LANES, (hp + 1) * LANES)
            _flash_slab(hp, _stack_heads(q_ref[:, sl]), kn_ref[:, sl], vn_ref[:, sl], cq, ck, True,
                        m_scr, l_scr, acc_scr)
        _flash_finish(o_ref, nhp, m_scr, l_scr, acc_scr)


def _attn_sample(q, kb, vb, cache_k, cache_v, c_new, c_past_t, c_new_t, n_prompt):
    d = q.shape[1]
    nb, plen, _ = cache_k.shape
    lq = c_new.shape[1]
    nh, nhp = d // HEAD_DIM, d // LANES
    tk = min(DEC_TK, plen)
    nkb = plen // tk
    r0 = n_prompt // lq
    new_rows = pl.BlockSpec((lq, d), lambda b, j: (r0 + b, 0))
    cache = pl.BlockSpec((1, tk, d), lambda b, j: (b, jnp.minimum(j, nkb - 1), 0))
    kern = functools.partial(_attn_sample_kernel, nhp=nhp, nkb=nkb)
    return pl.pallas_call(
        kern, grid=(nb, nkb + 1),
        in_specs=[new_rows, cache, cache, new_rows, new_rows,
                  pl.BlockSpec((1, lq, nh), lambda b, j: (b, 0, 0)),
                  pl.BlockSpec((1, nh, tk), lambda b, j: (b, 0, jnp.minimum(j, nkb - 1))),
                  pl.BlockSpec((1, nh, lq), lambda b, j: (b, 0, 0))],
        out_specs=pl.BlockSpec((lq, d), lambda b, j: (b, 0)),
        out_shape=jax.ShapeDtypeStruct((nb * lq, d), BF16),
        scratch_shapes=[pltpu.VMEM((nhp, 2 * lq, 1), F32), pltpu.VMEM((nhp, 2 * lq, 1), F32),
                        pltpu.VMEM((nhp, 2 * lq, LANES), F32)],
        compiler_params=_cparams(("parallel", "arbitrary")), name="attn_sample")(
            q, cache_k, cache_v, kb, vb, c_new, c_past_t, c_new_t)


def kernel(x_prompt, x_sample, cache_k, cache_v, cache_logf, state_ssm_re, state_ssm_im, ffn_norm, w_ffn_gate, w_ffn_up, w_ffn_down, mix_norm, ssm_a_re, ssm_a_im, ssm_log_dt, ssm_b_re, ssm_b_im, ssm_c_re, ssm_c_im, ssm_d, w_glu_a, w_glu_b, kv_norm, w_kvf, b_f, k_norm, w_q, q_norm, w_o):
    bp, s, d = x_prompt.shape
    bs, lq, _ = x_sample.shape
    plen = cache_k.shape[1]
    nh = d // HEAD_DIM
    assert ffn_norm.shape[0] == 2 and ssm_a_re.shape[0] == 1 and w_q.shape[0] == 1
    n_prompt = bp * s
    bf = lambda w: w.astype(BF16)

    def ffn(x, l, j, g2):
        return _ffn(x, ffn_norm[l, j], bf(w_ffn_gate[l, j]), bf(w_ffn_up[l, j]), bf(w_ffn_down[l, j]), g2)

    x = jnp.concatenate([x_prompt.reshape(n_prompt, d), x_sample.reshape(bs * lq, d)], axis=0)

    x, u = ffn(x, 0, 0, mix_norm[0])
    mats = _s5_params(ssm_log_dt[0], ssm_a_re[0], ssm_a_im[0], ssm_b_re[0], ssm_b_im[0], ssm_c_re[0], ssm_c_im[0])
    z, (p_re, p_im, s_re, s_im) = _s5_layer(u, n_prompt, bp, bs, state_ssm_re[:, 0], state_ssm_im[:, 0], mats, ssm_d[0])
    x = _glu(x, z, bf(w_glu_a[0]), bf(w_glu_b[0]))
    x, hn = ffn(x, 0, 1, kv_norm)

    k, v, kb, vb, logf = _kvf(hn, bf(w_kvf[:, :2 * d]), bf(w_kvf[:, 2 * d:]), b_f, k_norm)
    logf_p = logf[:n_prompt].reshape(bp, s, nh)
    logf_s = logf[n_prompt:].reshape(bs, lq, nh)
    def cumsum_t(lf, init):
        b, l, _ = lf.shape
        init = jnp.zeros((b * nh, 1), F32) if init is None else init.reshape(b * nh, 1)
        return _cumsum_rows(jnp.swapaxes(lf, 1, 2).reshape(b * nh, l), init).reshape(b, nh, l)

    cp_t = cumsum_t(logf_p, None)
    cpast_t = cumsum_t(cache_logf.astype(F32), None)
    cnew_t = cumsum_t(logf_s, cpast_t[:, :, -1])

    x, un = ffn(x, 1, 0, mix_norm[1])
    q = _qproj(un, bf(w_q[0]), (jnp.tile(q_norm[0], nh) * (1.0 / math.sqrt(HEAD_DIM))).reshape(1, d))
    a_p = _attn_prompt(q, kb, vb, jnp.swapaxes(cp_t, 1, 2).reshape(n_prompt, nh), cp_t, bp, s)
    a_s = _attn_sample(q, kb, vb, cache_k.reshape(bs, plen, d), cache_v.reshape(bs, plen, d),
                       jnp.swapaxes(cnew_t, 1, 2), cpast_t, cnew_t, n_prompt)
    x = _oproj(x, jnp.concatenate([a_p, a_s], axis=0), bf(w_o[0]))
    x, _ = ffn(x, 1, 1, None)

    def split(a, tail):
        return a[:n_prompt].reshape((bp, s) + tail), a[n_prompt:].reshape((bs, lq) + tail)

    y_p, y_s = split(x, (d,))
    k_p, k_s = split(k, (nh, HEAD_DIM))
    v_p, v_s = split(v, (nh, HEAD_DIM))
    return (y_p, y_s, p_re, p_im, k_p, v_p, logf_p, s_re, s_im, k_s, v_s, logf_s)
```

```python
import functools
import math

import jax
import jax.numpy as jnp
import numpy as np
from jax import lax
from jax.experimental import pallas as pl
from jax.experimental.pallas import tpu as pltpu

F32 = jnp.float32
BF16 = jnp.bfloat16

EPS = 1e-6
HEAD_DIM = 64
SSM_GROUP = 16
S5_CHUNK = 16
LANES = 128
NEG_BIG = -1e30
VMEM_LIMIT = 52 * 1024 * 1024

FFN_TM = 1024
FFN_TF = 512
TOK_TM = 512
ATT_TQ = 256
ATT_TK = 256
ATT_HS = 2
DEC_TK = 512
DEC_HS = 4


def _cparams(sem):
    return pltpu.CompilerParams(dimension_semantics=sem, vmem_limit_bytes=VMEM_LIMIT)


def _rms_scale(x):
    return lax.rsqrt(jnp.mean(x * x, axis=-1, keepdims=True) + EPS)


def _dot(a, b):
    return jnp.dot(a, b, preferred_element_type=F32)


def _split2(x):
    hi = x.astype(BF16)
    lo = (x - hi.astype(F32)).astype(BF16)
    return hi, lo


def _part_maps(npb):
    first = lambda i, *_: (jnp.minimum(i, npb - 1), 0)
    second = lambda i, *_: (jnp.maximum(i - npb, 0), 0)
    return first, second


def _ffn_kernel(*refs, nf, npb, n_in, n_out, with_norm):
    x_refs, refs = refs[:n_in], refs[n_in:]
    g_ref, wg_ref, wu_ref, wd_ref = refs[:4]
    refs = refs[4:]
    g2_ref = refs[0] if with_norm else None
    refs = refs[with_norm:]
    o_refs, refs = refs[:n_out], refs[n_out:]
    o2_ref = refs[0] if with_norm else None
    h_scr, acc_scr = refs[with_norm:]
    i, f = pl.program_id(0), pl.program_id(1)

    def load_x():
        if n_in == 1:
            return x_refs[0][...]
        return jnp.where(i < npb, x_refs[0][...], x_refs[1][...])

    @pl.when(f == 0)
    def _():
        x = load_x()
        h_scr[...] = (x * _rms_scale(x) * g_ref[...]).astype(BF16)
        acc_scr[...] = jnp.zeros_like(acc_scr)

    h = h_scr[...]
    a = _dot(h, wg_ref[...])
    b = _dot(h, wu_ref[...])
    t = (a * jax.nn.sigmoid(a)) * b
    acc_scr[...] += _dot(t.astype(BF16), wd_ref[...])

    @pl.when(f == nf - 1)
    def _():
        y = load_x() + 0.5 * acc_scr[...]
        if n_out == 1:
            o_refs[0][...] = y
        else:
            @pl.when(i < npb)
            def _():
                o_refs[0][...] = y

            @pl.when(i >= npb)
            def _():
                o_refs[1][...] = y
        if with_norm:
            o2_ref[...] = (y * _rms_scale(y) * g2_ref[...]).astype(o2_ref.dtype)


def _ffn(xs, g, wg, wu, wd, g2=None, split_out=None):
    d = xs[0].shape[1]
    n = sum(x.shape[0] for x in xs)
    dff = wg.shape[1]
    tm, tf = min(FFN_TM, n), min(FFN_TF, dff)
    nf = dff // tf
    n_first = xs[0].shape[0] if len(xs) == 2 else (split_out[0] if split_out else n)
    assert all(x.shape[0] % tm == 0 for x in xs) and n_first % tm == 0
    npb = n_first // tm
    first, second = _part_maps(npb)
    with_norm = g2 is not None
    row = pl.BlockSpec((tm, d), lambda i, f: (i, 0))
    vec = pl.BlockSpec((1, d), lambda i, f: (0, 0))
    parts = [pl.BlockSpec((tm, d), first), pl.BlockSpec((tm, d), second)]
    in_specs = (parts if len(xs) == 2 else [row]) + [
        vec, pl.BlockSpec((d, tf), lambda i, f: (0, f)), pl.BlockSpec((d, tf), lambda i, f: (0, f)),
        pl.BlockSpec((tf, d), lambda i, f: (f, 0))]
    args = list(xs) + [g.reshape(1, d), wg, wu, wd]
    if split_out:
        out_shape = [jax.ShapeDtypeStruct((m, d), F32) for m in split_out]
        out_specs = list(parts)
    else:
        out_shape = [jax.ShapeDtypeStruct((n, d), F32)]
        out_specs = [row]
    n_out = len(out_shape)
    if with_norm:
        in_specs.append(vec)
        args.append(g2.reshape(1, d))
        out_shape.append(jax.ShapeDtypeStruct((n, d), BF16))
        out_specs.append(row)
    kern = functools.partial(_ffn_kernel, nf=nf, npb=npb, n_in=len(xs), n_out=n_out, with_norm=with_norm)
    res = pl.pallas_call(
        kern, grid=(n // tm, nf), in_specs=in_specs, out_specs=out_specs, out_shape=out_shape,
        scratch_shapes=[pltpu.VMEM((tm, d), BF16), pltpu.VMEM((tm, d), F32)],
        compiler_params=_cparams(("arbitrary", "arbitrary")), name="ffn")(*args)
    y = tuple(res[:n_out]) if split_out else res[0]
    return y, (res[n_out] if with_norm else None)


def _s5_param_kernel(ldt_ref, arc_ref, aic_ref, arr_ref, air_ref, btr_ref, bti_ref, ctr_ref, cti_ref,
                     m_ref, sbr_ref, sbi_ref, car_ref, cai_ref, a16r_ref, a16i_ref):
    t, c = S5_CHUNK, SSM_GROUP
    dt = jnp.exp(ldt_ref[0])

    def cpow(ar, ai, n):
        mag = jnp.exp(ar * dt * n)
        ang = ai * dt * n
        return mag * jnp.cos(ang), mag * jnp.sin(ang)

    arc, aic = arc_ref[0], aic_ref[0]
    arr, air = arr_ref[0], air_ref[0]
    p = arc.shape[0]
    jidx = (lax.broadcasted_iota(jnp.int32, (p, t * c), 1) // c).astype(F32)
    cr, ci = ctr_ref[0], cti_ref[0]
    pr, pi = cpow(arc, aic, jidx + 1.0)
    car_ref[0] = (cr * pr - ci * pi).astype(BF16)
    cai_ref[0] = (-(cr * pi + ci * pr)).astype(BF16)
    pr, pi = cpow(arc, aic, jidx)
    csr = cr * pr - ci * pi
    csi = cr * pi + ci * pr

    abr, abi = cpow(arr, air, 1.0)
    xr, xi = abr - 1.0, abi
    den = arr * arr + air * air
    qr = (xr * arr + xi * air) / den
    qi = (xi * arr - xr * air) / den
    btr, bti = btr_ref[0], bti_ref[0]
    bbr = qr * btr - qi * bti
    bbi = qr * bti + qi * btr
    iidx = (lax.broadcasted_iota(jnp.int32, (t * c, p), 0) // c).astype(F32)
    pr, pi = cpow(arr, air, (t - 1.0) - iidx)
    sbr_ref[0] = (bbr * pr - bbi * pi).astype(BF16)
    sbi_ref[0] = (bbr * pi + bbi * pr).astype(BF16)
    pr, pi = cpow(arr, air, float(t))
    a16r_ref[0] = pr
    a16i_ref[0] = pi

    hp = lax.Precision.HIGHEST
    kt = (jnp.dot(bbr[:c], csr, precision=hp, preferred_element_type=F32)
          - jnp.dot(bbi[:c], csi, precision=hp, preferred_element_type=F32))
    lane = lax.broadcasted_iota(jnp.int32, (c, t * c), 1)
    for i in range(t):
        blk = kt if i == 0 else pltpu.roll(kt, i * c, 1)
        m_ref[0, i * c:(i + 1) * c, :] = jnp.where(lane >= i * c, blk, 0.0).astype(BF16)


def _s5_params(log_dt, a_re, a_im, b_re, b_im, c_re, c_im):
    g, p = a_re.shape
    t, c = S5_CHUNK, SSM_GROUP
    tc = t * c

    def tile_b(b):
        return jnp.broadcast_to(jnp.swapaxes(b, 1, 2)[:, None], (g, t, c, p)).reshape(g, tc, p)

    def tile_c(cm):
        return jnp.broadcast_to(jnp.swapaxes(cm, 1, 2)[:, :, None], (g, p, t, c)).reshape(g, p, tc)

    def spec(*shape):
        return pl.BlockSpec((1,) + shape, lambda i: (i, 0, 0))

    return pl.pallas_call(
        _s5_param_kernel, grid=(g,),
        in_specs=[spec(1, 1), spec(p, 1), spec(p, 1), spec(1, p), spec(1, p),
                  spec(tc, p), spec(tc, p), spec(p, tc), spec(p, tc)],
        out_specs=[spec(tc, tc), spec(tc, p), spec(tc, p), spec(p, tc), spec(p, tc), spec(1, p), spec(1, p)],
        out_shape=[jax.ShapeDtypeStruct((g, tc, tc), BF16),
                   jax.ShapeDtypeStruct((g, tc, p), BF16), jax.ShapeDtypeStruct((g, tc, p), BF16),
                   jax.ShapeDtypeStruct((g, p, tc), BF16), jax.ShapeDtypeStruct((g, p, tc), BF16),
                   jax.ShapeDtypeStruct((g, 1, p), F32), jax.ShapeDtypeStruct((g, 1, p), F32)],
        compiler_params=_cparams(("parallel",)), name="s5_params")(
            log_dt.reshape(g, 1, 1), a_re.reshape(g, p, 1), a_im.reshape(g, p, 1),
            a_re.reshape(g, 1, p), a_im.reshape(g, 1, p),
            tile_b(b_re), tile_b(b_im), tile_c(c_re), tile_c(c_im))


def _gelu_tanh(y):
    return 0.5 * y * (1.0 + jnp.tanh(math.sqrt(2.0 / math.pi) * (y + 0.044715 * (y * y * y))))


def _s5_main_kernel(u_ref, m_ref, sbr_ref, sbi_ref, car_ref, cai_ref, a16r_ref, a16i_ref,
                    h0r_ref, h0i_ref, d_ref, z_ref, hpr_ref, hpi_ref, hsr_ref, hsi_ref,
                    sr_scr, si_scr, hr_scr, hi_scr, *, bp, nkp, bs, nks):
    u = u_ref[0]
    sr_scr[...] = _dot(u, sbr_ref[0])
    si_scr[...] = _dot(u, sbi_ref[0])
    ar, ai = a16r_ref[0], a16i_ref[0]

    def step(off, nb, hr, hi):
        rows = pl.ds(off, nb)
        hr_scr[rows, :] = hr
        hi_scr[rows, :] = hi
        return (ar * hr - ai * hi + sr_scr[rows, :], ar * hi + ai * hr + si_scr[rows, :])

    def prompt_body(k, carry):
        return step(pl.multiple_of(k * bp, bp), bp, *carry)

    zero = jnp.zeros((bp, ar.shape[1]), F32)
    hr, hi = lax.fori_loop(0, nkp, prompt_body, (zero, zero), unroll=8)
    hpr_ref[0] = hr
    hpi_ref[0] = hi
    hr, hi = h0r_ref[0], h0i_ref[0]
    for k in range(nks):
        hr, hi = step(nkp * bp + k * bs, bs, hr, hi)
    hsr_ref[0] = hr
    hsi_ref[0] = hi

    y = (_dot(u, m_ref[0]) + _dot(hr_scr[...].astype(BF16), car_ref[0])
         + _dot(hi_scr[...].astype(BF16), cai_ref[0]) + d_ref[0] * u.astype(F32))
    z_ref[0] = _gelu_tanh(y).astype(BF16)


def _s5_layer(u, n_prompt, bp, bs, h0_re, h0_im, mats, d_skip):
    n, d = u.shape
    t, c = S5_CHUNK, SSM_GROUP
    g, tc = d // c, t * c
    p = h0_re.shape[-1]
    nkp = n_prompt // (bp * t)
    nks = (n - n_prompt) // (bs * t)
    rp, rs = nkp * bp, nks * bs
    r = rp + rs
    assert bp % 8 == 0 and bs % 8 == 0

    def to_rows(x, b, nk):
        return x.reshape(b, nk, t, g, c).transpose(3, 1, 0, 2, 4).reshape(g, nk * b, tc)

    def from_rows(x, b, nk):
        return x.reshape(g, nk, b, t, c).transpose(2, 1, 3, 0, 4).reshape(b * nk * t, d)

    u2 = jnp.concatenate([to_rows(u[:n_prompt], bp, nkp), to_rows(u[n_prompt:], bs, nks)], axis=1)
    m, sbr, sbi, car, cai, a16r, a16i = mats
    d_t = jnp.broadcast_to(d_skip.reshape(g, 1, 1, c), (g, 1, t, c)).reshape(g, 1, tc)
    h0r = jnp.swapaxes(h0_re, 0, 1)
    h0i = jnp.swapaxes(h0_im, 0, 1)

    def spec(*shape):
        return pl.BlockSpec((1,) + shape, lambda i: (i, 0, 0))

    kern = functools.partial(_s5_main_kernel, bp=bp, nkp=nkp, bs=bs, nks=nks)
    z2, hpr, hpi, hsr, hsi = pl.pallas_call(
        kern, grid=(g,),
        in_specs=[spec(r, tc), spec(tc, tc), spec(tc, p), spec(tc, p), spec(p, tc), spec(p, tc),
                  spec(1, p), spec(1, p), spec(bs, p), spec(bs, p), spec(1, tc)],
        out_specs=[spec(r, tc), spec(bp, p), spec(bp, p), spec(bs, p), spec(bs, p)],
        out_shape=[jax.ShapeDtypeStruct((g, r, tc), BF16),
                   jax.ShapeDtypeStruct((g, bp, p), F32), jax.ShapeDtypeStruct((g, bp, p), F32),
                   jax.ShapeDtypeStruct((g, bs, p), F32), jax.ShapeDtypeStruct((g, bs, p), F32)],
        scratch_shapes=[pltpu.VMEM((r, p), F32) for _ in range(4)],
        compiler_params=_cparams(("parallel",)), name="s5_main")(
            u2, m, sbr, sbi, car, cai, a16r, a16i, h0r, h0i, d_t)
    z = jnp.concatenate([from_rows(z2[:, :rp], bp, nkp), from_rows(z2[:, rp:], bs, nks)], axis=0)
    states = tuple(jnp.swapaxes(h, 0, 1)[:, None] for h in (hpr, hpi, hsr, hsi))
    return z, states


def _glu_kernel(x_ref, z_ref, wa_ref, wb_ref, o_ref):
    z = z_ref[...]
    o_ref[...] = x_ref[...] + _dot(z, wa_ref[...]) * jax.nn.sigmoid(_dot(z, wb_ref[...]))


def _glu(x, z, wa, wb):
    n, d = x.shape
    tm = min(TOK_TM, n)
    row = pl.BlockSpec((tm, d), lambda i: (i, 0))
    mat = pl.BlockSpec((d, d), lambda i: (0, 0))
    return pl.pallas_call(
        _glu_kernel, grid=(n // tm,), in_specs=[row, row, mat, mat], out_specs=row,
        out_shape=jax.ShapeDtypeStruct((n, d), F32),
        compiler_params=_cparams(("parallel",)), name="glu")(x, z, wa, wb)


def _head_norm(x, e_ref, et_ref):
    hi, lo = _split2(x * x)
    ms = (_dot(hi, e_ref[...]) + _dot(lo, e_ref[...])) * (1.0 / HEAD_DIM)
    hi, lo = _split2(lax.rsqrt(ms + EPS))
    return _dot(hi, et_ref[...]) + _dot(lo, et_ref[...])


def _head_indicator(d):
    nh = d // HEAD_DIM
    e = (np.arange(d)[:, None] // HEAD_DIM == np.arange(LANES)[None, :]).astype(np.float32)
    assert nh <= LANES
    return jnp.asarray(e, BF16), jnp.asarray(e.T, BF16)


def _kvf_kernel(h_ref, wkv_ref, wf_ref, bf_ref, kn_ref, e_ref, et_ref,
                kp_ref, ks_ref, vp_ref, vs_ref, lp_ref, ls_ref, kb_ref, vb_ref, *, d, nh, npb):
    i = pl.program_id(0)
    h = h_ref[...]
    tm = h.shape[0]
    pkv = _dot(h, wkv_ref[...])
    kraw, v = pkv[:, :d], pkv[:, d:]
    k = kraw * _head_norm(kraw, e_ref, et_ref) * kn_ref[...]
    kb_ref[...] = k.astype(BF16)
    vb_ref[...] = v.astype(BF16)
    pf = _dot(h, wf_ref[...]) + bf_ref[...]
    lft = (jnp.minimum(pf, 0.0) - jnp.log1p(jnp.exp(-jnp.abs(pf)))).T[:nh]

    @pl.when(i < npb)
    def _():
        kp_ref[0] = k.T
        vp_ref[0] = v.T
        lp_ref[0] = lft

    @pl.when(i >= npb)
    def _():
        for hd in range(nh):
            rows = pl.ds(hd, tm, stride=nh)
            ks_ref[rows, :] = k[:, hd * HEAD_DIM:(hd + 1) * HEAD_DIM]
            vs_ref[rows, :] = v[:, hd * HEAD_DIM:(hd + 1) * HEAD_DIM]
        lq = ls_ref.shape[2]
        for j in range(tm // lq):
            ls_ref[j] = lft[:, j * lq:(j + 1) * lq]


def _kvf(hn, bp, s, lq, w_kv, w_f, b_f, k_norm):
    n, d = hn.shape
    nh = d // HEAD_DIM
    tm = min(TOK_TM, s)
    n_prompt = bp * s
    n_s = n - n_prompt
    assert s % tm == 0 and n_s % tm == 0 and tm % lq == 0
    npb, spb = n_prompt // tm, s // tm
    e, et = _head_indicator(d)
    row = pl.BlockSpec((tm, d), lambda i: (i, 0))
    const = lambda *sh: pl.BlockSpec(sh, lambda i: (0, 0))
    prompt_t = lambda rows: pl.BlockSpec(
        (1, rows, tm), lambda i: (jnp.minimum(i, npb - 1) // spb, 0, jnp.minimum(i, npb - 1) % spb))
    second = lambda i: (jnp.maximum(i - npb, 0), 0)
    kern = functools.partial(_kvf_kernel, d=d, nh=nh, npb=npb)
    return pl.pallas_call(
        kern, grid=(n // tm,),
        in_specs=[row, const(d, 2 * d), const(d, LANES), const(1, LANES), const(1, d), const(d, LANES), const(LANES, d)],
        out_specs=[prompt_t(d), pl.BlockSpec((tm * nh, HEAD_DIM), second),
                   prompt_t(d), pl.BlockSpec((tm * nh, HEAD_DIM), second),
                   prompt_t(nh), pl.BlockSpec((tm // lq, nh, lq), lambda i: (jnp.maximum(i - npb, 0), 0, 0)), row, row],
        out_shape=[jax.ShapeDtypeStruct((bp, d, s), F32), jax.ShapeDtypeStruct((n_s * nh, HEAD_DIM), F32),
                   jax.ShapeDtypeStruct((bp, d, s), F32), jax.ShapeDtypeStruct((n_s * nh, HEAD_DIM), F32),
                   jax.ShapeDtypeStruct((bp, nh, s), F32), jax.ShapeDtypeStruct((n_s // lq, nh, lq), F32),
                   jax.ShapeDtypeStruct((n, d), BF16), jax.ShapeDtypeStruct((n, d), BF16)],
        compiler_params=_cparams(("arbitrary",)), name="kvf")(
            hn, w_kv, jnp.pad(w_f, ((0, 0), (0, LANES - nh))), jnp.pad(b_f, (0, LANES - nh)).reshape(1, LANES),
            jnp.tile(k_norm, nh).reshape(1, d), e, et)


def _qproj_kernel(u_ref, wq_ref, qn_ref, e_ref, et_ref, q_ref):
    q = _dot(u_ref[...], wq_ref[...])
    q_ref[...] = (q * _head_norm(q, e_ref, et_ref) * qn_ref[...]).astype(BF16)


def _qproj(un, wq, qn_scaled):
    n, d = un.shape
    tm = min(TOK_TM, n)
    e, et = _head_indicator(d)
    row = pl.BlockSpec((tm, d), lambda i: (i, 0))
    const = lambda *s: pl.BlockSpec(s, lambda i: (0, 0))
    return pl.pallas_call(
        _qproj_kernel, grid=(n // tm,),
        in_specs=[row, const(d, d), const(1, d), const(d, LANES), const(LANES, d)], out_specs=row,
        out_shape=jax.ShapeDtypeStruct((n, d), BF16),
        compiler_params=_cparams(("parallel",)), name="qproj")(un, wq, qn_scaled, e, et)


def _oproj_kernel(x_ref, ap_ref, as_ref, wo_ref, o_ref, *, npb):
    a = jnp.where(pl.program_id(0) < npb, ap_ref[...], as_ref[...])
    o_ref[...] = x_ref[...] + _dot(a, wo_ref[...])


def _oproj(x, a_p, a_s, wo):
    n, d = x.shape
    tm = min(TOK_TM, n)
    assert a_p.shape[0] % tm == 0 and a_s.shape[0] % tm == 0
    npb = a_p.shape[0] // tm
    first, second = _part_maps(npb)
    row = pl.BlockSpec((tm, d), lambda i: (i, 0))
    return pl.pallas_call(
        functools.partial(_oproj_kernel, npb=npb), grid=(n // tm,),
        in_specs=[row, pl.BlockSpec((tm, d), first), pl.BlockSpec((tm, d), second),
                  pl.BlockSpec((d, d), lambda i: (0, 0))],
        out_specs=row, out_shape=jax.ShapeDtypeStruct((n, d), F32),
        compiler_params=_cparams(("parallel",)), name="oproj")(x, a_p, a_s, wo)


def _cumsum_kernel(x_ref, init_ref, o_ref, *, nblk, bw):
    tri = (lax.broadcasted_iota(jnp.int32, (bw, bw), 0) <= lax.broadcasted_iota(jnp.int32, (bw, bw), 1)).astype(BF16)
    carry = init_ref[...]
    for j in range(nblk):
        x = x_ref[:, j * bw:(j + 1) * bw]
        hi = x.astype(BF16)
        r1 = x - hi.astype(F32)
        mid = r1.astype(BF16)
        lo = (r1 - mid.astype(F32)).astype(BF16)
        cs = _dot(hi, tri) + _dot(mid, tri) + _dot(lo, tri) + carry
        o_ref[:, j * bw:(j + 1) * bw] = cs
        carry = cs[:, bw - 1:bw]


def _cumsum_rows(x, init):
    r, w = x.shape
    bw = min(256, w)
    kern = functools.partial(_cumsum_kernel, nblk=w // bw, bw=bw)
    return pl.pallas_call(
        kern, grid=(1,),
        in_specs=[pl.BlockSpec((r, w), lambda i: (0, 0)), pl.BlockSpec((r, 1), lambda i: (0, 0))],
        out_specs=pl.BlockSpec((r, w), lambda i: (0, 0)), out_shape=jax.ShapeDtypeStruct((r, w), F32),
        compiler_params=_cparams(("arbitrary",)), name="cumsum")(x, init)


def _lanes(x, w):
    if w <= LANES:
        return x[:, :w]
    return jnp.concatenate([x] * (w // LANES), axis=1)


def _head_masks(shape):
    head = lax.broadcasted_iota(jnp.int32, shape, 1) // HEAD_DIM
    return [head == i for i in range(shape[1] // HEAD_DIM)]


def _stack_heads(qs):
    zero = jnp.zeros_like(qs)
    return jnp.concatenate([jnp.where(mk, qs, zero) for mk in _head_masks(qs.shape)], axis=0)


def _flash_init(cq, hs, m_scr, l_scr, acc_scr, cq_scr):
    tq = cq.shape[0]
    m_scr[...] = jnp.full_like(m_scr, NEG_BIG)
    l_scr[...] = jnp.zeros_like(l_scr)
    acc_scr[...] = jnp.zeros_like(acc_scr)
    for g in range(cq_scr.shape[0]):
        for i in range(hs):
            h = g * hs + i
            cq_scr[g, i * tq:(i + 1) * tq, :] = jnp.broadcast_to(cq[:, h:h + 1], (tq, LANES))


def _flash_slab(g, q_st, ks, vs, ck_rows, causal, m_scr, l_scr, acc_scr, cq_scr, kv_t=False):
    hs = len(ck_rows)
    tq, tk = q_st.shape[0] // hs, ck_rows[0].shape[1]
    nt = (((1,), (1,)), ((), ()))
    s = _dot(q_st, ks) if kv_t else lax.dot_general(q_st, ks, nt, preferred_element_type=F32)
    t = jnp.concatenate([s[i * tq:(i + 1) * tq] - ck_rows[i] for i in range(hs)], axis=0)
    if causal:
        keep = lax.broadcasted_iota(jnp.int32, (tq, tk), 1) <= lax.broadcasted_iota(jnp.int32, (tq, tk), 0)
        t = jnp.where(jnp.concatenate([keep] * hs, axis=0), t, NEG_BIG)
    cq = cq_scr[g]
    m_prev = m_scr[g]
    m_new = jnp.maximum(m_prev, jnp.max(t, axis=1, keepdims=True) + cq)
    alpha = jnp.exp(m_prev - m_new)
    pe = jnp.exp(t - _lanes(m_new - cq, tk))
    l_scr[g] = alpha * l_scr[g] + jnp.sum(pe, axis=1, keepdims=True)
    pv = lax.dot_general(pe.astype(BF16), vs, nt, preferred_element_type=F32) if kv_t else _dot(pe.astype(BF16), vs)
    acc_scr[g] = _lanes(alpha, hs * HEAD_DIM) * acc_scr[g] + pv
    m_scr[g] = m_new


def _flash_finish(o_ref, hs, l_scr, acc_scr):
    tq = o_ref.shape[0]
    w = hs * HEAD_DIM
    masks = _head_masks((tq, w))
    for g in range(acc_scr.shape[0]):
        o = acc_scr[g] / _lanes(l_scr[g], w)
        out = o[(hs - 1) * tq:]
        for i in range(hs - 2, -1, -1):
            out = jnp.where(masks[i], o[i * tq:(i + 1) * tq], out)
        o_ref[:, g * w:(g + 1) * w] = out.astype(o_ref.dtype)


def _flash_scratch(ng, rows, w):
    return [pltpu.VMEM((ng, rows, LANES), F32), pltpu.VMEM((ng, rows, LANES), F32),
            pltpu.VMEM((ng, rows, w), F32), pltpu.VMEM((ng, rows, LANES), F32)]


def _attn_prompt_kernel(qi_tab, ki_tab, q_ref, k_ref, v_ref, cq_ref, ck_ref, o_ref, *scr, hs):
    step = pl.program_id(1)
    qi, ki = qi_tab[step], ki_tab[step]
    w = hs * HEAD_DIM
    ng = q_ref.shape[1] // w

    @pl.when(ki == 0)
    def _():
        _flash_init(cq_ref[...], hs, *scr)

    def sweep(causal):
        ck = ck_ref[0]
        for g in range(ng):
            sl = slice(g * w, (g + 1) * w)
            _flash_slab(g, _stack_heads(q_ref[:, sl]), k_ref[:, sl], v_ref[:, sl],
                        [ck[g * hs + i:g * hs + i + 1, :] for i in range(hs)], causal, *scr)

    @pl.when(ki < qi)
    def _():
        sweep(False)

    @pl.when(ki == qi)
    def _():
        sweep(True)
        _flash_finish(o_ref, hs, scr[1], scr[2])


def _attn_prompt(q, kb, vb, c_rows, c_t, nb, s):
    d = q.shape[1]
    nh = d // HEAD_DIM
    hs = ATT_HS
    tq = tk = min(ATT_TQ, s)
    nq = s // tq
    pairs = [(i, j) for i in range(nq) for j in range(i + 1)]
    qi_tab = jnp.asarray([p[0] for p in pairs], jnp.int32)
    ki_tab = jnp.asarray([p[1] for p in pairs], jnp.int32)
    grid_spec = pltpu.PrefetchScalarGridSpec(
        num_scalar_prefetch=2, grid=(nb, len(pairs)),
        in_specs=[pl.BlockSpec((tq, d), lambda b, p, qt, kt: (b * nq + qt[p], 0)),
                  pl.BlockSpec((tk, d), lambda b, p, qt, kt: (b * nq + kt[p], 0)),
                  pl.BlockSpec((tk, d), lambda b, p, qt, kt: (b * nq + kt[p], 0)),
                  pl.BlockSpec((tq, nh), lambda b, p, qt, kt: (b * nq + qt[p], 0)),
                  pl.BlockSpec((1, nh, tk), lambda b, p, qt, kt: (b, 0, kt[p]))],
        out_specs=pl.BlockSpec((tq, d), lambda b, p, qt, kt: (b * nq + qt[p], 0)),
        scratch_shapes=_flash_scratch(nh // hs, hs * tq, hs * HEAD_DIM))
    return pl.pallas_call(
        functools.partial(_attn_prompt_kernel, hs=hs), grid_spec=grid_spec,
        out_shape=jax.ShapeDtypeStruct((nb * s, d), BF16),
        compiler_params=_cparams(("parallel", "arbitrary")), name="attn_prompt")(
            qi_tab, ki_tab, q, kb, vb, c_rows, c_t)


def _attn_sample_kernel(q_ref, ck_ref, cv_ref, kn_ref, vn_ref, cq_ref, ckc_ref, ckn_ref, o_ref, *scr, hs, nkb):
    j = pl.program_id(1)
    w = hs * HEAD_DIM
    ng = q_ref.shape[1] // w

    @pl.when(j == 0)
    def _():
        _flash_init(cq_ref[0], hs, *scr)

    def sweep(kslab, vslab, ck, causal, kv_t):
        for g in range(ng):
            _flash_slab(g, _stack_heads(q_ref[:, g * w:(g + 1) * w]), kslab(g), vslab(g),
                        [ck[g * hs + i:g * hs + i + 1, :] for i in range(hs)], causal, *scr, kv_t=kv_t)

    @pl.when(j < nkb)
    def _():
        sweep(lambda g: ck_ref[0, g * w:(g + 1) * w, :].astype(BF16),
              lambda g: cv_ref[0, g * w:(g + 1) * w, :].astype(BF16), ckc_ref[0], False, True)

    @pl.when(j == nkb)
    def _():
        sweep(lambda g: kn_ref[:, g * w:(g + 1) * w], lambda g: vn_ref[:, g * w:(g + 1) * w], ckn_ref[0], True, False)
        _flash_finish(o_ref, hs, scr[1], scr[2])


def _attn_sample(q, kb, vb, cache_kt, cache_vt, c_new, c_past_t, c_new_t, n_prompt):
    d = q.shape[1]
    nh = d // HEAD_DIM
    nb, _, plen = cache_kt.shape
    lq = c_new.shape[1]
    hs = DEC_HS
    tk = min(DEC_TK, plen)
    nkb = plen // tk
    r0 = n_prompt // lq
    new_rows = pl.BlockSpec((lq, d), lambda b, j: (r0 + b, 0))
    cache = pl.BlockSpec((1, d, tk), lambda b, j: (b, 0, jnp.minimum(j, nkb - 1)))
    kern = functools.partial(_attn_sample_kernel, hs=hs, nkb=nkb)
    return pl.pallas_call(
        kern, grid=(nb, nkb + 1),
        in_specs=[new_rows, cache, cache, new_rows, new_rows,
                  pl.BlockSpec((1, lq, nh), lambda b, j: (b, 0, 0)),
                  pl.BlockSpec((1, nh, tk), lambda b, j: (b, 0, jnp.minimum(j, nkb - 1))),
                  pl.BlockSpec((1, nh, lq), lambda b, j: (b, 0, 0))],
        out_specs=pl.BlockSpec((lq, d), lambda b, j: (b, 0)),
        out_shape=jax.ShapeDtypeStruct((nb * lq, d), BF16),
        scratch_shapes=_flash_scratch(nh // hs, hs * lq, hs * HEAD_DIM),
        compiler_params=_cparams(("parallel", "arbitrary")), name="attn_sample")(
            q, cache_kt, cache_vt, kb, vb, c_new, c_past_t, c_new_t)


def kernel(x_prompt, x_sample, cache_k, cache_v, cache_logf, state_ssm_re, state_ssm_im, ffn_norm, w_ffn_gate, w_ffn_up, w_ffn_down, mix_norm, ssm_a_re, ssm_a_im, ssm_log_dt, ssm_b_re, ssm_b_im, ssm_c_re, ssm_c_im, ssm_d, w_glu_a, w_glu_b, kv_norm, w_kvf, b_f, k_norm, w_q, q_norm, w_o):
    bp, s, d = x_prompt.shape
    bs, lq, _ = x_sample.shape
    plen = cache_k.shape[1]
    nh = d // HEAD_DIM
    assert ffn_norm.shape[0] == 2 and ssm_a_re.shape[0] == 1 and w_q.shape[0] == 1
    n_prompt, n_sample = bp * s, bs * lq
    bf = lambda w: w.astype(BF16)

    def ffn(xs, l, j, **kw):
        return _ffn(xs, ffn_norm[l, j], bf(w_ffn_gate[l, j]), bf(w_ffn_up[l, j]), bf(w_ffn_down[l, j]), **kw)

    x, u = ffn([x_prompt.reshape(n_prompt, d), x_sample.reshape(n_sample, d)], 0, 0, g2=mix_norm[0])
    mats = _s5_params(ssm_log_dt[0], ssm_a_re[0], ssm_a_im[0], ssm_b_re[0], ssm_b_im[0], ssm_c_re[0], ssm_c_im[0])
    z, (p_re, p_im, s_re, s_im) = _s5_layer(u, n_prompt, bp, bs, state_ssm_re[:, 0], state_ssm_im[:, 0], mats, ssm_d[0])
    x = _glu(x, z, bf(w_glu_a[0]), bf(w_glu_b[0]))
    x, hn = ffn([x], 0, 1, g2=kv_norm)

    kt_p, k_s, vt_p, v_s, lft_p, lft_s, kb, vb = _kvf(hn, bp, s, lq, bf(w_kvf[:, :2 * d]), bf(w_kvf[:, 2 * d:]), b_f, k_norm)

    def cumsum_t(lft, init):
        b, _, l = lft.shape
        init = jnp.zeros((b * nh, 1), F32) if init is None else init.reshape(b * nh, 1)
        return _cumsum_rows(lft.reshape(b * nh, l), init).reshape(b, nh, l)

    cp_t = cumsum_t(lft_p, None)
    cpast_t = cumsum_t(jnp.swapaxes(cache_logf.astype(F32), 1, 2), None)
    cnew_t = cumsum_t(lft_s, cpast_t[:, :, -1])

    x, un = ffn([x], 1, 0, g2=mix_norm[1])
    q = _qproj(un, bf(w_q[0]), (jnp.tile(q_norm[0], nh) * (1.0 / math.sqrt(HEAD_DIM))).reshape(1, d))
    a_p = _attn_prompt(q, kb, vb, jnp.swapaxes(cp_t, 1, 2).reshape(n_prompt, nh), cp_t, bp, s)
    pos_minor = lambda c: jnp.transpose(c, (0, 2, 3, 1)).reshape(bs, d, plen)
    a_s = _attn_sample(q, kb, vb, pos_minor(cache_k), pos_minor(cache_v),
                       jnp.swapaxes(cnew_t, 1, 2), cpast_t, cnew_t, n_prompt)
    x = _oproj(x, a_p, a_s, bf(w_o[0]))
    (y_p, y_s), _ = ffn([x], 1, 1, split_out=(n_prompt, n_sample))

    cache_p = lambda a: jnp.transpose(a.reshape(bp, nh, HEAD_DIM, s), (0, 3, 1, 2))
    cache_s = lambda a: a.reshape(bs, lq, nh, HEAD_DIM)
    return (y_p.reshape(bp, s, d), y_s.reshape(bs, lq, d), p_re, p_im, cache_p(kt_p), cache_p(vt_p),
            jnp.swapaxes(lft_p, 1, 2), s_re, s_im, cache_s(k_s), cache_s(v_s), jnp.swapaxes(lft_s, 1, 2))
```

```python
import functools
import math

import jax
import jax.numpy as jnp
import numpy as np
from jax import lax
from jax.experimental import pallas as pl
from jax.experimental.pallas import tpu as pltpu

F32 = jnp.float32
BF16 = jnp.bfloat16

EPS = 1e-6
HEAD_DIM = 64
SSM_GROUP = 16
S5_CHUNK = 16
LANES = 128
NEG_BIG = -1e30
VMEM_LIMIT = 52 * 1024 * 1024

FFN_TM = 1024
FFN_TF = 512
TOK_TM = 512
ATT_TQ = 256
ATT_TK = 256
ATT_HS = 2
DEC_TK = 512
DEC_HS = 4


def _cparams(sem):
    return pltpu.CompilerParams(dimension_semantics=sem, vmem_limit_bytes=VMEM_LIMIT)


def _rms_scale(x):
    return lax.rsqrt(jnp.mean(x * x, axis=-1, keepdims=True) + EPS)


def _dot(a, b):
    return jnp.dot(a, b, preferred_element_type=F32)


def _split2(x):
    hi = x.astype(BF16)
    lo = (x - hi.astype(F32)).astype(BF16)
    return hi, lo


def _part_maps(npb):
    first = lambda i, *_: (jnp.minimum(i, npb - 1), 0)
    second = lambda i, *_: (jnp.maximum(i - npb, 0), 0)
    return first, second


def _ffn_kernel(*refs, nf, npb, n_in, n_out, with_norm):
    x_refs, refs = refs[:n_in], refs[n_in:]
    g_ref, wg_ref, wu_ref, wd_ref = refs[:4]
    refs = refs[4:]
    g2_ref = refs[0] if with_norm else None
    refs = refs[with_norm:]
    o_refs, refs = refs[:n_out], refs[n_out:]
    o2_ref = refs[0] if with_norm else None
    h_scr, acc_scr = refs[with_norm:]
    i, f = pl.program_id(0), pl.program_id(1)

    def load_x():
        if n_in == 1:
            return x_refs[0][...]
        return jnp.where(i < npb, x_refs[0][...], x_refs[1][...])

    @pl.when(f == 0)
    def _():
        x = load_x()
        h_scr[...] = (x * _rms_scale(x) * g_ref[...]).astype(BF16)
        acc_scr[...] = jnp.zeros_like(acc_scr)

    h = h_scr[...]
    a = _dot(h, wg_ref[...])
    b = _dot(h, wu_ref[...])
    t = (a * jax.nn.sigmoid(a)) * b
    acc_scr[...] += _dot(t.astype(BF16), wd_ref[...])

    @pl.when(f == nf - 1)
    def _():
        y = load_x() + 0.5 * acc_scr[...]
        if n_out == 1:
            o_refs[0][...] = y
        else:
            @pl.when(i < npb)
            def _():
                o_refs[0][...] = y

            @pl.when(i >= npb)
            def _():
                o_refs[1][...] = y
        if with_norm:
            o2_ref[...] = (y * _rms_scale(y) * g2_ref[...]).astype(o2_ref.dtype)


def _ffn(xs, g, wg, wu, wd, g2=None, split_out=None, norm_dtype=BF16):
    d = xs[0].shape[1]
    n = sum(x.shape[0] for x in xs)
    dff = wg.shape[1]
    tm, tf = min(FFN_TM, n), min(FFN_TF, dff)
    nf = dff // tf
    n_first = xs[0].shape[0] if len(xs) == 2 else (split_out[0] if split_out else n)
    assert all(x.shape[0] % tm == 0 for x in xs) and n_first % tm == 0
    npb = n_first // tm
    first, second = _part_maps(npb)
    with_norm = g2 is not None
    row = pl.BlockSpec((tm, d), lambda i, f: (i, 0))
    vec = pl.BlockSpec((1, d), lambda i, f: (0, 0))
    parts = [pl.BlockSpec((tm, d), first), pl.BlockSpec((tm, d), second)]
    in_specs = (parts if len(xs) == 2 else [row]) + [
        vec, pl.BlockSpec((d, tf), lambda i, f: (0, f)), pl.BlockSpec((d, tf), lambda i, f: (0, f)),
        pl.BlockSpec((tf, d), lambda i, f: (f, 0))]
    args = list(xs) + [g.reshape(1, d), wg, wu, wd]
    if split_out:
        out_shape = [jax.ShapeDtypeStruct((m, d), F32) for m in split_out]
        out_specs = list(parts)
    else:
        out_shape = [jax.ShapeDtypeStruct((n, d), F32)]
        out_specs = [row]
    n_out = len(out_shape)
    if with_norm:
        in_specs.append(vec)
        args.append(g2.reshape(1, d))
        out_shape.append(jax.ShapeDtypeStruct((n, d), norm_dtype))
        out_specs.append(row)
    kern = functools.partial(_ffn_kernel, nf=nf, npb=npb, n_in=len(xs), n_out=n_out, with_norm=with_norm)
    res = pl.pallas_call(
        kern, grid=(n // tm, nf), in_specs=in_specs, out_specs=out_specs, out_shape=out_shape,
        scratch_shapes=[pltpu.VMEM((tm, d), BF16), pltpu.VMEM((tm, d), F32)],
        compiler_params=_cparams(("arbitrary", "arbitrary")), name="ffn")(*args)
    y = tuple(res[:n_out]) if split_out else res[0]
    return y, (res[n_out] if with_norm else None)


def _s5_param_kernel(ldt_ref, arc_ref, aic_ref, arr_ref, air_ref, btr_ref, bti_ref, ctr_ref, cti_ref,
                     m_ref, sbr_ref, sbi_ref, car_ref, cai_ref, a16r_ref, a16i_ref):
    t, c = S5_CHUNK, SSM_GROUP
    dt = jnp.exp(ldt_ref[0])

    def cpow(ar, ai, n):
        mag = jnp.exp(ar * dt * n)
        ang = ai * dt * n
        return mag * jnp.cos(ang), mag * jnp.sin(ang)

    arc, aic = arc_ref[0], aic_ref[0]
    arr, air = arr_ref[0], air_ref[0]
    p = arc.shape[0]
    jidx = (lax.broadcasted_iota(jnp.int32, (p, t * c), 1) // c).astype(F32)
    cr, ci = ctr_ref[0], cti_ref[0]
    pr, pi = cpow(arc, aic, jidx + 1.0)
    car_ref[0] = (cr * pr - ci * pi).astype(BF16)
    cai_ref[0] = (-(cr * pi + ci * pr)).astype(BF16)
    pr, pi = cpow(arc, aic, jidx)
    csr = cr * pr - ci * pi
    csi = cr * pi + ci * pr

    abr, abi = cpow(arr, air, 1.0)
    xr, xi = abr - 1.0, abi
    den = arr * arr + air * air
    qr = (xr * arr + xi * air) / den
    qi = (xi * arr - xr * air) / den
    btr, bti = btr_ref[0], bti_ref[0]
    bbr = qr * btr - qi * bti
    bbi = qr * bti + qi * btr
    iidx = (lax.broadcasted_iota(jnp.int32, (t * c, p), 0) // c).astype(F32)
    pr, pi = cpow(arr, air, (t - 1.0) - iidx)
    sbr_ref[0] = (bbr * pr - bbi * pi).astype(BF16)
    sbi_ref[0] = (bbr * pi + bbi * pr).astype(BF16)
    pr, pi = cpow(arr, air, float(t))
    a16r_ref[0] = pr
    a16i_ref[0] = pi

    hp = lax.Precision.HIGHEST
    kt = (jnp.dot(bbr[:c], csr, precision=hp, preferred_element_type=F32)
          - jnp.dot(bbi[:c], csi, precision=hp, preferred_element_type=F32))
    lane = lax.broadcasted_iota(jnp.int32, (c, t * c), 1)
    for i in range(t):
        blk = kt if i == 0 else pltpu.roll(kt, i * c, 1)
        m_ref[0, i * c:(i + 1) * c, :] = jnp.where(lane >= i * c, blk, 0.0).astype(BF16)


def _s5_params(log_dt, a_re, a_im, b_re, b_im, c_re, c_im):
    g, p = a_re.shape
    t, c = S5_CHUNK, SSM_GROUP
    tc = t * c

    def tile_b(b):
        return jnp.broadcast_to(jnp.swapaxes(b, 1, 2)[:, None], (g, t, c, p)).reshape(g, tc, p)

    def tile_c(cm):
        return jnp.broadcast_to(jnp.swapaxes(cm, 1, 2)[:, :, None], (g, p, t, c)).reshape(g, p, tc)

    def spec(*shape):
        return pl.BlockSpec((1,) + shape, lambda i: (i, 0, 0))

    return pl.pallas_call(
        _s5_param_kernel, grid=(g,),
        in_specs=[spec(1, 1), spec(p, 1), spec(p, 1), spec(1, p), spec(1, p),
                  spec(tc, p), spec(tc, p), spec(p, tc), spec(p, tc)],
        out_specs=[spec(tc, tc), spec(tc, p), spec(tc, p), spec(p, tc), spec(p, tc), spec(1, p), spec(1, p)],
        out_shape=[jax.ShapeDtypeStruct((g, tc, tc), BF16),
                   jax.ShapeDtypeStruct((g, tc, p), BF16), jax.ShapeDtypeStruct((g, tc, p), BF16),
                   jax.ShapeDtypeStruct((g, p, tc), BF16), jax.ShapeDtypeStruct((g, p, tc), BF16),
                   jax.ShapeDtypeStruct((g, 1, p), F32), jax.ShapeDtypeStruct((g, 1, p), F32)],
        compiler_params=_cparams(("parallel",)), name="s5_params")(
            log_dt.reshape(g, 1, 1), a_re.reshape(g, p, 1), a_im.reshape(g, p, 1),
            a_re.reshape(g, 1, p), a_im.reshape(g, 1, p),
            tile_b(b_re), tile_b(b_im), tile_c(c_re), tile_c(c_im))


def _gelu_tanh(y):
    return 0.5 * y * (1.0 + jnp.tanh(math.sqrt(2.0 / math.pi) * (y + 0.044715 * (y * y * y))))


def _seg_transpose(v):
    seg = lax.broadcasted_iota(jnp.int32, v[0].shape, 1) // SSM_GROUP
    for dist in (4, 2, 1):
        keep = (seg & dist) == 0
        nxt = list(v)
        for i in range(8):
            if i & dist == 0:
                lo, hi = v[i], v[i + dist]
                nxt[i] = jnp.where(keep, lo, pltpu.roll(hi, dist * SSM_GROUP, 1))
                nxt[i + dist] = jnp.where(keep, pltpu.roll(lo, LANES - dist * SSM_GROUP, 1), hi)
        v = nxt
    return v


def _s5_main_kernel(u_ref, m_ref, sbr_ref, sbi_ref, car_ref, cai_ref, a16r_ref, a16i_ref,
                    h0r_ref, h0i_ref, d_ref, z_ref, hpr_ref, hpi_ref, hsr_ref, hsi_ref,
                    u2_scr, sr_scr, si_scr, hr_scr, hi_scr, *, bp, nkp, bs, nks):
    t = S5_CHUNK
    npq = bp * nkp // 8
    nq = npq + bs * nks // 8
    sp = nkp * t

    def token0(q):
        return jnp.where(q < npq, (q % bp) * sp + (q // bp) * (8 * t), bp * sp + (q - npq) * (8 * t))

    def gather(q, carry):
        tok, rows = token0(q), pl.ds(pl.multiple_of(q * 8, 8), 8)
        for h in range(2):
            v = [u_ref[pl.ds(tok + 8 * h + i, 8, stride=t), :] for i in range(8)]
            for gi, w in enumerate(_seg_transpose(v)):
                u2_scr[gi, rows, h * LANES:(h + 1) * LANES] = w
        return carry

    lax.fori_loop(0, nq, gather, 0)

    def group(gi, carry):
        u = u2_scr[gi]
        ub = u.astype(BF16)
        sr_scr[...] = _dot(ub, sbr_ref[gi])
        si_scr[...] = _dot(ub, sbi_ref[gi])
        ar, ai = a16r_ref[gi], a16i_ref[gi]

        def step(rows, hr, hi):
            hr_scr[rows, :] = hr
            hi_scr[rows, :] = hi
            return (ar * hr - ai * hi + sr_scr[rows, :], ar * hi + ai * hr + si_scr[rows, :])

        hr = hi = jnp.zeros((bp, ar.shape[1]), F32)
        for k in range(nkp):
            hr, hi = step(pl.ds((k // 8) * bp * 8 + k % 8, bp, stride=8), hr, hi)
        hpr_ref[gi] = hr
        hpi_ref[gi] = hi
        hr, hi = h0r_ref[gi], h0i_ref[gi]
        for k in range(nks):
            hr, hi = step(pl.ds(bp * nkp + k, bs, stride=nks), hr, hi)
        hsr_ref[gi] = hr
        hsi_ref[gi] = hi

        y = (_dot(ub, m_ref[gi]) + _dot(hr_scr[...].astype(BF16), car_ref[gi])
             + _dot(hi_scr[...].astype(BF16), cai_ref[gi]) + d_ref[gi] * u)
        u2_scr[gi] = _gelu_tanh(y)
        return carry

    lax.fori_loop(0, 8, group, 0)

    def scatter(q, carry):
        tok, rows = token0(q), pl.ds(pl.multiple_of(q * 8, 8), 8)
        for h in range(2):
            v = [u2_scr[gi, rows, h * LANES:(h + 1) * LANES] for gi in range(8)]
            for i, w in enumerate(_seg_transpose(v)):
                z_ref[pl.ds(tok + 8 * h + i, 8, stride=t), :] = w
        return carry

    lax.fori_loop(0, nq, scatter, 0)


def _s5_layer(u, n_prompt, bp, bs, h0_re, h0_im, mats, d_skip):
    n, d = u.shape
    t, c = S5_CHUNK, SSM_GROUP
    g, tc = d // c, t * c
    gs = LANES // c
    p = h0_re.shape[-1]
    nkp = n_prompt // (bp * t)
    nks = (n - n_prompt) // (bs * t)
    r = nkp * bp + nks * bs
    assert gs == 8 and t == 16 and bp % 8 == 0 and nkp % 8 == 0 and (bs * nks) % 8 == 0

    m, sbr, sbi, car, cai, a16r, a16i = mats
    d_t = jnp.broadcast_to(d_skip.reshape(g, 1, 1, c), (g, 1, t, c)).reshape(g, 1, tc)
    h0r = jnp.swapaxes(h0_re, 0, 1)
    h0i = jnp.swapaxes(h0_im, 0, 1)

    def spec(*shape):
        return pl.BlockSpec((gs,) + shape, lambda i: (i, 0, 0))

    kern = functools.partial(_s5_main_kernel, bp=bp, nkp=nkp, bs=bs, nks=nks)
    z, hpr, hpi, hsr, hsi = pl.pallas_call(
        kern, grid=(g // gs,),
        in_specs=[pl.BlockSpec((n, LANES), lambda i: (0, i), pipeline_mode=pl.Buffered(1)),
                  spec(tc, tc), spec(tc, p), spec(tc, p), spec(p, tc), spec(p, tc),
                  spec(1, p), spec(1, p), spec(bs, p), spec(bs, p), spec(1, tc)],
        out_specs=[pl.BlockSpec((n, LANES), lambda i: (0, i)),
                   spec(bp, p), spec(bp, p), spec(bs, p), spec(bs, p)],
        out_shape=[jax.ShapeDtypeStruct((n, d), F32),
                   jax.ShapeDtypeStruct((g, bp, p), F32), jax.ShapeDtypeStruct((g, bp, p), F32),
                   jax.ShapeDtypeStruct((g, bs, p), F32), jax.ShapeDtypeStruct((g, bs, p), F32)],
        scratch_shapes=[pltpu.VMEM((gs, r, tc), F32)] + [pltpu.VMEM((r, p), F32) for _ in range(4)],
        compiler_params=_cparams(("arbitrary",)), name="s5_main")(
            u, m, sbr, sbi, car, cai, a16r, a16i, h0r, h0i, d_t)
    states = tuple(jnp.swapaxes(h, 0, 1)[:, None] for h in (hpr, hpi, hsr, hsi))
    return z, states


def _glu_kernel(x_ref, z_ref, wa_ref, wb_ref, o_ref):
    z = z_ref[...].astype(BF16)
    o_ref[...] = x_ref[...] + _dot(z, wa_ref[...]) * jax.nn.sigmoid(_dot(z, wb_ref[...]))


def _glu(x, z, wa, wb):
    n, d = x.shape
    tm = min(TOK_TM, n)
    row = pl.BlockSpec((tm, d), lambda i: (i, 0))
    mat = pl.BlockSpec((d, d), lambda i: (0, 0))
    return pl.pallas_call(
        _glu_kernel, grid=(n // tm,), in_specs=[row, row, mat, mat], out_specs=row,
        out_shape=jax.ShapeDtypeStruct((n, d), F32),
        compiler_params=_cparams(("parallel",)), name="glu")(x, z, wa, wb)


def _head_norm(x, e_ref, et_ref):
    hi, lo = _split2(x * x)
    ms = (_dot(hi, e_ref[...]) + _dot(lo, e_ref[...])) * (1.0 / HEAD_DIM)
    hi, lo = _split2(lax.rsqrt(ms + EPS))
    return _dot(hi, et_ref[...]) + _dot(lo, et_ref[...])


def _head_indicator(d):
    nh = d // HEAD_DIM
    e = (np.arange(d)[:, None] // HEAD_DIM == np.arange(LANES)[None, :]).astype(np.float32)
    assert nh <= LANES
    return jnp.asarray(e, BF16), jnp.asarray(e.T, BF16)


def _kvf_kernel(h_ref, wkv_ref, wf_ref, bf_ref, kn_ref, e_ref, et_ref,
                kp_ref, ks_ref, vp_ref, vs_ref, lp_ref, ls_ref, kb_ref, vb_ref, *, d, nh, npb):
    i = pl.program_id(0)
    h = h_ref[...]
    tm = h.shape[0]
    pkv = _dot(h, wkv_ref[...])
    kraw, v = pkv[:, :d], pkv[:, d:]
    k = kraw * _head_norm(kraw, e_ref, et_ref) * kn_ref[...]
    kb_ref[...] = k.astype(BF16)
    vb_ref[...] = v.astype(BF16)
    pf = _dot(h, wf_ref[...]) + bf_ref[...]
    lft = (jnp.minimum(pf, 0.0) - jnp.log1p(jnp.exp(-jnp.abs(pf)))).T[:nh]

    @pl.when(i < npb)
    def _():
        kp_ref[0] = k.T
        vp_ref[0] = v.T
        lp_ref[0] = lft

    @pl.when(i >= npb)
    def _():
        for hd in range(nh):
            rows = pl.ds(hd, tm, stride=nh)
            ks_ref[rows, :] = k[:, hd * HEAD_DIM:(hd + 1) * HEAD_DIM]
            vs_ref[rows, :] = v[:, hd * HEAD_DIM:(hd + 1) * HEAD_DIM]
        lq = ls_ref.shape[2]
        for j in range(tm // lq):
            ls_ref[j] = lft[:, j * lq:(j + 1) * lq]


def _kvf(hn, bp, s, lq, w_kv, w_f, b_f, k_norm):
    n, d = hn.shape
    nh = d // HEAD_DIM
    tm = min(TOK_TM, s)
    n_prompt = bp * s
    n_s = n - n_prompt
    assert s % tm == 0 and n_s % tm == 0 and tm % lq == 0
    npb, spb = n_prompt // tm, s // tm
    e, et = _head_indicator(d)
    row = pl.BlockSpec((tm, d), lambda i: (i, 0))
    const = lambda *sh: pl.BlockSpec(sh, lambda i: (0, 0))
    prompt_t = lambda rows: pl.BlockSpec(
        (1, rows, tm), lambda i: (jnp.minimum(i, npb - 1) // spb, 0, jnp.minimum(i, npb - 1) % spb))
    second = lambda i: (jnp.maximum(i - npb, 0), 0)
    kern = functools.partial(_kvf_kernel, d=d, nh=nh, npb=npb)
    return pl.pallas_call(
        kern, grid=(n // tm,),
        in_specs=[row, const(d, 2 * d), const(d, LANES), const(1, LANES), const(1, d), const(d, LANES), const(LANES, d)],
        out_specs=[prompt_t(d), pl.BlockSpec((tm * nh, HEAD_DIM), second),
                   prompt_t(d), pl.BlockSpec((tm * nh, HEAD_DIM), second),
                   prompt_t(nh), pl.BlockSpec((tm // lq, nh, lq), lambda i: (jnp.maximum(i - npb, 0), 0, 0)), row, row],
        out_shape=[jax.ShapeDtypeStruct((bp, d, s), F32), jax.ShapeDtypeStruct((n_s * nh, HEAD_DIM), F32),
                   jax.ShapeDtypeStruct((bp, d, s), F32), jax.ShapeDtypeStruct((n_s * nh, HEAD_DIM), F32),
                   jax.ShapeDtypeStruct((bp, nh, s), F32), jax.ShapeDtypeStruct((n_s // lq, nh, lq), F32),
                   jax.ShapeDtypeStruct((n, d), BF16), jax.ShapeDtypeStruct((n, d), BF16)],
        compiler_params=_cparams(("arbitrary",)), name="kvf")(
            hn, w_kv, jnp.pad(w_f, ((0, 0), (0, LANES - nh))), jnp.pad(b_f, (0, LANES - nh)).reshape(1, LANES),
            jnp.tile(k_norm, nh).reshape(1, d), e, et)


def _qproj_kernel(u_ref, wq_ref, qn_ref, e_ref, et_ref, q_ref):
    q = _dot(u_ref[...], wq_ref[...])
    q_ref[...] = (q * _head_norm(q, e_ref, et_ref) * qn_ref[...]).astype(BF16)


def _qproj(un, wq, qn_scaled):
    n, d = un.shape
    tm = min(TOK_TM, n)
    e, et = _head_indicator(d)
    row = pl.BlockSpec((tm, d), lambda i: (i, 0))
    const = lambda *s: pl.BlockSpec(s, lambda i: (0, 0))
    return pl.pallas_call(
        _qproj_kernel, grid=(n // tm,),
        in_specs=[row, const(d, d), const(1, d), const(d, LANES), const(LANES, d)], out_specs=row,
        out_shape=jax.ShapeDtypeStruct((n, d), BF16),
        compiler_params=_cparams(("parallel",)), name="qproj")(un, wq, qn_scaled, e, et)


def _oproj_kernel(x_ref, ap_ref, as_ref, wo_ref, o_ref, *, npb):
    a = jnp.where(pl.program_id(0) < npb, ap_ref[...], as_ref[...])
    o_ref[...] = x_ref[...] + _dot(a, wo_ref[...])


def _oproj(x, a_p, a_s, wo):
    n, d = x.shape
    tm = min(TOK_TM, n)
    assert a_p.shape[0] % tm == 0 and a_s.shape[0] % tm == 0
    npb = a_p.shape[0] // tm
    first, second = _part_maps(npb)
    row = pl.BlockSpec((tm, d), lambda i: (i, 0))
    return pl.pallas_call(
        functools.partial(_oproj_kernel, npb=npb), grid=(n // tm,),
        in_specs=[row, pl.BlockSpec((tm, d), first), pl.BlockSpec((tm, d), second),
                  pl.BlockSpec((d, d), lambda i: (0, 0))],
        out_specs=row, out_shape=jax.ShapeDtypeStruct((n, d), F32),
        compiler_params=_cparams(("parallel",)), name="oproj")(x, a_p, a_s, wo)


def _cumsum_kernel(x_ref, init_ref, o_ref, *, nblk, bw):
    tri = (lax.broadcasted_iota(jnp.int32, (bw, bw), 0) <= lax.broadcasted_iota(jnp.int32, (bw, bw), 1)).astype(BF16)
    carry = init_ref[...]
    for j in range(nblk):
        x = x_ref[:, j * bw:(j + 1) * bw]
        hi = x.astype(BF16)
        r1 = x - hi.astype(F32)
        mid = r1.astype(BF16)
        lo = (r1 - mid.astype(F32)).astype(BF16)
        cs = _dot(hi, tri) + _dot(mid, tri) + _dot(lo, tri) + carry
        o_ref[:, j * bw:(j + 1) * bw] = cs
        carry = cs[:, bw - 1:bw]


def _cumsum_rows(x, init):
    r, w = x.shape
    bw = min(256, w)
    kern = functools.partial(_cumsum_kernel, nblk=w // bw, bw=bw)
    return pl.pallas_call(
        kern, grid=(1,),
        in_specs=[pl.BlockSpec((r, w), lambda i: (0, 0)), pl.BlockSpec((r, 1), lambda i: (0, 0))],
        out_specs=pl.BlockSpec((r, w), lambda i: (0, 0)), out_shape=jax.ShapeDtypeStruct((r, w), F32),
        compiler_params=_cparams(("arbitrary",)), name="cumsum")(x, init)


def _lanes(x, w):
    if w <= LANES:
        return x[:, :w]
    return jnp.concatenate([x] * (w // LANES), axis=1)


def _head_masks(shape):
    head = lax.broadcasted_iota(jnp.int32, shape, 1) // HEAD_DIM
    return [head == i for i in range(shape[1] // HEAD_DIM)]


def _stack_heads(qs):
    zero = jnp.zeros_like(qs)
    return jnp.concatenate([jnp.where(mk, qs, zero) for mk in _head_masks(qs.shape)], axis=0)


def _flash_init(cq, hs, m_scr, l_scr, acc_scr, cq_scr):
    tq = cq.shape[0]
    m_scr[...] = jnp.full_like(m_scr, NEG_BIG)
    l_scr[...] = jnp.zeros_like(l_scr)
    acc_scr[...] = jnp.zeros_like(acc_scr)
    for g in range(cq_scr.shape[0]):
        for i in range(hs):
            h = g * hs + i
            cq_scr[g, i * tq:(i + 1) * tq, :] = jnp.broadcast_to(cq[:, h:h + 1], (tq, LANES))


def _flash_slab(g, q_st, ks, vs, ck_rows, causal, m_scr, l_scr, acc_scr, cq_scr, kv_t=False):
    hs = len(ck_rows)
    tq, tk = q_st.shape[0] // hs, ck_rows[0].shape[1]
    nt = (((1,), (1,)), ((), ()))
    s = _dot(q_st, ks) if kv_t else lax.dot_general(q_st, ks, nt, preferred_element_type=F32)
    t = jnp.concatenate([s[i * tq:(i + 1) * tq] - ck_rows[i] for i in range(hs)], axis=0)
    if causal:
        keep = lax.broadcasted_iota(jnp.int32, (tq, tk), 1) <= lax.broadcasted_iota(jnp.int32, (tq, tk), 0)
        t = jnp.where(jnp.concatenate([keep] * hs, axis=0), t, NEG_BIG)
    cq = cq_scr[g]
    m_prev = m_scr[g]
    m_new = jnp.maximum(m_prev, jnp.max(t, axis=1, keepdims=True) + cq)
    alpha = jnp.exp(m_prev - m_new)
    pe = jnp.exp(t - _lanes(m_new - cq, tk))
    l_scr[g] = alpha * l_scr[g] + jnp.sum(pe, axis=1, keepdims=True)
    pv = lax.dot_general(pe.astype(BF16), vs, nt, preferred_element_type=F32) if kv_t else _dot(pe.astype(BF16), vs)
    acc_scr[g] = _lanes(alpha, hs * HEAD_DIM) * acc_scr[g] + pv
    m_scr[g] = m_new


def _flash_finish(o_ref, hs, l_scr, acc_scr):
    tq = o_ref.shape[0]
    w = hs * HEAD_DIM
    masks = _head_masks((tq, w))
    for g in range(acc_scr.shape[0]):
        o = acc_scr[g] / _lanes(l_scr[g], w)
        out = o[(hs - 1) * tq:]
        for i in range(hs - 2, -1, -1):
            out = jnp.where(masks[i], o[i * tq:(i + 1) * tq], out)
        o_ref[:, g * w:(g + 1) * w] = out.astype(o_ref.dtype)


def _flash_scratch(ng, rows, w):
    return [pltpu.VMEM((ng, rows, LANES), F32), pltpu.VMEM((ng, rows, LANES), F32),
            pltpu.VMEM((ng, rows, w), F32), pltpu.VMEM((ng, rows, LANES), F32)]


def _attn_prompt_kernel(qi_tab, ki_tab, q_ref, k_ref, v_ref, cq_ref, ck_ref, o_ref, *scr, hs):
    step = pl.program_id(1)
    qi, ki = qi_tab[step], ki_tab[step]
    w = hs * HEAD_DIM
    ng = q_ref.shape[1] // w

    @pl.when(ki == 0)
    def _():
        _flash_init(cq_ref[...], hs, *scr)

    def sweep(causal):
        ck = ck_ref[0]
        for g in range(ng):
            sl = slice(g * w, (g + 1) * w)
            _flash_slab(g, _stack_heads(q_ref[:, sl]), k_ref[:, sl], v_ref[:, sl],
                        [ck[g * hs + i:g * hs + i + 1, :] for i in range(hs)], causal, *scr)

    @pl.when(ki < qi)
    def _():
        sweep(False)

    @pl.when(ki == qi)
    def _():
        sweep(True)
        _flash_finish(o_ref, hs, scr[1], scr[2])


def _attn_prompt(q, kb, vb, c_rows, c_t, nb, s):
    d = q.shape[1]
    nh = d // HEAD_DIM
    hs = ATT_HS
    tq = tk = min(ATT_TQ, s)
    nq = s // tq
    pairs = [(i, j) for i in range(nq) for j in range(i + 1)]
    qi_tab = jnp.asarray([p[0] for p in pairs], jnp.int32)
    ki_tab = jnp.asarray([p[1] for p in pairs], jnp.int32)
    grid_spec = pltpu.PrefetchScalarGridSpec(
        num_scalar_prefetch=2, grid=(nb, len(pairs)),
        in_specs=[pl.BlockSpec((tq, d), lambda b, p, qt, kt: (b * nq + qt[p], 0)),
                  pl.BlockSpec((tk, d), lambda b, p, qt, kt: (b * nq + kt[p], 0)),
                  pl.BlockSpec((tk, d), lambda b, p, qt, kt: (b * nq + kt[p], 0)),
                  pl.BlockSpec((tq, nh), lambda b, p, qt, kt: (b * nq + qt[p], 0)),
                  pl.BlockSpec((1, nh, tk), lambda b, p, qt, kt: (b, 0, kt[p]))],
        out_specs=pl.BlockSpec((tq, d), lambda b, p, qt, kt: (b * nq + qt[p], 0)),
        scratch_shapes=_flash_scratch(nh // hs, hs * tq, hs * HEAD_DIM))
    return pl.pallas_call(
        functools.partial(_attn_prompt_kernel, hs=hs), grid_spec=grid_spec,
        out_shape=jax.ShapeDtypeStruct((nb * s, d), BF16),
        compiler_params=_cparams(("parallel", "arbitrary")), name="attn_prompt")(
            qi_tab, ki_tab, q, kb, vb, c_rows, c_t)


def _attn_sample_kernel(q_ref, ck_ref, cv_ref, kn_ref, vn_ref, cq_ref, ckc_ref, ckn_ref, o_ref, *scr, hs, nkb):
    j = pl.program_id(1)
    w = hs * HEAD_DIM
    ng = q_ref.shape[1] // w

    @pl.when(j == 0)
    def _():
        _flash_init(cq_ref[0], hs, *scr)

    def sweep(kslab, vslab, ck, causal, kv_t):
        for g in range(ng):
            _flash_slab(g, _stack_heads(q_ref[:, g * w:(g + 1) * w]), kslab(g), vslab(g),
                        [ck[g * hs + i:g * hs + i + 1, :] for i in range(hs)], causal, *scr, kv_t=kv_t)

    @pl.when(j < nkb)
    def _():
        sweep(lambda g: ck_ref[0, g * w:(g + 1) * w, :].astype(BF16),
              lambda g: cv_ref[0, g * w:(g + 1) * w, :].astype(BF16), ckc_ref[0], False, True)

    @pl.when(j == nkb)
    def _():
        sweep(lambda g: kn_ref[:, g * w:(g + 1) * w], lambda g: vn_ref[:, g * w:(g + 1) * w], ckn_ref[0], True, False)
        _flash_finish(o_ref, hs, scr[1], scr[2])


def _attn_sample(q, kb, vb, cache_kt, cache_vt, c_new, c_past_t, c_new_t, n_prompt):
    d = q.shape[1]
    nh = d // HEAD_DIM
    nb, _, plen = cache_kt.shape
    lq = c_new.shape[1]
    hs = DEC_HS
    tk = min(DEC_TK, plen)
    nkb = plen // tk
    r0 = n_prompt // lq
    new_rows = pl.BlockSpec((lq, d), lambda b, j: (r0 + b, 0))
    cache = pl.BlockSpec((1, d, tk), lambda b, j: (b, 0, jnp.minimum(j, nkb - 1)))
    kern = functools.partial(_attn_sample_kernel, hs=hs, nkb=nkb)
    return pl.pallas_call(
        kern, grid=(nb, nkb + 1),
        in_specs=[new_rows, cache, cache, new_rows, new_rows,
                  pl.BlockSpec((1, lq, nh), lambda b, j: (b, 0, 0)),
                  pl.BlockSpec((1, nh, tk), lambda b, j: (b, 0, jnp.minimum(j, nkb - 1))),
                  pl.BlockSpec((1, nh, lq), lambda b, j: (b, 0, 0))],
        out_specs=pl.BlockSpec((lq, d), lambda b, j: (b, 0)),
        out_shape=jax.ShapeDtypeStruct((nb * lq, d), BF16),
        scratch_shapes=_flash_scratch(nh // hs, hs * lq, hs * HEAD_DIM),
        compiler_params=_cparams(("parallel", "arbitrary")), name="attn_sample")(
            q, cache_kt, cache_vt, kb, vb, c_new, c_past_t, c_new_t)


def kernel(x_prompt, x_sample, cache_k, cache_v, cache_logf, state_ssm_re, state_ssm_im, ffn_norm, w_ffn_gate, w_ffn_up, w_ffn_down, mix_norm, ssm_a_re, ssm_a_im, ssm_log_dt, ssm_b_re, ssm_b_im, ssm_c_re, ssm_c_im, ssm_d, w_glu_a, w_glu_b, kv_norm, w_kvf, b_f, k_norm, w_q, q_norm, w_o):
    bp, s, d = x_prompt.shape
    bs, lq, _ = x_sample.shape
    plen = cache_k.shape[1]
    nh = d // HEAD_DIM
    assert ffn_norm.shape[0] == 2 and ssm_a_re.shape[0] == 1 and w_q.shape[0] == 1
    n_prompt, n_sample = bp * s, bs * lq
    bf = lambda w: w.astype(BF16)

    def ffn(xs, l, j, **kw):
        return _ffn(xs, ffn_norm[l, j], bf(w_ffn_gate[l, j]), bf(w_ffn_up[l, j]), bf(w_ffn_down[l, j]), **kw)

    x, u = ffn([x_prompt.reshape(n_prompt, d), x_sample.reshape(n_sample, d)], 0, 0, g2=mix_norm[0], norm_dtype=F32)
    mats = _s5_params(ssm_log_dt[0], ssm_a_re[0], ssm_a_im[0], ssm_b_re[0], ssm_b_im[0], ssm_c_re[0], ssm_c_im[0])
    z, (p_re, p_im, s_re, s_im) = _s5_layer(u, n_prompt, bp, bs, state_ssm_re[:, 0], state_ssm_im[:, 0], mats, ssm_d[0])
    x = _glu(x, z, bf(w_glu_a[0]), bf(w_glu_b[0]))
    x, hn = ffn([x], 0, 1, g2=kv_norm)

    kt_p, k_s, vt_p, v_s, lft_p, lft_s, kb, vb = _kvf(hn, bp, s, lq, bf(w_kvf[:, :2 * d]), bf(w_kvf[:, 2 * d:]), b_f, k_norm)

    def cumsum_t(lft, init):
        b, _, l = lft.shape
        init = jnp.zeros((b * nh, 1), F32) if init is None else init.reshape(b * nh, 1)
        return _cumsum_rows(lft.reshape(b * nh, l), init).reshape(b, nh, l)

    cp_t = cumsum_t(lft_p, None)
    cpast_t = cumsum_t(jnp.swapaxes(cache_logf.astype(F32), 1, 2), None)
    cnew_t = cumsum_t(lft_s, cpast_t[:, :, -1])

    x, un = ffn([x], 1, 0, g2=mix_norm[1])
    q = _qproj(un, bf(w_q[0]), (jnp.tile(q_norm[0], nh) * (1.0 / math.sqrt(HEAD_DIM))).reshape(1, d))
    a_p = _attn_prompt(q, kb, vb, jnp.swapaxes(cp_t, 1, 2).reshape(n_prompt, nh), cp_t, bp, s)
    pos_minor = lambda c: jnp.transpose(c, (0, 2, 3, 1)).reshape(bs, d, plen)
    a_s = _attn_sample(q, kb, vb, pos_minor(cache_k), pos_minor(cache_v),
                       jnp.swapaxes(cnew_t, 1, 2), cpast_t, cnew_t, n_prompt)
    x = _oproj(x, a_p, a_s, bf(w_o[0]))
    (y_p, y_s), _ = ffn([x], 1, 1, split_out=(n_prompt, n_sample))

    cache_p = lambda a: jnp.transpose(a.reshape(bp, nh, HEAD_DIM, s), (0, 3, 1, 2))
    cache_s = lambda a: a.reshape(bs, lq, nh, HEAD_DIM)
    return (y_p.reshape(bp, s, d), y_s.reshape(bs, lq, d), p_re, p_im, cache_p(kt_p), cache_p(vt_p),
            jnp.swapaxes(lft_p, 1, 2), s_re, s_im, cache_s(k_s), cache_s(v_s), jnp.swapaxes(lft_s, 1, 2))
```

```python
import functools
import math

import jax
import jax.numpy as jnp
import numpy as np
from jax import lax
from jax.experimental import pallas as pl
from jax.experimental.pallas import tpu as pltpu

F32 = jnp.float32
BF16 = jnp.bfloat16

EPS = 1e-6
HEAD_DIM = 64
SSM_GROUP = 16
S5_CHUNK = 16
LANES = 128
_SLAB_GROUPS = LANES // SSM_GROUP
NEG_BIG = -1e30
VMEM_LIMIT = 52 * 1024 * 1024

FFN_TM = 1024
FFN_TF = 512
TOK_TM = 512
ATT_TQ = 256
ATT_TK = 256
ATT_HS = 2
DEC_TK = 512
DEC_HS = 4


def _cparams(sem):
    return pltpu.CompilerParams(dimension_semantics=sem, vmem_limit_bytes=VMEM_LIMIT)


def _rms_scale(x):
    return lax.rsqrt(jnp.mean(x * x, axis=-1, keepdims=True) + EPS)


def _dot(a, b):
    return jnp.dot(a, b, preferred_element_type=F32)


def _split2(x):
    hi = x.astype(BF16)
    lo = (x - hi.astype(F32)).astype(BF16)
    return hi, lo


def _part_maps(npb):
    first = lambda i, *_: (jnp.minimum(i, npb - 1), 0)
    second = lambda i, *_: (jnp.maximum(i - npb, 0), 0)
    return first, second


def _ffn_kernel(*refs, nf, npb, n_in, n_out, with_norm):
    x_refs, refs = refs[:n_in], refs[n_in:]
    g_ref, wg_ref, wu_ref, wd_ref = refs[:4]
    refs = refs[4:]
    g2_ref = refs[0] if with_norm else None
    refs = refs[with_norm:]
    o_refs, refs = refs[:n_out], refs[n_out:]
    o2_ref = refs[0] if with_norm else None
    h_scr, acc_scr = refs[with_norm:]
    i, f = pl.program_id(0), pl.program_id(1)

    def load_x():
        if n_in == 1:
            return x_refs[0][...]
        return jnp.where(i < npb, x_refs[0][...], x_refs[1][...])

    @pl.when(f == 0)
    def _():
        x = load_x()
        h_scr[...] = (x * _rms_scale(x) * g_ref[...]).astype(BF16)
        acc_scr[...] = jnp.zeros_like(acc_scr)

    h = h_scr[...]
    a = _dot(h, wg_ref[...])
    b = _dot(h, wu_ref[...])
    t = (a * jax.nn.sigmoid(a)) * b
    acc_scr[...] += _dot(t.astype(BF16), wd_ref[...])

    @pl.when(f == nf - 1)
    def _():
        y = load_x() + 0.5 * acc_scr[...]
        if n_out == 1:
            o_refs[0][...] = y
        else:
            @pl.when(i < npb)
            def _():
                o_refs[0][...] = y

            @pl.when(i >= npb)
            def _():
                o_refs[1][...] = y
        if with_norm:
            o2_ref[...] = (y * _rms_scale(y) * g2_ref[...]).astype(o2_ref.dtype)


def _ffn(xs, g, wg, wu, wd, g2=None, split_out=None, norm_dtype=BF16):
    d = xs[0].shape[1]
    n = sum(x.shape[0] for x in xs)
    dff = wg.shape[1]
    tm, tf = min(FFN_TM, n), min(FFN_TF, dff)
    nf = dff // tf
    n_first = xs[0].shape[0] if len(xs) == 2 else (split_out[0] if split_out else n)
    assert all(x.shape[0] % tm == 0 for x in xs) and n_first % tm == 0
    npb = n_first // tm
    first, second = _part_maps(npb)
    with_norm = g2 is not None
    row = pl.BlockSpec((tm, d), lambda i, f: (i, 0))
    vec = pl.BlockSpec((1, d), lambda i, f: (0, 0))
    parts = [pl.BlockSpec((tm, d), first), pl.BlockSpec((tm, d), second)]
    in_specs = (parts if len(xs) == 2 else [row]) + [
        vec, pl.BlockSpec((d, tf), lambda i, f: (0, f)), pl.BlockSpec((d, tf), lambda i, f: (0, f)),
        pl.BlockSpec((tf, d), lambda i, f: (f, 0))]
    args = list(xs) + [g.reshape(1, d), wg, wu, wd]
    if split_out:
        out_shape = [jax.ShapeDtypeStruct((m, d), F32) for m in split_out]
        out_specs = list(parts)
    else:
        out_shape = [jax.ShapeDtypeStruct((n, d), F32)]
        out_specs = [row]
    n_out = len(out_shape)
    if with_norm:
        in_specs.append(vec)
        args.append(g2.reshape(1, d))
        out_shape.append(jax.ShapeDtypeStruct((n, d), norm_dtype))
        out_specs.append(row)
    kern = functools.partial(_ffn_kernel, nf=nf, npb=npb, n_in=len(xs), n_out=n_out, with_norm=with_norm)
    res = pl.pallas_call(
        kern, grid=(n // tm, nf), in_specs=in_specs, out_specs=out_specs, out_shape=out_shape,
        scratch_shapes=[pltpu.VMEM((tm, d), BF16), pltpu.VMEM((tm, d), F32)],
        compiler_params=_cparams(("arbitrary", "arbitrary")), name="ffn")(*args)
    y = tuple(res[:n_out]) if split_out else res[0]
    return y, (res[n_out] if with_norm else None)


def _s5_param_kernel(ldt_ref, arc_ref, aic_ref, arr_ref, air_ref, btr_ref, bti_ref, ctr_ref, cti_ref,
                     m_ref, sbr_ref, sbi_ref, car_ref, cai_ref, a16r_ref, a16i_ref):
    t, c = S5_CHUNK, SSM_GROUP
    dt = jnp.exp(ldt_ref[0])
    slot = pl.program_id(0) % _SLAB_GROUPS

    def cpow(ar, ai, n):
        mag = jnp.exp(ar * dt * n)
        ang = ai * dt * n
        return mag * jnp.cos(ang), mag * jnp.sin(ang)

    def step_of(pos):
        seg = pos // c
        return (seg // _SLAB_GROUPS) * _SLAB_GROUPS + ((seg - slot) & (_SLAB_GROUPS - 1))

    arc, aic = arc_ref[0], aic_ref[0]
    arr, air = arr_ref[0], air_ref[0]
    p = arc.shape[0]
    lane_pc = lax.broadcasted_iota(jnp.int32, (p, t * c), 1)
    cr, ci = ctr_ref[0], cti_ref[0]
    pr, pi = cpow(arc, aic, step_of(lane_pc).astype(F32) + 1.0)
    car_ref[0] = (cr * pr - ci * pi).astype(BF16)
    cai_ref[0] = (-(cr * pi + ci * pr)).astype(BF16)
    pr, pi = cpow(arc, aic, (lane_pc // c).astype(F32))
    csr = cr * pr - ci * pi
    csi = cr * pi + ci * pr

    abr, abi = cpow(arr, air, 1.0)
    xr, xi = abr - 1.0, abi
    den = arr * arr + air * air
    qr = (xr * arr + xi * air) / den
    qi = (xi * arr - xr * air) / den
    btr, bti = btr_ref[0], bti_ref[0]
    bbr = qr * btr - qi * bti
    bbi = qr * bti + qi * btr
    iidx = step_of(lax.broadcasted_iota(jnp.int32, (t * c, p), 0)).astype(F32)
    pr, pi = cpow(arr, air, (t - 1.0) - iidx)
    sbr_ref[0] = (bbr * pr - bbi * pi).astype(BF16)
    sbi_ref[0] = (bbr * pi + bbi * pr).astype(BF16)
    pr, pi = cpow(arr, air, float(t))
    a16r_ref[0] = pr
    a16i_ref[0] = pi

    hp = lax.Precision.HIGHEST
    kt = (jnp.dot(bbr[:c], csr, precision=hp, preferred_element_type=F32)
          - jnp.dot(bbi[:c], csi, precision=hp, preferred_element_type=F32))
    lane = lax.broadcasted_iota(jnp.int32, (c, t * c), 1)
    for i in range(t):
        blk = kt if i == 0 else pltpu.roll(kt, i * c, 1)
        blk = jnp.where(lane >= i * c, blk, 0.0)
        blk = jnp.concatenate([pltpu.roll(blk[:, h * LANES:(h + 1) * LANES], slot * c, 1)
                               for h in range(t * c // LANES)], axis=1)
        seg = (i // _SLAB_GROUPS) * _SLAB_GROUPS + ((i + slot) & (_SLAB_GROUPS - 1))
        m_ref[0, pl.ds(pl.multiple_of(seg * c, c), c), :] = blk.astype(BF16)


def _s5_params(log_dt, a_re, a_im, b_re, b_im, c_re, c_im):
    g, p = a_re.shape
    t, c = S5_CHUNK, SSM_GROUP
    tc = t * c

    def tile_b(b):
        return jnp.broadcast_to(jnp.swapaxes(b, 1, 2)[:, None], (g, t, c, p)).reshape(g, tc, p)

    def tile_c(cm):
        return jnp.broadcast_to(jnp.swapaxes(cm, 1, 2)[:, :, None], (g, p, t, c)).reshape(g, p, tc)

    def spec(*shape):
        return pl.BlockSpec((1,) + shape, lambda i: (i, 0, 0))

    return pl.pallas_call(
        _s5_param_kernel, grid=(g,),
        in_specs=[spec(1, 1), spec(p, 1), spec(p, 1), spec(1, p), spec(1, p),
                  spec(tc, p), spec(tc, p), spec(p, tc), spec(p, tc)],
        out_specs=[spec(tc, tc), spec(tc, p), spec(tc, p), spec(p, tc), spec(p, tc), spec(1, p), spec(1, p)],
        out_shape=[jax.ShapeDtypeStruct((g, tc, tc), BF16),
                   jax.ShapeDtypeStruct((g, tc, p), BF16), jax.ShapeDtypeStruct((g, tc, p), BF16),
                   jax.ShapeDtypeStruct((g, p, tc), BF16), jax.ShapeDtypeStruct((g, p, tc), BF16),
                   jax.ShapeDtypeStruct((g, 1, p), F32), jax.ShapeDtypeStruct((g, 1, p), F32)],
        compiler_params=_cparams(("parallel",)), name="s5_params")(
            log_dt.reshape(g, 1, 1), a_re.reshape(g, p, 1), a_im.reshape(g, p, 1),
            a_re.reshape(g, 1, p), a_im.reshape(g, 1, p),
            tile_b(b_re), tile_b(b_im), tile_c(c_re), tile_c(c_im))


def _gelu_tanh(y):
    return 0.5 * y * (1.0 + jnp.tanh(math.sqrt(2.0 / math.pi) * (y + 0.044715 * (y * y * y))))


def _pick_segments(v, first):
    seg = lax.broadcasted_iota(jnp.int32, v[0].shape, 1) // SSM_GROUP
    out = v[(7 + first) % 8]
    for s in range(6, -1, -1):
        out = jnp.where(seg == s, v[(s + first) % 8], out)
    return out


def _s5_main_kernel(u_ref, m_ref, sbr_ref, sbi_ref, car_ref, cai_ref, a16r_ref, a16i_ref,
                    h0r_ref, h0i_ref, d_ref, z_ref, hpr_ref, hpi_ref, hsr_ref, hsi_ref,
                    u2_scr, sr_scr, si_scr, hr_scr, hi_scr, *, bp, nkp, bs, nks):
    t = S5_CHUNK
    npq = bp * nkp // 8
    nq = npq + bs * nks // 8
    sp = nkp * t

    def token0(q):
        return jnp.where(q < npq, (q % bp) * sp + (q // bp) * (8 * t), bp * sp + (q - npq) * (8 * t))

    def gather(q, carry):
        tok, rows = token0(q), pl.ds(pl.multiple_of(q * 8, 8), 8)
        for h in range(2):
            v = [u_ref[pl.ds(tok + 8 * h + i, 8, stride=t), :] for i in range(8)]
            v = [x if i == 0 else pltpu.roll(x, i * SSM_GROUP, 1) for i, x in enumerate(v)]
            for gi in range(8):
                u2_scr[gi, rows, h * LANES:(h + 1) * LANES] = _pick_segments(v, 8 - gi)
        return carry

    lax.fori_loop(0, nq, gather, 0, unroll=4)

    def group(gi, carry):
        u = u2_scr[gi]
        ub = u.astype(BF16)
        sr_scr[...] = _dot(ub, sbr_ref[gi])
        si_scr[...] = _dot(ub, sbi_ref[gi])
        ar, ai = a16r_ref[gi], a16i_ref[gi]

        def step(rows, hr, hi):
            hr_scr[rows, :] = hr
            hi_scr[rows, :] = hi
            return (ar * hr - ai * hi + sr_scr[rows, :], ar * hi + ai * hr + si_scr[rows, :])

        hr = hi = jnp.zeros((bp, ar.shape[1]), F32)
        for k in range(nkp):
            hr, hi = step(pl.ds((k // 8) * bp * 8 + k % 8, bp, stride=8), hr, hi)
        hpr_ref[gi] = hr
        hpi_ref[gi] = hi
        hr, hi = h0r_ref[gi], h0i_ref[gi]
        for k in range(nks):
            hr, hi = step(pl.ds(bp * nkp + k, bs, stride=nks), hr, hi)
        hsr_ref[gi] = hr
        hsi_ref[gi] = hi

        y = (_dot(ub, m_ref[gi]) + _dot(hr_scr[...].astype(BF16), car_ref[gi])
             + _dot(hi_scr[...].astype(BF16), cai_ref[gi]) + d_ref[gi] * u)
        u2_scr[gi] = _gelu_tanh(y)
        return carry

    lax.fori_loop(0, 8, group, 0)

    def scatter(q, carry):
        tok, rows = token0(q), pl.ds(pl.multiple_of(q * 8, 8), 8)
        for h in range(2):
            v = [u2_scr[gi, rows, h * LANES:(h + 1) * LANES] for gi in range(8)]
            for i in range(8):
                w = _pick_segments(v, 8 - i)
                w = w if i == 0 else pltpu.roll(w, LANES - i * SSM_GROUP, 1)
                z_ref[pl.ds(tok + 8 * h + i, 8, stride=t), :] = w
        return carry

    lax.fori_loop(0, nq, scatter, 0, unroll=4)


def _s5_layer(u, n_prompt, bp, bs, h0_re, h0_im, mats, d_skip):
    n, d = u.shape
    t, c = S5_CHUNK, SSM_GROUP
    g, tc = d // c, t * c
    gs = LANES // c
    p = h0_re.shape[-1]
    nkp = n_prompt // (bp * t)
    nks = (n - n_prompt) // (bs * t)
    r = nkp * bp + nks * bs
    assert gs == 8 and t == 16 and bp % 8 == 0 and nkp % 8 == 0 and (bs * nks) % 8 == 0

    m, sbr, sbi, car, cai, a16r, a16i = mats
    d_t = jnp.broadcast_to(d_skip.reshape(g, 1, 1, c), (g, 1, t, c)).reshape(g, 1, tc)
    h0r = jnp.swapaxes(h0_re, 0, 1)
    h0i = jnp.swapaxes(h0_im, 0, 1)

    def spec(*shape):
        return pl.BlockSpec((gs,) + shape, lambda i: (i, 0, 0))

    kern = functools.partial(_s5_main_kernel, bp=bp, nkp=nkp, bs=bs, nks=nks)
    z, hpr, hpi, hsr, hsi = pl.pallas_call(
        kern, grid=(g // gs,),
        in_specs=[pl.BlockSpec((n, LANES), lambda i: (0, i), pipeline_mode=pl.Buffered(1)),
                  spec(tc, tc), spec(tc, p), spec(tc, p), spec(p, tc), spec(p, tc),
                  spec(1, p), spec(1, p), spec(bs, p), spec(bs, p), spec(1, tc)],
        out_specs=[pl.BlockSpec((n, LANES), lambda i: (0, i)),
                   spec(bp, p), spec(bp, p), spec(bs, p), spec(bs, p)],
        out_shape=[jax.ShapeDtypeStruct((n, d), F32),
                   jax.ShapeDtypeStruct((g, bp, p), F32), jax.ShapeDtypeStruct((g, bp, p), F32),
                   jax.ShapeDtypeStruct((g, bs, p), F32), jax.ShapeDtypeStruct((g, bs, p), F32)],
        scratch_shapes=[pltpu.VMEM((gs, r, tc), F32)] + [pltpu.VMEM((r, p), F32) for _ in range(4)],
        compiler_params=_cparams(("arbitrary",)), name="s5_main")(
            u, m, sbr, sbi, car, cai, a16r, a16i, h0r, h0i, d_t)
    states = tuple(jnp.swapaxes(h, 0, 1)[:, None] for h in (hpr, hpi, hsr, hsi))
    return z, states


def _glu_kernel(x_ref, z_ref, wa_ref, wb_ref, o_ref):
    z = z_ref[...].astype(BF16)
    o_ref[...] = x_ref[...] + _dot(z, wa_ref[...]) * jax.nn.sigmoid(_dot(z, wb_ref[...]))


def _glu(x, z, wa, wb):
    n, d = x.shape
    tm = min(TOK_TM, n)
    row = pl.BlockSpec((tm, d), lambda i: (i, 0))
    mat = pl.BlockSpec((d, d), lambda i: (0, 0))
    return pl.pallas_call(
        _glu_kernel, grid=(n // tm,), in_specs=[row, row, mat, mat], out_specs=row,
        out_shape=jax.ShapeDtypeStruct((n, d), F32),
        compiler_params=_cparams(("parallel",)), name="glu")(x, z, wa, wb)


def _head_norm(x, e_ref, et_ref):
    hi, lo = _split2(x * x)
    ms = (_dot(hi, e_ref[...]) + _dot(lo, e_ref[...])) * (1.0 / HEAD_DIM)
    hi, lo = _split2(lax.rsqrt(ms + EPS))
    return _dot(hi, et_ref[...]) + _dot(lo, et_ref[...])


def _head_indicator(d):
    nh = d // HEAD_DIM
    e = (np.arange(d)[:, None] // HEAD_DIM == np.arange(LANES)[None, :]).astype(np.float32)
    assert nh <= LANES
    return jnp.asarray(e, BF16), jnp.asarray(e.T, BF16)


def _kvf_kernel(h_ref, wkv_ref, wf_ref, bf_ref, kn_ref, e_ref, et_ref,
                kp_ref, ks_ref, vp_ref, vs_ref, lp_ref, ls_ref, kb_ref, vb_ref, *, d, nh, npb):
    i = pl.program_id(0)
    h = h_ref[...]
    tm = h.shape[0]
    pkv = _dot(h, wkv_ref[...])
    kraw, v = pkv[:, :d], pkv[:, d:]
    k = kraw * _head_norm(kraw, e_ref, et_ref) * kn_ref[...]
    kb_ref[...] = k.astype(BF16)
    vb_ref[...] = v.astype(BF16)
    pf = _dot(h, wf_ref[...]) + bf_ref[...]
    lft = (jnp.minimum(pf, 0.0) - jnp.log1p(jnp.exp(-jnp.abs(pf)))).T[:nh]

    @pl.when(i < npb)
    def _():
        kp_ref[0] = k.T
        vp_ref[0] = v.T
        lp_ref[0] = lft

    @pl.when(i >= npb)
    def _():
        for hd in range(nh):
            rows = pl.ds(hd, tm, stride=nh)
            ks_ref[rows, :] = k[:, hd * HEAD_DIM:(hd + 1) * HEAD_DIM]
            vs_ref[rows, :] = v[:, hd * HEAD_DIM:(hd + 1) * HEAD_DIM]
        lq = ls_ref.shape[2]
        for j in range(tm // lq):
            ls_ref[j] = lft[:, j * lq:(j + 1) * lq]


def _kvf(hn, bp, s, lq, w_kv, w_f, b_f, k_norm):
    n, d = hn.shape
    nh = d // HEAD_DIM
    tm = min(TOK_TM, s)
    n_prompt = bp * s
    n_s = n - n_prompt
    assert s % tm == 0 and n_s % tm == 0 and tm % lq == 0
    npb, spb = n_prompt // tm, s // tm
    e, et = _head_indicator(d)
    row = pl.BlockSpec((tm, d), lambda i: (i, 0))
    const = lambda *sh: pl.BlockSpec(sh, lambda i: (0, 0))
    prompt_t = lambda rows: pl.BlockSpec(
        (1, rows, tm), lambda i: (jnp.minimum(i, npb - 1) // spb, 0, jnp.minimum(i, npb - 1) % spb))
    second = lambda i: (jnp.maximum(i - npb, 0), 0)
    kern = functools.partial(_kvf_kernel, d=d, nh=nh, npb=npb)
    return pl.pallas_call(
        kern, grid=(n // tm,),
        in_specs=[row, const(d, 2 * d), const(d, LANES), const(1, LANES), const(1, d), const(d, LANES), const(LANES, d)],
        out_specs=[prompt_t(d), pl.BlockSpec((tm * nh, HEAD_DIM), second),
                   prompt_t(d), pl.BlockSpec((tm * nh, HEAD_DIM), second),
                   prompt_t(nh), pl.BlockSpec((tm // lq, nh, lq), lambda i: (jnp.maximum(i - npb, 0), 0, 0)), row, row],
        out_shape=[jax.ShapeDtypeStruct((bp, d, s), F32), jax.ShapeDtypeStruct((n_s * nh, HEAD_DIM), F32),
                   jax.ShapeDtypeStruct((bp, d, s), F32), jax.ShapeDtypeStruct((n_s * nh, HEAD_DIM), F32),
                   jax.ShapeDtypeStruct((bp, nh, s), F32), jax.ShapeDtypeStruct((n_s // lq, nh, lq), F32),
                   jax.ShapeDtypeStruct((n, d), BF16), jax.ShapeDtypeStruct((n, d), BF16)],
        compiler_params=_cparams(("arbitrary",)), name="kvf")(
            hn, w_kv, jnp.pad(w_f, ((0, 0), (0, LANES - nh))), jnp.pad(b_f, (0, LANES - nh)).reshape(1, LANES),
            jnp.tile(k_norm, nh).reshape(1, d), e, et)


def _qproj_kernel(u_ref, wq_ref, qn_ref, e_ref, et_ref, q_ref):
    q = _dot(u_ref[...], wq_ref[...])
    q_ref[...] = (q * _head_norm(q, e_ref, et_ref) * qn_ref[...]).astype(BF16)


def _qproj(un, wq, qn_scaled):
    n, d = un.shape
    tm = min(TOK_TM, n)
    e, et = _head_indicator(d)
    row = pl.BlockSpec((tm, d), lambda i: (i, 0))
    const = lambda *s: pl.BlockSpec(s, lambda i: (0, 0))
    return pl.pallas_call(
        _qproj_kernel, grid=(n // tm,),
        in_specs=[row, const(d, d), const(1, d), const(d, LANES), const(LANES, d)], out_specs=row,
        out_shape=jax.ShapeDtypeStruct((n, d), BF16),
        compiler_params=_cparams(("parallel",)), name="qproj")(un, wq, qn_scaled, e, et)


def _oproj_kernel(x_ref, ap_ref, as_ref, wo_ref, o_ref, *, npb):
    a = jnp.where(pl.program_id(0) < npb, ap_ref[...], as_ref[...])
    o_ref[...] = x_ref[...] + _dot(a, wo_ref[...])


def _oproj(x, a_p, a_s, wo):
    n, d = x.shape
    tm = min(TOK_TM, n)
    assert a_p.shape[0] % tm == 0 and a_s.shape[0] % tm == 0
    npb = a_p.shape[0] // tm
    first, second = _part_maps(npb)
    row = pl.BlockSpec((tm, d), lambda i: (i, 0))
    return pl.pallas_call(
        functools.partial(_oproj_kernel, npb=npb), grid=(n // tm,),
        in_specs=[row, pl.BlockSpec((tm, d), first), pl.BlockSpec((tm, d), second),
                  pl.BlockSpec((d, d), lambda i: (0, 0))],
        out_specs=row, out_shape=jax.ShapeDtypeStruct((n, d), F32),
        compiler_params=_cparams(("parallel",)), name="oproj")(x, a_p, a_s, wo)


def _cumsum_kernel(x_ref, init_ref, o_ref, *, nblk, bw):
    tri = (lax.broadcasted_iota(jnp.int32, (bw, bw), 0) <= lax.broadcasted_iota(jnp.int32, (bw, bw), 1)).astype(BF16)
    carry = init_ref[...]
    for j in range(nblk):
        x = x_ref[:, j * bw:(j + 1) * bw]
        hi = x.astype(BF16)
        r1 = x - hi.astype(F32)
        mid = r1.astype(BF16)
        lo = (r1 - mid.astype(F32)).astype(BF16)
        cs = _dot(hi, tri) + _dot(mid, tri) + _dot(lo, tri) + carry
        o_ref[:, j * bw:(j + 1) * bw] = cs
        carry = cs[:, bw - 1:bw]


def _cumsum_rows(x, init):
    r, w = x.shape
    bw = min(256, w)
    kern = functools.partial(_cumsum_kernel, nblk=w // bw, bw=bw)
    return pl.pallas_call(
        kern, grid=(1,),
        in_specs=[pl.BlockSpec((r, w), lambda i: (0, 0)), pl.BlockSpec((r, 1), lambda i: (0, 0))],
        out_specs=pl.BlockSpec((r, w), lambda i: (0, 0)), out_shape=jax.ShapeDtypeStruct((r, w), F32),
        compiler_params=_cparams(("arbitrary",)), name="cumsum")(x, init)


def _lanes(x, w):
    if w <= LANES:
        return x[:, :w]
    return jnp.concatenate([x] * (w // LANES), axis=1)


def _head_masks(shape):
    head = lax.broadcasted_iota(jnp.int32, shape, 1) // HEAD_DIM
    return [head == i for i in range(shape[1] // HEAD_DIM)]


def _stack_heads(qs):
    zero = jnp.zeros_like(qs)
    return jnp.concatenate([jnp.where(mk, qs, zero) for mk in _head_masks(qs.shape)], axis=0)


def _flash_init(cq, hs, m_scr, l_scr, acc_scr, cq_scr):
    tq = cq.shape[0]
    m_scr[...] = jnp.full_like(m_scr, NEG_BIG)
    l_scr[...] = jnp.zeros_like(l_scr)
    acc_scr[...] = jnp.zeros_like(acc_scr)
    for g in range(cq_scr.shape[0]):
        for i in range(hs):
            h = g * hs + i
            cq_scr[g, i * tq:(i + 1) * tq, :] = jnp.broadcast_to(cq[:, h:h + 1], (tq, LANES))


def _flash_slab(g, q_st, ks, vs, ck_rows, causal, m_scr, l_scr, acc_scr, cq_scr, kv_t=False):
    hs = len(ck_rows)
    tq, tk = q_st.shape[0] // hs, ck_rows[0].shape[1]
    nt = (((1,), (1,)), ((), ()))
    s = _dot(q_st, ks) if kv_t else lax.dot_general(q_st, ks, nt, preferred_element_type=F32)
    t = jnp.concatenate([s[i * tq:(i + 1) * tq] - ck_rows[i] for i in range(hs)], axis=0)
    if causal:
        keep = lax.broadcasted_iota(jnp.int32, (tq, tk), 1) <= lax.broadcasted_iota(jnp.int32, (tq, tk), 0)
        t = jnp.where(jnp.concatenate([keep] * hs, axis=0), t, NEG_BIG)
    cq = cq_scr[g]
    m_prev = m_scr[g]
    m_new = jnp.maximum(m_prev, jnp.max(t, axis=1, keepdims=True) + cq)
    alpha = jnp.exp(m_prev - m_new)
    pe = jnp.exp(t - _lanes(m_new - cq, tk))
    l_scr[g] = alpha * l_scr[g] + jnp.sum(pe, axis=1, keepdims=True)
    pv = lax.dot_general(pe.astype(BF16), vs, nt, preferred_element_type=F32) if kv_t else _dot(pe.astype(BF16), vs)
    acc_scr[g] = _lanes(alpha, hs * HEAD_DIM) * acc_scr[g] + pv
    m_scr[g] = m_new


def _flash_finish(o_ref, hs, l_scr, acc_scr):
    tq = o_ref.shape[0]
    w = hs * HEAD_DIM
    masks = _head_masks((tq, w))
    for g in range(acc_scr.shape[0]):
        o = acc_scr[g] / _lanes(l_scr[g], w)
        out = o[(hs - 1) * tq:]
        for i in range(hs - 2, -1, -1):
            out = jnp.where(masks[i], o[i * tq:(i + 1) * tq], out)
        o_ref[:, g * w:(g + 1) * w] = out.astype(o_ref.dtype)


def _flash_scratch(ng, rows, w):
    return [pltpu.VMEM((ng, rows, LANES), F32), pltpu.VMEM((ng, rows, LANES), F32),
            pltpu.VMEM((ng, rows, w), F32), pltpu.VMEM((ng, rows, LANES), F32)]


def _attn_prompt_kernel(qi_tab, ki_tab, q_ref, k_ref, v_ref, cq_ref, ck_ref, o_ref, *scr, hs):
    step = pl.program_id(1)
    qi, ki = qi_tab[step], ki_tab[step]
    w = hs * HEAD_DIM
    ng = q_ref.shape[1] // w

    @pl.when(ki == 0)
    def _():
        _flash_init(cq_ref[...], hs, *scr)

    def sweep(causal):
        ck = ck_ref[0]
        for g in range(ng):
            sl = slice(g * w, (g + 1) * w)
            _flash_slab(g, _stack_heads(q_ref[:, sl]), k_ref[:, sl], v_ref[:, sl],
                        [ck[g * hs + i:g * hs + i + 1, :] for i in range(hs)], causal, *scr)

    @pl.when(ki < qi)
    def _():
        sweep(False)

    @pl.when(ki == qi)
    def _():
        sweep(True)
        _flash_finish(o_ref, hs, scr[1], scr[2])


def _attn_prompt(q, kb, vb, c_rows, c_t, nb, s):
    d = q.shape[1]
    nh = d // HEAD_DIM
    hs = ATT_HS
    tq = tk = min(ATT_TQ, s)
    nq = s // tq
    pairs = [(i, j) for i in range(nq) for j in range(i + 1)]
    qi_tab = jnp.asarray([p[0] for p in pairs], jnp.int32)
    ki_tab = jnp.asarray([p[1] for p in pairs], jnp.int32)
    grid_spec = pltpu.PrefetchScalarGridSpec(
        num_scalar_prefetch=2, grid=(nb, len(pairs)),
        in_specs=[pl.BlockSpec((tq, d), lambda b, p, qt, kt: (b * nq + qt[p], 0)),
                  pl.BlockSpec((tk, d), lambda b, p, qt, kt: (b * nq + kt[p], 0)),
                  pl.BlockSpec((tk, d), lambda b, p, qt, kt: (b * nq + kt[p], 0)),
                  pl.BlockSpec((tq, nh), lambda b, p, qt, kt: (b * nq + qt[p], 0)),
                  pl.BlockSpec((1, nh, tk), lambda b, p, qt, kt: (b, 0, kt[p]))],
        out_specs=pl.BlockSpec((tq, d), lambda b, p, qt, kt: (b * nq + qt[p], 0)),
        scratch_shapes=_flash_scratch(nh // hs, hs * tq, hs * HEAD_DIM))
    return pl.pallas_call(
        functools.partial(_attn_prompt_kernel, hs=hs), grid_spec=grid_spec,
        out_shape=jax.ShapeDtypeStruct((nb * s, d), BF16),
        compiler_params=_cparams(("parallel", "arbitrary")), name="attn_prompt")(
            qi_tab, ki_tab, q, kb, vb, c_rows, c_t)


def _attn_sample_kernel(q_ref, ck_ref, cv_ref, kn_ref, vn_ref, cq_ref, ckc_ref, ckn_ref, o_ref, *scr, hs, nkb):
    j = pl.program_id(1)
    w = hs * HEAD_DIM
    ng = q_ref.shape[1] // w

    @pl.when(j == 0)
    def _():
        _flash_init(cq_ref[0], hs, *scr)

    def sweep(kslab, vslab, ck, causal, kv_t):
        for g in range(ng):
            _flash_slab(g, _stack_heads(q_ref[:, g * w:(g + 1) * w]), kslab(g), vslab(g),
                        [ck[g * hs + i:g * hs + i + 1, :] for i in range(hs)], causal, *scr, kv_t=kv_t)

    @pl.when(j < nkb)
    def _():
        sweep(lambda g: ck_ref[0, g * w:(g + 1) * w, :].astype(BF16),
              lambda g: cv_ref[0, g * w:(g + 1) * w, :].astype(BF16), ckc_ref[0], False, True)

    @pl.when(j == nkb)
    def _():
        sweep(lambda g: kn_ref[:, g * w:(g + 1) * w], lambda g: vn_ref[:, g * w:(g + 1) * w], ckn_ref[0], True, False)
        _flash_finish(o_ref, hs, scr[1], scr[2])


def _attn_sample(q, kb, vb, cache_kt, cache_vt, c_new, c_past_t, c_new_t, n_prompt):
    d = q.shape[1]
    nh = d // HEAD_DIM
    nb, _, plen = cache_kt.shape
    lq = c_new.shape[1]
    hs = DEC_HS
    tk = min(DEC_TK, plen)
    nkb = plen // tk
    r0 = n_prompt // lq
    new_rows = pl.BlockSpec((lq, d), lambda b, j: (r0 + b, 0))
    cache = pl.BlockSpec((1, d, tk), lambda b, j: (b, 0, jnp.minimum(j, nkb - 1)))
    kern = functools.partial(_attn_sample_kernel, hs=hs, nkb=nkb)
    return pl.pallas_call(
        kern, grid=(nb, nkb + 1),
        in_specs=[new_rows, cache, cache, new_rows, new_rows,
                  pl.BlockSpec((1, lq, nh), lambda b, j: (b, 0, 0)),
                  pl.BlockSpec((1, nh, tk), lambda b, j: (b, 0, jnp.minimum(j, nkb - 1))),
                  pl.BlockSpec((1, nh, lq), lambda b, j: (b, 0, 0))],
        out_specs=pl.BlockSpec((lq, d), lambda b, j: (b, 0)),
        out_shape=jax.ShapeDtypeStruct((nb * lq, d), BF16),
        scratch_shapes=_flash_scratch(nh // hs, hs * lq, hs * HEAD_DIM),
        compiler_params=_cparams(("parallel", "arbitrary")), name="attn_sample")(
            q, cache_kt, cache_vt, kb, vb, c_new, c_past_t, c_new_t)


def kernel(x_prompt, x_sample, cache_k, cache_v, cache_logf, state_ssm_re, state_ssm_im, ffn_norm, w_ffn_gate, w_ffn_up, w_ffn_down, mix_norm, ssm_a_re, ssm_a_im, ssm_log_dt, ssm_b_re, ssm_b_im, ssm_c_re, ssm_c_im, ssm_d, w_glu_a, w_glu_b, kv_norm, w_kvf, b_f, k_norm, w_q, q_norm, w_o):
    bp, s, d = x_prompt.shape
    bs, lq, _ = x_sample.shape
    plen = cache_k.shape[1]
    nh = d // HEAD_DIM
    assert ffn_norm.shape[0] == 2 and ssm_a_re.shape[0] == 1 and w_q.shape[0] == 1
    n_prompt, n_sample = bp * s, bs * lq
    bf = lambda w: w.astype(BF16)

    def ffn(xs, l, j, **kw):
        return _ffn(xs, ffn_norm[l, j], bf(w_ffn_gate[l, j]), bf(w_ffn_up[l, j]), bf(w_ffn_down[l, j]), **kw)

    x, u = ffn([x_prompt.reshape(n_prompt, d), x_sample.reshape(n_sample, d)], 0, 0, g2=mix_norm[0], norm_dtype=F32)
    mats = _s5_params(ssm_log_dt[0], ssm_a_re[0], ssm_a_im[0], ssm_b_re[0], ssm_b_im[0], ssm_c_re[0], ssm_c_im[0])
    z, (p_re, p_im, s_re, s_im) = _s5_layer(u, n_prompt, bp, bs, state_ssm_re[:, 0], state_ssm_im[:, 0], mats, ssm_d[0])
    x = _glu(x, z, bf(w_glu_a[0]), bf(w_glu_b[0]))
    x, hn = ffn([x], 0, 1, g2=kv_norm)

    kt_p, k_s, vt_p, v_s, lft_p, lft_s, kb, vb = _kvf(hn, bp, s, lq, bf(w_kvf[:, :2 * d]), bf(w_kvf[:, 2 * d:]), b_f, k_norm)

    def cumsum_t(lft, init):
        b, _, l = lft.shape
        init = jnp.zeros((b * nh, 1), F32) if init is None else init.reshape(b * nh, 1)
        return _cumsum_rows(lft.reshape(b * nh, l), init).reshape(b, nh, l)

    cp_t = cumsum_t(lft_p, None)
    cpast_t = cumsum_t(jnp.swapaxes(cache_logf.astype(F32), 1, 2), None)
    cnew_t = cumsum_t(lft_s, cpast_t[:, :, -1])

    x, un = ffn([x], 1, 0, g2=mix_norm[1])
    q = _qproj(un, bf(w_q[0]), (jnp.tile(q_norm[0], nh) * (1.0 / math.sqrt(HEAD_DIM))).reshape(1, d))
    a_p = _attn_prompt(q, kb, vb, jnp.swapaxes(cp_t, 1, 2).reshape(n_prompt, nh), cp_t, bp, s)
    pos_minor = lambda c: jnp.transpose(c, (0, 2, 3, 1)).reshape(bs, d, plen)
    a_s = _attn_sample(q, kb, vb, pos_minor(cache_k), pos_minor(cache_v),
                       jnp.swapaxes(cnew_t, 1, 2), cpast_t, cnew_t, n_prompt)
    x = _oproj(x, a_p, a_s, bf(w_o[0]))
    (y_p, y_s), _ = ffn([x], 1, 1, split_out=(n_prompt, n_sample))

    cache_p = lambda a: jnp.transpose(a.reshape(bp, nh, HEAD_DIM, s), (0, 3, 1, 2))
    cache_s = lambda a: a.reshape(bs, lq, nh, HEAD_DIM)
    return (y_p.reshape(bp, s, d), y_s.reshape(bs, lq, d), p_re, p_im, cache_p(kt_p), cache_p(vt_p),
            jnp.swapaxes(lft_p, 1, 2), s_re, s_im, cache_s(k_s), cache_s(v_s), jnp.swapaxes(lft_s, 1, 2))
```

```python
import functools
import math

import jax
import jax.numpy as jnp
import numpy as np
from jax import lax
from jax.experimental import pallas as pl
from jax.experimental.pallas import tpu as pltpu

F32 = jnp.float32
BF16 = jnp.bfloat16

EPS = 1e-6
HEAD_DIM = 64
SSM_GROUP = 16
S5_CHUNK = 16
LANES = 128
_SLAB_GROUPS = LANES // SSM_GROUP
NEG_BIG = -1e30
VMEM_LIMIT = 57 * 1024 * 1024

FFN_TM = 1024
FFN_TF = 512
TOK_TM = 512
ATT_TQ = 256
ATT_TK = 512
ATT_HS = 2
DEC_TK = 1024
DEC_HS = 4


def _cparams(sem):
    return pltpu.CompilerParams(dimension_semantics=sem, vmem_limit_bytes=VMEM_LIMIT)


def _rms_scale(x):
    return lax.rsqrt(jnp.mean(x * x, axis=-1, keepdims=True) + EPS)


def _dot(a, b):
    return jnp.dot(a, b, preferred_element_type=F32)


def _split2(x):
    hi = x.astype(BF16)
    lo = (x - hi.astype(F32)).astype(BF16)
    return hi, lo


def _part_maps(npb):
    first = lambda i, *_: (jnp.minimum(i, npb - 1), 0)
    second = lambda i, *_: (jnp.maximum(i - npb, 0), 0)
    return first, second


def _ffn_kernel(*refs, nf, npb, n_in, n_out, with_norm):
    x_refs, refs = refs[:n_in], refs[n_in:]
    g_ref, wg_ref, wu_ref, wd_ref = refs[:4]
    refs = refs[4:]
    g2_ref = refs[0] if with_norm else None
    refs = refs[with_norm:]
    o_refs, refs = refs[:n_out], refs[n_out:]
    o2_ref = refs[0] if with_norm else None
    h_scr, acc_scr = refs[with_norm:]
    i, f = pl.program_id(0), pl.program_id(1)

    def load_x():
        if n_in == 1:
            return x_refs[0][...]
        return jnp.where(i < npb, x_refs[0][...], x_refs[1][...])

    @pl.when(f == 0)
    def _():
        x = load_x()
        h_scr[...] = (x * _rms_scale(x) * g_ref[...]).astype(BF16)
        acc_scr[...] = jnp.zeros_like(acc_scr)

    h = h_scr[...]
    a = _dot(h, wg_ref[...].astype(BF16))
    b = _dot(h, wu_ref[...].astype(BF16))
    t = (a * jax.nn.sigmoid(a)) * b
    acc_scr[...] += _dot(t.astype(BF16), wd_ref[...].astype(BF16))

    @pl.when(f == nf - 1)
    def _():
        y = load_x() + 0.5 * acc_scr[...]
        if n_out == 1:
            o_refs[0][...] = y
        else:
            @pl.when(i < npb)
            def _():
                o_refs[0][...] = y

            @pl.when(i >= npb)
            def _():
                o_refs[1][...] = y
        if with_norm:
            o2_ref[...] = (y * _rms_scale(y) * g2_ref[...]).astype(o2_ref.dtype)


def _ffn(xs, g, wg, wu, wd, lj, g2=None, split_out=None, norm_dtype=BF16):
    d = xs[0].shape[1]
    n = sum(x.shape[0] for x in xs)
    dff = wg.shape[-1]
    l, j = lj
    tm, tf = min(FFN_TM, n), min(FFN_TF, dff)
    nf = dff // tf
    n_first = xs[0].shape[0] if len(xs) == 2 else (split_out[0] if split_out else n)
    assert all(x.shape[0] % tm == 0 for x in xs) and n_first % tm == 0
    npb = n_first // tm
    first, second = _part_maps(npb)
    with_norm = g2 is not None
    row = pl.BlockSpec((tm, d), lambda i, f: (i, 0))
    vec = pl.BlockSpec((1, d), lambda i, f: (0, 0))
    parts = [pl.BlockSpec((tm, d), first), pl.BlockSpec((tm, d), second)]
    in_specs = (parts if len(xs) == 2 else [row]) + [
        vec, pl.BlockSpec((None, None, d, tf), lambda i, f: (l, j, 0, f)),
        pl.BlockSpec((None, None, d, tf), lambda i, f: (l, j, 0, f)),
        pl.BlockSpec((None, None, tf, d), lambda i, f: (l, j, f, 0))]
    args = list(xs) + [g.reshape(1, d), wg, wu, wd]
    if split_out:
        out_shape = [jax.ShapeDtypeStruct((m, d), F32) for m in split_out]
        out_specs = list(parts)
    else:
        out_shape = [jax.ShapeDtypeStruct((n, d), F32)]
        out_specs = [row]
    n_out = len(out_shape)
    if with_norm:
        in_specs.append(vec)
        args.append(g2.reshape(1, d))
        out_shape.append(jax.ShapeDtypeStruct((n, d), norm_dtype))
        out_specs.append(row)
    kern = functools.partial(_ffn_kernel, nf=nf, npb=npb, n_in=len(xs), n_out=n_out, with_norm=with_norm)
    res = pl.pallas_call(
        kern, grid=(n // tm, nf), in_specs=in_specs, out_specs=out_specs, out_shape=out_shape,
        scratch_shapes=[pltpu.VMEM((tm, d), BF16), pltpu.VMEM((tm, d), F32)],
        compiler_params=_cparams(("arbitrary", "arbitrary")), name="ffn")(*args)
    y = tuple(res[:n_out]) if split_out else res[0]
    return y, (res[n_out] if with_norm else None)


def _s5_param_kernel(*refs):
    lax.fori_loop(0, _SLAB_GROUPS, functools.partial(_s5_param_group, refs), 0)


def _s5_param_group(refs, slot, carry):
    (ldt_ref, arc_ref, aic_ref, arr_ref, air_ref, btr_ref, bti_ref, ctr_ref, cti_ref,
     m_ref, sbr_ref, sbi_ref, car_ref, cai_ref, a16r_ref, a16i_ref) = refs
    t, c = S5_CHUNK, SSM_GROUP
    dt = jnp.exp(ldt_ref[slot])

    def cpow(ar, ai, n):
        mag = jnp.exp(ar * dt * n)
        ang = ai * dt * n
        return mag * jnp.cos(ang), mag * jnp.sin(ang)

    def step_of(pos):
        seg = pos // c
        return (seg // _SLAB_GROUPS) * _SLAB_GROUPS + ((seg - slot) & (_SLAB_GROUPS - 1))

    arc, aic = arc_ref[slot], aic_ref[slot]
    arr, air = arr_ref[slot], air_ref[slot]
    p = arc.shape[0]
    lane_pc = lax.broadcasted_iota(jnp.int32, (p, t * c), 1)
    cr, ci = ctr_ref[slot], cti_ref[slot]
    pr, pi = cpow(arc, aic, step_of(lane_pc).astype(F32) + 1.0)
    car_ref[slot] = (cr * pr - ci * pi).astype(BF16)
    cai_ref[slot] = (-(cr * pi + ci * pr)).astype(BF16)
    pr, pi = cpow(arc, aic, (lane_pc // c).astype(F32))
    csr = cr * pr - ci * pi
    csi = cr * pi + ci * pr

    abr, abi = cpow(arr, air, 1.0)
    xr, xi = abr - 1.0, abi
    den = arr * arr + air * air
    qr = (xr * arr + xi * air) / den
    qi = (xi * arr - xr * air) / den
    btr, bti = btr_ref[slot], bti_ref[slot]
    bbr = qr * btr - qi * bti
    bbi = qr * bti + qi * btr
    iidx = step_of(lax.broadcasted_iota(jnp.int32, (t * c, p), 0)).astype(F32)
    pr, pi = cpow(arr, air, (t - 1.0) - iidx)
    sbr_ref[slot] = (bbr * pr - bbi * pi).astype(BF16)
    sbi_ref[slot] = (bbr * pi + bbi * pr).astype(BF16)
    pr, pi = cpow(arr, air, float(t))
    a16r_ref[slot] = pr
    a16i_ref[slot] = pi

    hp = lax.Precision.HIGHEST
    kt = (jnp.dot(bbr[:c], csr, precision=hp, preferred_element_type=F32)
          - jnp.dot(bbi[:c], csi, precision=hp, preferred_element_type=F32))
    lane = lax.broadcasted_iota(jnp.int32, (c, t * c), 1)
    for i in range(t):
        blk = kt if i == 0 else pltpu.roll(kt, i * c, 1)
        blk = jnp.where(lane >= i * c, blk, 0.0)
        blk = jnp.concatenate([pltpu.roll(blk[:, h * LANES:(h + 1) * LANES], slot * c, 1)
                               for h in range(t * c // LANES)], axis=1)
        seg = (i // _SLAB_GROUPS) * _SLAB_GROUPS + ((i + slot) & (_SLAB_GROUPS - 1))
        m_ref[slot, pl.ds(pl.multiple_of(seg * c, c), c), :] = blk.astype(BF16)
    return carry


def _s5_params(log_dt, a_re, a_im, b_re, b_im, c_re, c_im):
    g, p = a_re.shape
    t, c = S5_CHUNK, SSM_GROUP
    tc = t * c

    def tile_b(b):
        return jnp.broadcast_to(jnp.swapaxes(b, 1, 2)[:, None], (g, t, c, p)).reshape(g, tc, p)

    def tile_c(cm):
        return jnp.broadcast_to(jnp.swapaxes(cm, 1, 2)[:, :, None], (g, p, t, c)).reshape(g, p, tc)

    def spec(*shape):
        return pl.BlockSpec((_SLAB_GROUPS,) + shape, lambda i: (i, 0, 0))

    assert g % _SLAB_GROUPS == 0
    return pl.pallas_call(
        _s5_param_kernel, grid=(g // _SLAB_GROUPS,),
        in_specs=[spec(1, 1), spec(p, 1), spec(p, 1), spec(1, p), spec(1, p),
                  spec(tc, p), spec(tc, p), spec(p, tc), spec(p, tc)],
        out_specs=[spec(tc, tc), spec(tc, p), spec(tc, p), spec(p, tc), spec(p, tc), spec(1, p), spec(1, p)],
        out_shape=[jax.ShapeDtypeStruct((g, tc, tc), BF16),
                   jax.ShapeDtypeStruct((g, tc, p), BF16), jax.ShapeDtypeStruct((g, tc, p), BF16),
                   jax.ShapeDtypeStruct((g, p, tc), BF16), jax.ShapeDtypeStruct((g, p, tc), BF16),
                   jax.ShapeDtypeStruct((g, 1, p), F32), jax.ShapeDtypeStruct((g, 1, p), F32)],
        compiler_params=_cparams(("parallel",)), name="s5_params")(
            log_dt.reshape(g, 1, 1), a_re.reshape(g, p, 1), a_im.reshape(g, p, 1),
            a_re.reshape(g, 1, p), a_im.reshape(g, 1, p),
            tile_b(b_re), tile_b(b_im), tile_c(c_re), tile_c(c_im))


def _gelu_tanh(y):
    return 0.5 * y * (1.0 + jnp.tanh(math.sqrt(2.0 / math.pi) * (y + 0.044715 * (y * y * y))))


def _pick_segments(v, first):
    seg = lax.broadcasted_iota(jnp.int32, v[0].shape, 1) // SSM_GROUP
    out = v[(7 + first) % 8]
    for s in range(6, -1, -1):
        out = jnp.where(seg == s, v[(s + first) % 8], out)
    return out


def _s5_main_kernel(u_ref, m_ref, sbr_ref, sbi_ref, car_ref, cai_ref, a16r_ref, a16i_ref,
                    h0r_ref, h0i_ref, d_ref, z_ref, hpr_ref, hpi_ref, hsr_ref, hsi_ref,
                    u2_scr, sr_scr, si_scr, hr_scr, hi_scr, *, bp, nkp, bs, nks):
    t = S5_CHUNK
    npq = bp * nkp // 8
    nq = npq + bs * nks // 8
    sp = nkp * t

    def token0(q):
        return jnp.where(q < npq, (q % bp) * sp + (q // bp) * (8 * t), bp * sp + (q - npq) * (8 * t))

    def gather(q, carry):
        tok, rows = token0(q), pl.ds(pl.multiple_of(q * 8, 8), 8)
        for h in range(2):
            v = [u_ref[pl.ds(tok + 8 * h + i, 8, stride=t), :] for i in range(8)]
            v = [x if i == 0 else pltpu.roll(x, i * SSM_GROUP, 1) for i, x in enumerate(v)]
            for gi in range(8):
                u2_scr[gi, rows, h * LANES:(h + 1) * LANES] = _pick_segments(v, 8 - gi)
        return carry

    lax.fori_loop(0, nq, gather, 0, unroll=4)

    def group(gi, carry):
        u = u2_scr[gi]
        ub = u.astype(BF16)
        sr_scr[...] = _dot(ub, sbr_ref[gi])
        si_scr[...] = _dot(ub, sbi_ref[gi])
        ar, ai = a16r_ref[gi], a16i_ref[gi]

        def step(rows, hr, hi):
            hr_scr[rows, :] = hr
            hi_scr[rows, :] = hi
            return (ar * hr - ai * hi + sr_scr[rows, :], ar * hi + ai * hr + si_scr[rows, :])

        hr = hi = jnp.zeros((bp, ar.shape[1]), F32)
        for k in range(nkp):
            hr, hi = step(pl.ds((k // 8) * bp * 8 + k % 8, bp, stride=8), hr, hi)
        hpr_ref[gi] = hr
        hpi_ref[gi] = hi
        hr, hi = h0r_ref[gi], h0i_ref[gi]
        for k in range(nks):
            hr, hi = step(pl.ds(bp * nkp + k, bs, stride=nks), hr, hi)
        hsr_ref[gi] = hr
        hsi_ref[gi] = hi

        y = (_dot(ub, m_ref[gi]) + _dot(hr_scr[...].astype(BF16), car_ref[gi])
             + _dot(hi_scr[...].astype(BF16), cai_ref[gi]) + d_ref[gi] * u)
        u2_scr[gi] = _gelu_tanh(y)
        return carry

    lax.fori_loop(0, 8, group, 0)

    def scatter(q, carry):
        tok, rows = token0(q), pl.ds(pl.multiple_of(q * 8, 8), 8)
        for h in range(2):
            v = [u2_scr[gi, rows, h * LANES:(h + 1) * LANES] for gi in range(8)]
            for i in range(8):
                w = _pick_segments(v, 8 - i)
                w = w if i == 0 else pltpu.roll(w, LANES - i * SSM_GROUP, 1)
                z_ref[pl.ds(tok + 8 * h + i, 8, stride=t), :] = w
        return carry

    lax.fori_loop(0, nq, scatter, 0, unroll=4)


def _s5_layer(u, n_prompt, bp, bs, h0_re, h0_im, mats, d_skip):
    n, d = u.shape
    t, c = S5_CHUNK, SSM_GROUP
    g, tc = d // c, t * c
    gs = LANES // c
    p = h0_re.shape[-1]
    nkp = n_prompt // (bp * t)
    nks = (n - n_prompt) // (bs * t)
    r = nkp * bp + nks * bs
    assert gs == 8 and t == 16 and bp % 8 == 0 and nkp % 8 == 0 and (bs * nks) % 8 == 0

    m, sbr, sbi, car, cai, a16r, a16i = mats
    d_t = jnp.broadcast_to(d_skip.reshape(g, 1, 1, c), (g, 1, t, c)).reshape(g, 1, tc)
    h0r = jnp.swapaxes(h0_re, 0, 1)
    h0i = jnp.swapaxes(h0_im, 0, 1)

    def spec(*shape):
        return pl.BlockSpec((gs,) + shape, lambda i: (i, 0, 0))

    kern = functools.partial(_s5_main_kernel, bp=bp, nkp=nkp, bs=bs, nks=nks)
    z, hpr, hpi, hsr, hsi = pl.pallas_call(
        kern, grid=(g // gs,),
        in_specs=[pl.BlockSpec((n, LANES), lambda i: (0, i), pipeline_mode=pl.Buffered(1)),
                  spec(tc, tc), spec(tc, p), spec(tc, p), spec(p, tc), spec(p, tc),
                  spec(1, p), spec(1, p), spec(bs, p), spec(bs, p), spec(1, tc)],
        out_specs=[pl.BlockSpec((n, LANES), lambda i: (0, i)),
                   spec(bp, p), spec(bp, p), spec(bs, p), spec(bs, p)],
        out_shape=[jax.ShapeDtypeStruct((n, d), F32),
                   jax.ShapeDtypeStruct((g, bp, p), F32), jax.ShapeDtypeStruct((g, bp, p), F32),
                   jax.ShapeDtypeStruct((g, bs, p), F32), jax.ShapeDtypeStruct((g, bs, p), F32)],
        scratch_shapes=[pltpu.VMEM((gs, r, tc), F32)] + [pltpu.VMEM((r, p), F32) for _ in range(4)],
        compiler_params=_cparams(("arbitrary",)), name="s5_main")(
            u, m, sbr, sbi, car, cai, a16r, a16i, h0r, h0i, d_t)
    states = tuple(jnp.swapaxes(h, 0, 1)[:, None] for h in (hpr, hpi, hsr, hsi))
    return z, states


def _glu_kernel(x_ref, z_ref, wa_ref, wb_ref, o_ref):
    z = z_ref[...].astype(BF16)
    o_ref[...] = x_ref[...] + _dot(z, wa_ref[...]) * jax.nn.sigmoid(_dot(z, wb_ref[...]))


def _glu(x, z, wa, wb):
    n, d = x.shape
    tm = min(TOK_TM, n)
    row = pl.BlockSpec((tm, d), lambda i: (i, 0))
    mat = pl.BlockSpec((d, d), lambda i: (0, 0))
    return pl.pallas_call(
        _glu_kernel, grid=(n // tm,), in_specs=[row, row, mat, mat], out_specs=row,
        out_shape=jax.ShapeDtypeStruct((n, d), F32),
        compiler_params=_cparams(("parallel",)), name="glu")(x, z, wa, wb)


def _head_norm(x, e_ref, et_ref):
    hi, lo = _split2(x * x)
    ms = (_dot(hi, e_ref[...]) + _dot(lo, e_ref[...])) * (1.0 / HEAD_DIM)
    hi, lo = _split2(lax.rsqrt(ms + EPS))
    return _dot(hi, et_ref[...]) + _dot(lo, et_ref[...])


def _head_indicator(d):
    nh = d // HEAD_DIM
    e = (np.arange(d)[:, None] // HEAD_DIM == np.arange(LANES)[None, :]).astype(np.float32)
    assert nh <= LANES
    return jnp.asarray(e, BF16), jnp.asarray(e.T, BF16)


def _kvf_kernel(h_ref, wkv_ref, wf_ref, bf_ref, kn_ref, e_ref, et_ref,
                kp_ref, ks_ref, vp_ref, vs_ref, lp_ref, ls_ref, kb_ref, vb_ref, *, d, nh, npb):
    i = pl.program_id(0)
    h = h_ref[...]
    tm = h.shape[0]
    pkv = _dot(h, wkv_ref[...])
    kraw, v = pkv[:, :d], pkv[:, d:]
    k = kraw * _head_norm(kraw, e_ref, et_ref) * kn_ref[...]
    kb_ref[...] = k.astype(BF16)
    vb_ref[...] = v.astype(BF16)
    pf = _dot(h, wf_ref[...]) + bf_ref[...]
    lft = (jnp.minimum(pf, 0.0) - jnp.log1p(jnp.exp(-jnp.abs(pf)))).T[:nh]

    @pl.when(i < npb)
    def _():
        kp_ref[0] = k.T
        vp_ref[0] = v.T
        lp_ref[0] = lft

    @pl.when(i >= npb)
    def _():
        for hd in range(nh):
            rows = pl.ds(hd, tm, stride=nh)
            ks_ref[rows, :] = k[:, hd * HEAD_DIM:(hd + 1) * HEAD_DIM]
            vs_ref[rows, :] = v[:, hd * HEAD_DIM:(hd + 1) * HEAD_DIM]
        lq = ls_ref.shape[2]
        for j in range(tm // lq):
            ls_ref[j] = lft[:, j * lq:(j + 1) * lq]


def _kvf(hn, bp, s, lq, w_kv, w_f, b_f, k_norm):
    n, d = hn.shape
    nh = d // HEAD_DIM
    tm = min(TOK_TM, s)
    n_prompt = bp * s
    n_s = n - n_prompt
    assert s % tm == 0 and n_s % tm == 0 and tm % lq == 0
    npb, spb = n_prompt // tm, s // tm
    e, et = _head_indicator(d)
    row = pl.BlockSpec((tm, d), lambda i: (i, 0))
    const = lambda *sh: pl.BlockSpec(sh, lambda i: (0, 0))
    prompt_t = lambda rows: pl.BlockSpec(
        (1, rows, tm), lambda i: (jnp.minimum(i, npb - 1) // spb, 0, jnp.minimum(i, npb - 1) % spb))
    second = lambda i: (jnp.maximum(i - npb, 0), 0)
    kern = functools.partial(_kvf_kernel, d=d, nh=nh, npb=npb)
    return pl.pallas_call(
        kern, grid=(n // tm,),
        in_specs=[row, const(d, 2 * d), const(d, LANES), const(1, LANES), const(1, d), const(d, LANES), const(LANES, d)],
        out_specs=[prompt_t(d), pl.BlockSpec((tm * nh, HEAD_DIM), second),
                   prompt_t(d), pl.BlockSpec((tm * nh, HEAD_DIM), second),
                   prompt_t(nh), pl.BlockSpec((tm // lq, nh, lq), lambda i: (jnp.maximum(i - npb, 0), 0, 0)), row, row],
        out_shape=[jax.ShapeDtypeStruct((bp, d, s), F32), jax.ShapeDtypeStruct((n_s * nh, HEAD_DIM), F32),
                   jax.ShapeDtypeStruct((bp, d, s), F32), jax.ShapeDtypeStruct((n_s * nh, HEAD_DIM), F32),
                   jax.ShapeDtypeStruct((bp, nh, s), F32), jax.ShapeDtypeStruct((n_s // lq, nh, lq), F32),
                   jax.ShapeDtypeStruct((n, d), BF16), jax.ShapeDtypeStruct((n, d), BF16)],
        compiler_params=_cparams(("arbitrary",)), name="kvf")(
            hn, w_kv, jnp.pad(w_f, ((0, 0), (0, LANES - nh))), jnp.pad(b_f, (0, LANES - nh)).reshape(1, LANES),
            jnp.tile(k_norm, nh).reshape(1, d), e, et)


def _qproj_kernel(u_ref, wq_ref, qn_ref, e_ref, et_ref, q_ref):
    q = _dot(u_ref[...], wq_ref[...])
    q_ref[...] = (q * _head_norm(q, e_ref, et_ref) * qn_ref[...]).astype(BF16)


def _qproj(un, wq, qn_scaled):
    n, d = un.shape
    tm = min(TOK_TM, n)
    e, et = _head_indicator(d)
    row = pl.BlockSpec((tm, d), lambda i: (i, 0))
    const = lambda *s: pl.BlockSpec(s, lambda i: (0, 0))
    return pl.pallas_call(
        _qproj_kernel, grid=(n // tm,),
        in_specs=[row, const(d, d), const(1, d), const(d, LANES), const(LANES, d)], out_specs=row,
        out_shape=jax.ShapeDtypeStruct((n, d), BF16),
        compiler_params=_cparams(("parallel",)), name="qproj")(un, wq, qn_scaled, e, et)


def _oproj_kernel(x_ref, ap_ref, as_ref, wo_ref, o_ref, *, npb):
    a = jnp.where(pl.program_id(0) < npb, ap_ref[...], as_ref[...])
    o_ref[...] = x_ref[...] + _dot(a, wo_ref[...])


def _oproj(x, a_p, a_s, wo):
    n, d = x.shape
    tm = min(TOK_TM, n)
    assert a_p.shape[0] % tm == 0 and a_s.shape[0] % tm == 0
    npb = a_p.shape[0] // tm
    first, second = _part_maps(npb)
    row = pl.BlockSpec((tm, d), lambda i: (i, 0))
    return pl.pallas_call(
        functools.partial(_oproj_kernel, npb=npb), grid=(n // tm,),
        in_specs=[row, pl.BlockSpec((tm, d), first), pl.BlockSpec((tm, d), second),
                  pl.BlockSpec((d, d), lambda i: (0, 0))],
        out_specs=row, out_shape=jax.ShapeDtypeStruct((n, d), F32),
        compiler_params=_cparams(("parallel",)), name="oproj")(x, a_p, a_s, wo)


def _cumsum_kernel(x_ref, init_ref, o_ref, *, nblk, bw):
    tri = (lax.broadcasted_iota(jnp.int32, (bw, bw), 0) <= lax.broadcasted_iota(jnp.int32, (bw, bw), 1)).astype(BF16)
    carry = init_ref[...]
    for j in range(nblk):
        x = x_ref[:, j * bw:(j + 1) * bw]
        hi = x.astype(BF16)
        r1 = x - hi.astype(F32)
        mid = r1.astype(BF16)
        lo = (r1 - mid.astype(F32)).astype(BF16)
        cs = _dot(hi, tri) + _dot(mid, tri) + _dot(lo, tri) + carry
        o_ref[:, j * bw:(j + 1) * bw] = cs
        carry = cs[:, bw - 1:bw]


def _cumsum_rows(x, init):
    r, w = x.shape
    bw = min(256, w)
    kern = functools.partial(_cumsum_kernel, nblk=w // bw, bw=bw)
    return pl.pallas_call(
        kern, grid=(1,),
        in_specs=[pl.BlockSpec((r, w), lambda i: (0, 0)), pl.BlockSpec((r, 1), lambda i: (0, 0))],
        out_specs=pl.BlockSpec((r, w), lambda i: (0, 0)), out_shape=jax.ShapeDtypeStruct((r, w), F32),
        compiler_params=_cparams(("arbitrary",)), name="cumsum")(x, init)


def _lanes(x, w):
    if w <= LANES:
        return x[:, :w]
    return jnp.concatenate([x] * (w // LANES), axis=1)


def _head_masks(shape):
    head = lax.broadcasted_iota(jnp.int32, shape, 1) // HEAD_DIM
    return [head == i for i in range(shape[1] // HEAD_DIM)]


def _stack_heads(qs):
    zero = jnp.zeros_like(qs)
    return jnp.concatenate([jnp.where(mk, qs, zero) for mk in _head_masks(qs.shape)], axis=0)


def _flash_init(cq, hs, m_scr, l_scr, acc_scr, cq_scr):
    tq = cq.shape[0]
    m_scr[...] = jnp.full_like(m_scr, NEG_BIG)
    l_scr[...] = jnp.zeros_like(l_scr)
    acc_scr[...] = jnp.zeros_like(acc_scr)
    for g in range(cq_scr.shape[0]):
        for i in range(hs):
            h = g * hs + i
            cq_scr[g, i * tq:(i + 1) * tq, :] = jnp.broadcast_to(cq[:, h:h + 1], (tq, LANES))


def _flash_slab(g, q_st, ks, vs, ck_rows, causal, m_scr, l_scr, acc_scr, cq_scr, kv_t=False):
    hs = len(ck_rows)
    tq, tk = q_st.shape[0] // hs, ck_rows[0].shape[1]
    nt = (((1,), (1,)), ((), ()))
    s = _dot(q_st, ks) if kv_t else lax.dot_general(q_st, ks, nt, preferred_element_type=F32)
    t = jnp.concatenate([s[i * tq:(i + 1) * tq] - ck_rows[i] for i in range(hs)], axis=0)
    if causal is not None:
        keep = lax.broadcasted_iota(jnp.int32, (tq, tk), 1) <= lax.broadcasted_iota(jnp.int32, (tq, tk), 0) + causal
        t = jnp.where(jnp.concatenate([keep] * hs, axis=0), t, NEG_BIG)
    cq = cq_scr[g]
    m_prev = m_scr[g]
    m_new = jnp.maximum(m_prev, jnp.max(t, axis=1, keepdims=True) + cq)
    alpha = jnp.exp(m_prev - m_new)
    pe = jnp.exp(t - _lanes(m_new - cq, tk))
    l_scr[g] = alpha * l_scr[g] + jnp.sum(pe, axis=1, keepdims=True)
    pv = lax.dot_general(pe.astype(BF16), vs, nt, preferred_element_type=F32) if kv_t else _dot(pe.astype(BF16), vs)
    acc_scr[g] = _lanes(alpha, hs * HEAD_DIM) * acc_scr[g] + pv
    m_scr[g] = m_new


def _flash_finish(o_ref, hs, l_scr, acc_scr):
    tq = o_ref.shape[0]
    w = hs * HEAD_DIM
    masks = _head_masks((tq, w))
    for g in range(acc_scr.shape[0]):
        o = acc_scr[g] / _lanes(l_scr[g], w)
        out = o[(hs - 1) * tq:]
        for i in range(hs - 2, -1, -1):
            out = jnp.where(masks[i], o[i * tq:(i + 1) * tq], out)
        o_ref[:, g * w:(g + 1) * w] = out.astype(o_ref.dtype)


def _flash_scratch(ng, rows, w):
    return [pltpu.VMEM((ng, rows, LANES), F32), pltpu.VMEM((ng, rows, LANES), F32),
            pltpu.VMEM((ng, rows, w), F32), pltpu.VMEM((ng, rows, LANES), F32)]


def _attn_prompt_kernel(qi_tab, ki_tab, q_ref, k_ref, v_ref, cq_ref, ck_ref, o_ref, *scr, hs):
    step = pl.program_id(1)
    qi, ki = qi_tab[step], ki_tab[step]
    tq, tk = q_ref.shape[0], k_ref.shape[0]
    w = hs * HEAD_DIM
    ng = q_ref.shape[1] // w
    last = (qi * tq + tq - 1) // tk

    @pl.when(ki == 0)
    def _():
        _flash_init(cq_ref[...], hs, *scr)

    def sweep(causal):
        ck = ck_ref[0]
        for g in range(ng):
            sl = slice(g * w, (g + 1) * w)
            _flash_slab(g, _stack_heads(q_ref[:, sl]), k_ref[:, sl], v_ref[:, sl],
                        [ck[g * hs + i:g * hs + i + 1, :] for i in range(hs)], causal, *scr)

    @pl.when(ki < last)
    def _():
        sweep(None)

    @pl.when(ki == last)
    def _():
        sweep(qi * tq - ki * tk)
        _flash_finish(o_ref, hs, scr[1], scr[2])


def _attn_prompt(q, kb, vb, c_rows, c_t, nb, s):
    d = q.shape[1]
    nh = d // HEAD_DIM
    hs = ATT_HS
    tq, tk = min(ATT_TQ, s), min(ATT_TK, s)
    assert tk % tq == 0 and s % tk == 0
    nq, nk = s // tq, s // tk
    pairs = [(i, j) for i in range(nq) for j in range((i * tq + tq - 1) // tk + 1)]
    qi_tab = jnp.asarray([p[0] for p in pairs], jnp.int32)
    ki_tab = jnp.asarray([p[1] for p in pairs], jnp.int32)
    grid_spec = pltpu.PrefetchScalarGridSpec(
        num_scalar_prefetch=2, grid=(nb, len(pairs)),
        in_specs=[pl.BlockSpec((tq, d), lambda b, p, qt, kt: (b * nq + qt[p], 0)),
                  pl.BlockSpec((tk, d), lambda b, p, qt, kt: (b * nk + kt[p], 0)),
                  pl.BlockSpec((tk, d), lambda b, p, qt, kt: (b * nk + kt[p], 0)),
                  pl.BlockSpec((tq, nh), lambda b, p, qt, kt: (b * nq + qt[p], 0)),
                  pl.BlockSpec((1, nh, tk), lambda b, p, qt, kt: (b, 0, kt[p]))],
        out_specs=pl.BlockSpec((tq, d), lambda b, p, qt, kt: (b * nq + qt[p], 0)),
        scratch_shapes=_flash_scratch(nh // hs, hs * tq, hs * HEAD_DIM))
    return pl.pallas_call(
        functools.partial(_attn_prompt_kernel, hs=hs), grid_spec=grid_spec,
        out_shape=jax.ShapeDtypeStruct((nb * s, d), BF16),
        compiler_params=_cparams(("parallel", "arbitrary")), name="attn_prompt")(
            qi_tab, ki_tab, q, kb, vb, c_rows, c_t)


def _attn_sample_kernel(q_ref, ck_ref, cv_ref, kn_ref, vn_ref, cq_ref, ckc_ref, ckn_ref, o_ref, *scr, hs, nkb):
    j = pl.program_id(1)
    w = hs * HEAD_DIM
    ng = q_ref.shape[1] // w

    @pl.when(j == 0)
    def _():
        _flash_init(cq_ref[0], hs, *scr)

    def sweep(kslab, vslab, ck, causal, kv_t):
        for g in range(ng):
            _flash_slab(g, _stack_heads(q_ref[:, g * w:(g + 1) * w]), kslab(g), vslab(g),
                        [ck[g * hs + i:g * hs + i + 1, :] for i in range(hs)], causal, *scr, kv_t=kv_t)

    @pl.when(j < nkb)
    def _():
        sweep(lambda g: ck_ref[0, g * w:(g + 1) * w, :].astype(BF16),
              lambda g: cv_ref[0, g * w:(g + 1) * w, :].astype(BF16), ckc_ref[0], None, True)

    @pl.when(j == nkb)
    def _():
        sweep(lambda g: kn_ref[:, g * w:(g + 1) * w], lambda g: vn_ref[:, g * w:(g + 1) * w], ckn_ref[0], 0, False)
        _flash_finish(o_ref, hs, scr[1], scr[2])


def _attn_sample(q, kb, vb, cache_kt, cache_vt, c_new, c_past_t, c_new_t, n_prompt):
    d = q.shape[1]
    nh = d // HEAD_DIM
    nb, _, plen = cache_kt.shape
    lq = c_new.shape[1]
    hs = DEC_HS
    tk = min(DEC_TK, plen)
    nkb = plen // tk
    r0 = n_prompt // lq
    new_rows = pl.BlockSpec((lq, d), lambda b, j: (r0 + b, 0))
    cache = pl.BlockSpec((1, d, tk), lambda b, j: (b, 0, jnp.minimum(j, nkb - 1)))
    kern = functools.partial(_attn_sample_kernel, hs=hs, nkb=nkb)
    return pl.pallas_call(
        kern, grid=(nb, nkb + 1),
        in_specs=[new_rows, cache, cache, new_rows, new_rows,
                  pl.BlockSpec((1, lq, nh), lambda b, j: (b, 0, 0)),
                  pl.BlockSpec((1, nh, tk), lambda b, j: (b, 0, jnp.minimum(j, nkb - 1))),
                  pl.BlockSpec((1, nh, lq), lambda b, j: (b, 0, 0))],
        out_specs=pl.BlockSpec((lq, d), lambda b, j: (b, 0)),
        out_shape=jax.ShapeDtypeStruct((nb * lq, d), BF16),
        scratch_shapes=_flash_scratch(nh // hs, hs * lq, hs * HEAD_DIM),
        compiler_params=_cparams(("parallel", "arbitrary")), name="attn_sample")(
            q, cache_kt, cache_vt, kb, vb, c_new, c_past_t, c_new_t)


def kernel(x_prompt, x_sample, cache_k, cache_v, cache_logf, state_ssm_re, state_ssm_im, ffn_norm, w_ffn_gate, w_ffn_up, w_ffn_down, mix_norm, ssm_a_re, ssm_a_im, ssm_log_dt, ssm_b_re, ssm_b_im, ssm_c_re, ssm_c_im, ssm_d, w_glu_a, w_glu_b, kv_norm, w_kvf, b_f, k_norm, w_q, q_norm, w_o):
    bp, s, d = x_prompt.shape
    bs, lq, _ = x_sample.shape
    plen = cache_k.shape[1]
    nh = d // HEAD_DIM
    assert ffn_norm.shape[0] == 2 and ssm_a_re.shape[0] == 1 and w_q.shape[0] == 1
    n_prompt, n_sample = bp * s, bs * lq
    bf = lambda w: w.astype(BF16)

    def ffn(xs, l, j, **kw):
        return _ffn(xs, ffn_norm[l, j], w_ffn_gate, w_ffn_up, w_ffn_down, (l, j), **kw)

    x, u = ffn([x_prompt.reshape(n_prompt, d), x_sample.reshape(n_sample, d)], 0, 0, g2=mix_norm[0], norm_dtype=F32)
    mats = _s5_params(ssm_log_dt[0], ssm_a_re[0], ssm_a_im[0], ssm_b_re[0], ssm_b_im[0], ssm_c_re[0], ssm_c_im[0])
    z, (p_re, p_im, s_re, s_im) = _s5_layer(u, n_prompt, bp, bs, state_ssm_re[:, 0], state_ssm_im[:, 0], mats, ssm_d[0])
    x = _glu(x, z, bf(w_glu_a[0]), bf(w_glu_b[0]))
    x, hn = ffn([x], 0, 1, g2=kv_norm)

    kt_p, k_s, vt_p, v_s, lft_p, lft_s, kb, vb = _kvf(hn, bp, s, lq, bf(w_kvf[:, :2 * d]), bf(w_kvf[:, 2 * d:]), b_f, k_norm)

    def cumsum_t(lft, init):
        b, _, l = lft.shape
        init = jnp.zeros((b * nh, 1), F32) if init is None else init.reshape(b * nh, 1)
        return _cumsum_rows(lft.reshape(b * nh, l), init).reshape(b, nh, l)

    cp_t = cumsum_t(lft_p, None)
    cpast_t = cumsum_t(jnp.swapaxes(cache_logf.astype(F32), 1, 2), None)
    cnew_t = cumsum_t(lft_s, cpast_t[:, :, -1])

    x, un = ffn([x], 1, 0, g2=mix_norm[1])
    q = _qproj(un, bf(w_q[0]), (jnp.tile(q_norm[0], nh) * (1.0 / math.sqrt(HEAD_DIM))).reshape(1, d))
    a_p = _attn_prompt(q, kb, vb, jnp.swapaxes(cp_t, 1, 2).reshape(n_prompt, nh), cp_t, bp, s)
    pos_minor = lambda c: jnp.transpose(c, (0, 2, 3, 1)).reshape(bs, d, plen)
    a_s = _attn_sample(q, kb, vb, pos_minor(cache_k), pos_minor(cache_v),
                       jnp.swapaxes(cnew_t, 1, 2), cpast_t, cnew_t, n_prompt)
    x = _oproj(x, a_p, a_s, bf(w_o[0]))
    (y_p, y_s), _ = ffn([x], 1, 1, split_out=(n_prompt, n_sample))

    cache_p = lambda a: jnp.transpose(a.reshape(bp, nh, HEAD_DIM, s), (0, 3, 1, 2))
    cache_s = lambda a: a.reshape(bs, lq, nh, HEAD_DIM)
    return (y_p.reshape(bp, s, d), y_s.reshape(bs, lq, d), p_re, p_im, cache_p(kt_p), cache_p(vt_p),
            jnp.swapaxes(lft_p, 1, 2), s_re, s_im, cache_s(k_s), cache_s(v_s), jnp.swapaxes(lft_s, 1, 2))
```

```python
import functools
import math

import jax
import jax.numpy as jnp
import numpy as np
from jax import lax
from jax.experimental import pallas as pl
from jax.experimental.pallas import tpu as pltpu

F32 = jnp.float32
BF16 = jnp.bfloat16

EPS = 1e-6
HEAD_DIM = 64
SSM_GROUP = 16
S5_CHUNK = 16
LANES = 128
_SLAB_GROUPS = LANES // SSM_GROUP
NEG_BIG = -1e30
VMEM_LIMIT = 57 * 1024 * 1024

FFN_TM = 1024
FFN_TF = 512
TOK_TM = 512
ATT_TQ = 256
ATT_TK = 256
ATT_HS = 2
DEC_TK = 1024
DEC_HS = 4


def _cparams(sem):
    return pltpu.CompilerParams(dimension_semantics=sem, vmem_limit_bytes=VMEM_LIMIT)


def _rms_scale(x):
    return lax.rsqrt(jnp.mean(x * x, axis=-1, keepdims=True) + EPS)


def _dot(a, b):
    return jnp.dot(a, b, preferred_element_type=F32)


def _split2(x):
    hi = x.astype(BF16)
    lo = (x - hi.astype(F32)).astype(BF16)
    return hi, lo


def _split3(x):
    hi = x.astype(BF16)
    r1 = x - hi.astype(F32)
    mid = r1.astype(BF16)
    return hi, mid, (r1 - mid.astype(F32)).astype(BF16)


def _part_maps(npb):
    first = lambda i, *_: (jnp.minimum(i, npb - 1), 0)
    second = lambda i, *_: (jnp.maximum(i - npb, 0), 0)
    return first, second


def _ffn_kernel(*refs, nf, npb, n_in, n_out, with_norm):
    x_refs, refs = refs[:n_in], refs[n_in:]
    g_ref, wg_ref, wu_ref, wd_ref = refs[:4]
    refs = refs[4:]
    g2_ref = refs[0] if with_norm else None
    refs = refs[with_norm:]
    o_refs, refs = refs[:n_out], refs[n_out:]
    o2_ref = refs[0] if with_norm else None
    h_scr, acc_scr = refs[with_norm:]
    i, f = pl.program_id(0), pl.program_id(1)

    def load_x():
        if n_in == 1:
            return x_refs[0][...]
        return jnp.where(i < npb, x_refs[0][...], x_refs[1][...])

    @pl.when(f == 0)
    def _():
        x = load_x()
        h_scr[...] = (x * _rms_scale(x) * g_ref[...]).astype(BF16)
        acc_scr[...] = jnp.zeros_like(acc_scr)

    h = h_scr[...]
    a = _dot(h, wg_ref[...].astype(BF16))
    b = _dot(h, wu_ref[...].astype(BF16))
    t = (a * jax.nn.sigmoid(a)) * b
    acc_scr[...] += _dot(t.astype(BF16), wd_ref[...].astype(BF16))

    @pl.when(f == nf - 1)
    def _():
        y = load_x() + 0.5 * acc_scr[...]
        if n_out == 1:
            o_refs[0][...] = y
        else:
            @pl.when(i < npb)
            def _():
                o_refs[0][...] = y

            @pl.when(i >= npb)
            def _():
                o_refs[1][...] = y
        if with_norm:
            o2_ref[...] = (y * _rms_scale(y) * g2_ref[...]).astype(o2_ref.dtype)


def _ffn(xs, g, wg, wu, wd, lj, g2=None, split_out=None, norm_dtype=BF16):
    d = xs[0].shape[1]
    n = sum(x.shape[0] for x in xs)
    dff = wg.shape[-1]
    l, j = lj
    tm, tf = min(FFN_TM, n), min(FFN_TF, dff)
    nf = dff // tf
    n_first = xs[0].shape[0] if len(xs) == 2 else (split_out[0] if split_out else n)
    assert all(x.shape[0] % tm == 0 for x in xs) and n_first % tm == 0
    npb = n_first // tm
    first, second = _part_maps(npb)
    with_norm = g2 is not None
    row = pl.BlockSpec((tm, d), lambda i, f: (i, 0))
    vec = pl.BlockSpec((1, d), lambda i, f: (0, 0))
    parts = [pl.BlockSpec((tm, d), first), pl.BlockSpec((tm, d), second)]
    in_specs = (parts if len(xs) == 2 else [row]) + [
        vec, pl.BlockSpec((None, None, d, tf), lambda i, f: (l, j, 0, f)),
        pl.BlockSpec((None, None, d, tf), lambda i, f: (l, j, 0, f)),
        pl.BlockSpec((None, None, tf, d), lambda i, f: (l, j, f, 0))]
    args = list(xs) + [g.reshape(1, d), wg, wu, wd]
    if split_out:
        out_shape = [jax.ShapeDtypeStruct((m, d), F32) for m in split_out]
        out_specs = list(parts)
    else:
        out_shape = [jax.ShapeDtypeStruct((n, d), F32)]
        out_specs = [row]
    n_out = len(out_shape)
    if with_norm:
        in_specs.append(vec)
        args.append(g2.reshape(1, d))
        out_shape.append(jax.ShapeDtypeStruct((n, d), norm_dtype))
        out_specs.append(row)
    kern = functools.partial(_ffn_kernel, nf=nf, npb=npb, n_in=len(xs), n_out=n_out, with_norm=with_norm)
    res = pl.pallas_call(
        kern, grid=(n // tm, nf), in_specs=in_specs, out_specs=out_specs, out_shape=out_shape,
        scratch_shapes=[pltpu.VMEM((tm, d), BF16), pltpu.VMEM((tm, d), F32)],
        compiler_params=_cparams(("arbitrary", "arbitrary")), name="ffn")(*args)
    y = tuple(res[:n_out]) if split_out else res[0]
    return y, (res[n_out] if with_norm else None)


def _s5_param_kernel(*refs):
    lax.fori_loop(0, _SLAB_GROUPS, functools.partial(_s5_param_group, refs), 0)


def _s5_param_group(refs, slot, carry):
    (ldt_ref, arc_ref, aic_ref, arr_ref, air_ref, btr_ref, bti_ref, ctr_ref, cti_ref,
     m_ref, sbr_ref, sbi_ref, car_ref, cai_ref, a16r_ref, a16i_ref) = refs
    t, c = S5_CHUNK, SSM_GROUP
    dt = jnp.exp(ldt_ref[slot])

    def cpow(ar, ai, n):
        mag = jnp.exp(ar * dt)
        pr, pi = mag * jnp.cos(ai * dt), mag * jnp.sin(ai * dt)
        n = jnp.asarray(n, jnp.int32)
        shape = jnp.broadcast_shapes(ar.shape, n.shape)
        re, im = jnp.ones(shape, F32), jnp.zeros(shape, F32)
        for b in range(t.bit_length()):
            bit = ((n >> b) & 1) == 1
            re, im = jnp.where(bit, re * pr - im * pi, re), jnp.where(bit, re * pi + im * pr, im)
            pr, pi = pr * pr - pi * pi, 2.0 * pr * pi
        return re, im

    def step_of(pos):
        seg = pos // c
        return (seg // _SLAB_GROUPS) * _SLAB_GROUPS + ((seg - slot) & (_SLAB_GROUPS - 1))

    arc, aic = arc_ref[slot], aic_ref[slot]
    arr, air = arr_ref[slot], air_ref[slot]
    p = arc.shape[0]
    lane_pc = lax.broadcasted_iota(jnp.int32, (p, t * c), 1)
    cr, ci = ctr_ref[slot], cti_ref[slot]
    pr, pi = cpow(arc, aic, step_of(lane_pc) + 1)
    car_ref[slot] = (cr * pr - ci * pi).astype(BF16)
    cai_ref[slot] = (-(cr * pi + ci * pr)).astype(BF16)
    pr, pi = cpow(arc, aic, lane_pc // c)
    csr = cr * pr - ci * pi
    csi = cr * pi + ci * pr

    abr, abi = cpow(arr, air, 1)
    xr, xi = abr - 1.0, abi
    den = arr * arr + air * air
    qr = (xr * arr + xi * air) / den
    qi = (xi * arr - xr * air) / den
    btr, bti = btr_ref[slot], bti_ref[slot]
    bbr = qr * btr - qi * bti
    bbi = qr * bti + qi * btr
    iidx = step_of(lax.broadcasted_iota(jnp.int32, (t * c, p), 0))
    pr, pi = cpow(arr, air, (t - 1) - iidx)
    sbr_ref[slot] = (bbr * pr - bbi * pi).astype(BF16)
    sbi_ref[slot] = (bbr * pi + bbi * pr).astype(BF16)
    pr, pi = cpow(arr, air, t)
    a16r_ref[slot] = pr
    a16i_ref[slot] = pi

    hp = lax.Precision.HIGHEST
    kt = (jnp.dot(bbr[:c], csr, precision=hp, preferred_element_type=F32)
          - jnp.dot(bbi[:c], csi, precision=hp, preferred_element_type=F32))
    lane = lax.broadcasted_iota(jnp.int32, (c, t * c), 1)
    for i in range(t):
        blk = kt if i == 0 else pltpu.roll(kt, i * c, 1)
        blk = jnp.where(lane >= i * c, blk, 0.0)
        blk = jnp.concatenate([pltpu.roll(blk[:, h * LANES:(h + 1) * LANES], slot * c, 1)
                               for h in range(t * c // LANES)], axis=1)
        seg = (i // _SLAB_GROUPS) * _SLAB_GROUPS + ((i + slot) & (_SLAB_GROUPS - 1))
        m_ref[slot, pl.ds(pl.multiple_of(seg * c, c), c), :] = blk.astype(BF16)
    return carry


def _s5_params(log_dt, a_re, a_im, b_re, b_im, c_re, c_im):
    g, p = a_re.shape
    t, c = S5_CHUNK, SSM_GROUP
    tc = t * c

    def tile_b(b):
        return jnp.broadcast_to(jnp.swapaxes(b, 1, 2)[:, None], (g, t, c, p)).reshape(g, tc, p)

    def tile_c(cm):
        return jnp.broadcast_to(jnp.swapaxes(cm, 1, 2)[:, :, None], (g, p, t, c)).reshape(g, p, tc)

    def spec(*shape):
        return pl.BlockSpec((_SLAB_GROUPS,) + shape, lambda i: (i, 0, 0))

    assert g % _SLAB_GROUPS == 0
    return pl.pallas_call(
        _s5_param_kernel, grid=(g // _SLAB_GROUPS,),
        in_specs=[spec(1, 1), spec(p, 1), spec(p, 1), spec(1, p), spec(1, p),
                  spec(tc, p), spec(tc, p), spec(p, tc), spec(p, tc)],
        out_specs=[spec(tc, tc), spec(tc, p), spec(tc, p), spec(p, tc), spec(p, tc), spec(1, p), spec(1, p)],
        out_shape=[jax.ShapeDtypeStruct((g, tc, tc), BF16),
                   jax.ShapeDtypeStruct((g, tc, p), BF16), jax.ShapeDtypeStruct((g, tc, p), BF16),
                   jax.ShapeDtypeStruct((g, p, tc), BF16), jax.ShapeDtypeStruct((g, p, tc), BF16),
                   jax.ShapeDtypeStruct((g, 1, p), F32), jax.ShapeDtypeStruct((g, 1, p), F32)],
        compiler_params=_cparams(("parallel",)), name="s5_params")(
            log_dt.reshape(g, 1, 1), a_re.reshape(g, p, 1), a_im.reshape(g, p, 1),
            a_re.reshape(g, 1, p), a_im.reshape(g, 1, p),
            tile_b(b_re), tile_b(b_im), tile_c(c_re), tile_c(c_im))


def _gelu_tanh(y):
    return 0.5 * y * (1.0 + jnp.tanh(math.sqrt(2.0 / math.pi) * (y + 0.044715 * (y * y * y))))


def _pick_segments(v, first):
    seg = lax.broadcasted_iota(jnp.int32, v[0].shape, 1) // SSM_GROUP
    out = v[(7 + first) % 8]
    for s in range(6, -1, -1):
        out = jnp.where(seg == s, v[(s + first) % 8], out)
    return out


def _s5_main_kernel(u_ref, m_ref, sbr_ref, sbi_ref, car_ref, cai_ref, a16r_ref, a16i_ref,
                    h0r_ref, h0i_ref, d_ref, z_ref, hpr_ref, hpi_ref, hsr_ref, hsi_ref,
                    u2_scr, sr_scr, si_scr, hr_scr, hi_scr, *, bp, nkp, bs, nks):
    t = S5_CHUNK
    npq = bp * nkp // 8
    nq = npq + bs * nks // 8
    sp = nkp * t

    def token0(q):
        return jnp.where(q < npq, (q % bp) * sp + (q // bp) * (8 * t), bp * sp + (q - npq) * (8 * t))

    def gather(q, carry):
        tok, rows = token0(q), pl.ds(pl.multiple_of(q * 8, 8), 8)
        for h in range(2):
            v = [u_ref[pl.ds(tok + 8 * h + i, 8, stride=t), :] for i in range(8)]
            v = [x if i == 0 else pltpu.roll(x, i * SSM_GROUP, 1) for i, x in enumerate(v)]
            for gi in range(8):
                u2_scr[gi, rows, h * LANES:(h + 1) * LANES] = _pick_segments(v, 8 - gi)
        return carry

    lax.fori_loop(0, nq, gather, 0, unroll=4)

    def group(gi, carry):
        u = u2_scr[gi]
        ub = u.astype(BF16)
        sr_scr[...] = _dot(ub, sbr_ref[gi])
        si_scr[...] = _dot(ub, sbi_ref[gi])
        ar, ai = a16r_ref[gi], a16i_ref[gi]

        def step(rows, hr, hi):
            hr_scr[rows, :] = hr
            hi_scr[rows, :] = hi
            return (ar * hr - ai * hi + sr_scr[rows, :], ar * hi + ai * hr + si_scr[rows, :])

        hr = hi = jnp.zeros((bp, ar.shape[1]), F32)
        for k in range(nkp):
            hr, hi = step(pl.ds((k // 8) * bp * 8 + k % 8, bp, stride=8), hr, hi)
        hpr_ref[gi] = hr
        hpi_ref[gi] = hi
        hr, hi = h0r_ref[gi], h0i_ref[gi]
        for k in range(nks):
            hr, hi = step(pl.ds(bp * nkp + k, bs, stride=nks), hr, hi)
        hsr_ref[gi] = hr
        hsi_ref[gi] = hi

        y = (_dot(ub, m_ref[gi]) + _dot(hr_scr[...].astype(BF16), car_ref[gi])
             + _dot(hi_scr[...].astype(BF16), cai_ref[gi]) + d_ref[gi] * u)
        u2_scr[gi] = _gelu_tanh(y)
        return carry

    lax.fori_loop(0, 8, group, 0)

    def scatter(q, carry):
        tok, rows = token0(q), pl.ds(pl.multiple_of(q * 8, 8), 8)
        for h in range(2):
            v = [u2_scr[gi, rows, h * LANES:(h + 1) * LANES] for gi in range(8)]
            for i in range(8):
                w = _pick_segments(v, 8 - i)
                w = w if i == 0 else pltpu.roll(w, LANES - i * SSM_GROUP, 1)
                z_ref[pl.ds(tok + 8 * h + i, 8, stride=t), :] = w
        return carry

    lax.fori_loop(0, nq, scatter, 0, unroll=4)


def _s5_layer(u, n_prompt, bp, bs, h0_re, h0_im, mats, d_skip):
    n, d = u.shape
    t, c = S5_CHUNK, SSM_GROUP
    g, tc = d // c, t * c
    gs = LANES // c
    p = h0_re.shape[-1]
    nkp = n_prompt // (bp * t)
    nks = (n - n_prompt) // (bs * t)
    r = nkp * bp + nks * bs
    assert gs == 8 and t == 16 and bp % 8 == 0 and nkp % 8 == 0 and (bs * nks) % 8 == 0

    m, sbr, sbi, car, cai, a16r, a16i = mats
    d_t = jnp.broadcast_to(d_skip.reshape(g, 1, 1, c), (g, 1, t, c)).reshape(g, 1, tc)
    h0r = jnp.swapaxes(h0_re, 0, 1)
    h0i = jnp.swapaxes(h0_im, 0, 1)

    def spec(*shape):
        return pl.BlockSpec((gs,) + shape, lambda i: (i, 0, 0))

    kern = functools.partial(_s5_main_kernel, bp=bp, nkp=nkp, bs=bs, nks=nks)
    z, hpr, hpi, hsr, hsi = pl.pallas_call(
        kern, grid=(g // gs,),
        in_specs=[pl.BlockSpec((n, LANES), lambda i: (0, i), pipeline_mode=pl.Buffered(1)),
                  spec(tc, tc), spec(tc, p), spec(tc, p), spec(p, tc), spec(p, tc),
                  spec(1, p), spec(1, p), spec(bs, p), spec(bs, p), spec(1, tc)],
        out_specs=[pl.BlockSpec((n, LANES), lambda i: (0, i)),
                   spec(bp, p), spec(bp, p), spec(bs, p), spec(bs, p)],
        out_shape=[jax.ShapeDtypeStruct((n, d), F32),
                   jax.ShapeDtypeStruct((g, bp, p), F32), jax.ShapeDtypeStruct((g, bp, p), F32),
                   jax.ShapeDtypeStruct((g, bs, p), F32), jax.ShapeDtypeStruct((g, bs, p), F32)],
        scratch_shapes=[pltpu.VMEM((gs, r, tc), F32)] + [pltpu.VMEM((r, p), F32) for _ in range(4)],
        compiler_params=_cparams(("arbitrary",)), name="s5_main")(
            u, m, sbr, sbi, car, cai, a16r, a16i, h0r, h0i, d_t)
    states = tuple(jnp.swapaxes(h, 0, 1)[:, None] for h in (hpr, hpi, hsr, hsi))
    return z, states


def _glu_kernel(x_ref, z_ref, wa_ref, wb_ref, o_ref):
    z = z_ref[...].astype(BF16)
    o_ref[...] = x_ref[...] + _dot(z, wa_ref[...]) * jax.nn.sigmoid(_dot(z, wb_ref[...]))


def _glu(x, z, wa, wb):
    n, d = x.shape
    tm = min(TOK_TM, n)
    row = pl.BlockSpec((tm, d), lambda i: (i, 0))
    mat = pl.BlockSpec((d, d), lambda i: (0, 0))
    return pl.pallas_call(
        _glu_kernel, grid=(n // tm,), in_specs=[row, row, mat, mat], out_specs=row,
        out_shape=jax.ShapeDtypeStruct((n, d), F32),
        compiler_params=_cparams(("parallel",)), name="glu")(x, z, wa, wb)


def _head_norm(x, e_ref, et_ref):
    hi, lo = _split2(x * x)
    ms = (_dot(hi, e_ref[...]) + _dot(lo, e_ref[...])) * (1.0 / HEAD_DIM)
    hi, lo = _split2(lax.rsqrt(ms + EPS))
    return _dot(hi, et_ref[...]) + _dot(lo, et_ref[...])


def _head_indicator(d):
    nh = d // HEAD_DIM
    e = (np.arange(d)[:, None] // HEAD_DIM == np.arange(LANES)[None, :]).astype(np.float32)
    assert nh <= LANES
    return jnp.asarray(e, BF16), jnp.asarray(e.T, BF16)


def _kvf_kernel(h_ref, wkv_ref, wf_ref, bf_ref, kn_ref, e_ref, et_ref,
                kp_ref, ks_ref, vp_ref, vs_ref, lp_ref, ls_ref, kb_ref, vb_ref, *, d, nh, npb):
    i = pl.program_id(0)
    h = h_ref[...]
    tm = h.shape[0]
    pkv = _dot(h, wkv_ref[...])
    kraw, v = pkv[:, :d], pkv[:, d:]
    k = kraw * _head_norm(kraw, e_ref, et_ref) * kn_ref[...]
    kb_ref[...] = k.astype(BF16)
    vb_ref[...] = v.astype(BF16)
    pf = _dot(h, wf_ref[...]) + bf_ref[...]
    lft = (jnp.minimum(pf, 0.0) - jnp.log1p(jnp.exp(-jnp.abs(pf)))).T[:nh]

    @pl.when(i < npb)
    def _():
        kp_ref[0] = k.T
        vp_ref[0] = v.T
        lp_ref[0] = lft

    @pl.when(i >= npb)
    def _():
        for hd in range(nh):
            rows = pl.ds(hd, tm, stride=nh)
            ks_ref[rows, :] = k[:, hd * HEAD_DIM:(hd + 1) * HEAD_DIM]
            vs_ref[rows, :] = v[:, hd * HEAD_DIM:(hd + 1) * HEAD_DIM]
        lq = ls_ref.shape[2]
        for j in range(tm // lq):
            ls_ref[j] = lft[:, j * lq:(j + 1) * lq]


def _kvf(hn, bp, s, lq, w_kv, w_f, b_f, k_norm):
    n, d = hn.shape
    nh = d // HEAD_DIM
    tm = min(TOK_TM, s)
    n_prompt = bp * s
    n_s = n - n_prompt
    assert s % tm == 0 and n_s % tm == 0 and tm % lq == 0
    npb, spb = n_prompt // tm, s // tm
    e, et = _head_indicator(d)
    row = pl.BlockSpec((tm, d), lambda i: (i, 0))
    const = lambda *sh: pl.BlockSpec(sh, lambda i: (0, 0))
    prompt_t = lambda rows: pl.BlockSpec(
        (1, rows, tm), lambda i: (jnp.minimum(i, npb - 1) // spb, 0, jnp.minimum(i, npb - 1) % spb))
    second = lambda i: (jnp.maximum(i - npb, 0), 0)
    kern = functools.partial(_kvf_kernel, d=d, nh=nh, npb=npb)
    return pl.pallas_call(
        kern, grid=(n // tm,),
        in_specs=[row, const(d, 2 * d), const(d, LANES), const(1, LANES), const(1, d), const(d, LANES), const(LANES, d)],
        out_specs=[prompt_t(d), pl.BlockSpec((tm * nh, HEAD_DIM), second),
                   prompt_t(d), pl.BlockSpec((tm * nh, HEAD_DIM), second),
                   prompt_t(nh), pl.BlockSpec((tm // lq, nh, lq), lambda i: (jnp.maximum(i - npb, 0), 0, 0)), row, row],
        out_shape=[jax.ShapeDtypeStruct((bp, d, s), F32), jax.ShapeDtypeStruct((n_s * nh, HEAD_DIM), F32),
                   jax.ShapeDtypeStruct((bp, d, s), F32), jax.ShapeDtypeStruct((n_s * nh, HEAD_DIM), F32),
                   jax.ShapeDtypeStruct((bp, nh, s), F32), jax.ShapeDtypeStruct((n_s // lq, nh, lq), F32),
                   jax.ShapeDtypeStruct((n, d), BF16), jax.ShapeDtypeStruct((n, d), BF16)],
        compiler_params=_cparams(("arbitrary",)), name="kvf")(
            hn, w_kv, jnp.pad(w_f, ((0, 0), (0, LANES - nh))), jnp.pad(b_f, (0, LANES - nh)).reshape(1, LANES),
            jnp.tile(k_norm, nh).reshape(1, d), e, et)


def _qproj_kernel(u_ref, wq_ref, qn_ref, e_ref, et_ref, q_ref):
    q = _dot(u_ref[...], wq_ref[...])
    q_ref[...] = (q * _head_norm(q, e_ref, et_ref) * qn_ref[...]).astype(BF16)


def _qproj(un, wq, qn_scaled):
    n, d = un.shape
    tm = min(TOK_TM, n)
    e, et = _head_indicator(d)
    row = pl.BlockSpec((tm, d), lambda i: (i, 0))
    const = lambda *s: pl.BlockSpec(s, lambda i: (0, 0))
    return pl.pallas_call(
        _qproj_kernel, grid=(n // tm,),
        in_specs=[row, const(d, d), const(1, d), const(d, LANES), const(LANES, d)], out_specs=row,
        out_shape=jax.ShapeDtypeStruct((n, d), BF16),
        compiler_params=_cparams(("parallel",)), name="qproj")(un, wq, qn_scaled, e, et)


def _oproj_kernel(x_ref, ap_ref, as_ref, wo_ref, o_ref, *, npb):
    a = jnp.where(pl.program_id(0) < npb, ap_ref[...], as_ref[...])
    o_ref[...] = x_ref[...] + _dot(a, wo_ref[...])


def _oproj(x, a_p, a_s, wo):
    n, d = x.shape
    tm = min(TOK_TM, n)
    assert a_p.shape[0] % tm == 0 and a_s.shape[0] % tm == 0
    npb = a_p.shape[0] // tm
    first, second = _part_maps(npb)
    row = pl.BlockSpec((tm, d), lambda i: (i, 0))
    return pl.pallas_call(
        functools.partial(_oproj_kernel, npb=npb), grid=(n // tm,),
        in_specs=[row, pl.BlockSpec((tm, d), first), pl.BlockSpec((tm, d), second),
                  pl.BlockSpec((d, d), lambda i: (0, 0))],
        out_specs=row, out_shape=jax.ShapeDtypeStruct((n, d), F32),
        compiler_params=_cparams(("parallel",)), name="oproj")(x, a_p, a_s, wo)


def _cumsum_kernel(x_ref, init_ref, o_ref, *, nblk, bw):
    tri = (lax.broadcasted_iota(jnp.int32, (bw, bw), 0) <= lax.broadcasted_iota(jnp.int32, (bw, bw), 1)).astype(BF16)
    carry = init_ref[...]
    for j in range(nblk):
        x = x_ref[:, j * bw:(j + 1) * bw]
        hi, mid, lo = _split3(x)
        cs = _dot(hi, tri) + _dot(mid, tri) + _dot(lo, tri) + carry
        o_ref[:, j * bw:(j + 1) * bw] = cs
        carry = cs[:, bw - 1:bw]


def _cumsum_rows(x, init):
    r, w = x.shape
    bw = min(256, w)
    kern = functools.partial(_cumsum_kernel, nblk=w // bw, bw=bw)
    return pl.pallas_call(
        kern, grid=(1,),
        in_specs=[pl.BlockSpec((r, w), lambda i: (0, 0)), pl.BlockSpec((r, 1), lambda i: (0, 0))],
        out_specs=pl.BlockSpec((r, w), lambda i: (0, 0)), out_shape=jax.ShapeDtypeStruct((r, w), F32),
        compiler_params=_cparams(("arbitrary",)), name="cumsum")(x, init)


def _lanes(x, w):
    if w <= LANES:
        return x[:, :w]
    return jnp.concatenate([x] * (w // LANES), axis=1)


def _head_masks(shape):
    head = lax.broadcasted_iota(jnp.int32, shape, 1) // HEAD_DIM
    return [head == i for i in range(shape[1] // HEAD_DIM)]


def _stack_heads(qs):
    zero = jnp.zeros_like(qs)
    return jnp.concatenate([jnp.where(mk, qs, zero) for mk in _head_masks(qs.shape)], axis=0)


def _flash_init(cq, hs, m_scr, acc_scr, cq_scr):
    tq = cq.shape[0]
    m_scr[...] = jnp.full_like(m_scr, NEG_BIG)
    acc_scr[...] = jnp.zeros_like(acc_scr)
    for g in range(cq_scr.shape[0]):
        for i in range(hs):
            h = g * hs + i
            cq_scr[g, i * tq:(i + 1) * tq, :] = jnp.broadcast_to(cq[:, h:h + 1], (tq, LANES))


def _flash_slab(g, q_st, ks, vs, ck_rows, causal, m_scr, acc_scr, cq_scr, kv_t=False):
    hs = len(ck_rows)
    tq, tk = q_st.shape[0] // hs, ck_rows[0].shape[1]
    nt = (((1,), (1,)), ((), ()))
    s = _dot(q_st, ks) if kv_t else lax.dot_general(q_st, ks, nt, preferred_element_type=F32)
    t = jnp.concatenate([s[i * tq:(i + 1) * tq] - ck_rows[i] for i in range(hs)], axis=0)
    if causal is not None:
        keep = lax.broadcasted_iota(jnp.int32, (tq, tk), 1) <= lax.broadcasted_iota(jnp.int32, (tq, tk), 0) + causal
        t = jnp.where(jnp.concatenate([keep] * hs, axis=0), t, NEG_BIG)
    cq = cq_scr[g]
    m_prev = m_scr[g]
    m_new = jnp.maximum(m_prev, jnp.max(t, axis=1, keepdims=True) + cq)
    alpha = jnp.exp(m_prev - m_new)
    pe = jnp.exp(t - _lanes(m_new - cq, tk))
    w = hs * HEAD_DIM
    if w == LANES and not kv_t:
        pv = _dot(pe.astype(BF16), jnp.concatenate([vs, jnp.ones((tk, LANES), BF16)], axis=1))
    else:
        pv = lax.dot_general(pe.astype(BF16), vs, nt, preferred_element_type=F32) if kv_t else _dot(pe.astype(BF16), vs)
        pv = jnp.concatenate([pv, jnp.broadcast_to(jnp.sum(pe, axis=1, keepdims=True), (hs * tq, LANES))], axis=1)
    acc_scr[g] = _lanes(alpha, w + LANES) * acc_scr[g] + pv
    m_scr[g] = m_new


def _flash_finish(o_ref, hs, acc_scr):
    tq = o_ref.shape[0]
    w = hs * HEAD_DIM
    masks = _head_masks((tq, w))
    for g in range(acc_scr.shape[0]):
        acc = acc_scr[g]
        o = acc[:, :w] / _lanes(acc[:, w:], w)
        out = o[(hs - 1) * tq:]
        for i in range(hs - 2, -1, -1):
            out = jnp.where(masks[i], o[i * tq:(i + 1) * tq], out)
        o_ref[:, g * w:(g + 1) * w] = out.astype(o_ref.dtype)


def _flash_scratch(ng, rows, w):
    return [pltpu.VMEM((ng, rows, LANES), F32), pltpu.VMEM((ng, rows, w + LANES), F32),
            pltpu.VMEM((ng, rows, LANES), F32)]


def _attn_prompt_kernel(qi_tab, ki_tab, q_ref, k_ref, v_ref, cq_ref, ck_ref, o_ref, *scr, hs):
    step = pl.program_id(1)
    qi, ki = qi_tab[step], ki_tab[step]
    tq, tk = q_ref.shape[0], k_ref.shape[0]
    w = hs * HEAD_DIM
    ng = q_ref.shape[1] // w
    last = (qi * tq + tq - 1) // tk

    @pl.when(ki == 0)
    def _():
        _flash_init(cq_ref[...], hs, *scr)

    def sweep(causal):
        ck = ck_ref[0]
        for g in range(ng):
            sl = slice(g * w, (g + 1) * w)
            _flash_slab(g, _stack_heads(q_ref[:, sl]), k_ref[:, sl], v_ref[:, sl],
                        [ck[g * hs + i:g * hs + i + 1, :] for i in range(hs)], causal, *scr)

    @pl.when(ki < last)
    def _():
        sweep(None)

    @pl.when(ki == last)
    def _():
        sweep(qi * tq - ki * tk)
        _flash_finish(o_ref, hs, scr[1])


def _attn_prompt(q, kb, vb, c_rows, c_t, nb, s):
    d = q.shape[1]
    nh = d // HEAD_DIM
    hs = ATT_HS
    tq, tk = min(ATT_TQ, s), min(ATT_TK, s)
    assert tk % tq == 0 and s % tk == 0
    nq, nk = s // tq, s // tk
    pairs = [(i, j) for i in range(nq) for j in range((i * tq + tq - 1) // tk + 1)]
    qi_tab = jnp.asarray([p[0] for p in pairs], jnp.int32)
    ki_tab = jnp.asarray([p[1] for p in pairs], jnp.int32)
    grid_spec = pltpu.PrefetchScalarGridSpec(
        num_scalar_prefetch=2, grid=(nb, len(pairs)),
        in_specs=[pl.BlockSpec((tq, d), lambda b, p, qt, kt: (b * nq + qt[p], 0)),
                  pl.BlockSpec((tk, d), lambda b, p, qt, kt: (b * nk + kt[p], 0)),
                  pl.BlockSpec((tk, d), lambda b, p, qt, kt: (b * nk + kt[p], 0)),
                  pl.BlockSpec((tq, nh), lambda b, p, qt, kt: (b * nq + qt[p], 0)),
                  pl.BlockSpec((1, nh, tk), lambda b, p, qt, kt: (b, 0, kt[p]))],
        out_specs=pl.BlockSpec((tq, d), lambda b, p, qt, kt: (b * nq + qt[p], 0)),
        scratch_shapes=_flash_scratch(nh // hs, hs * tq, hs * HEAD_DIM))
    return pl.pallas_call(
        functools.partial(_attn_prompt_kernel, hs=hs), grid_spec=grid_spec,
        out_shape=jax.ShapeDtypeStruct((nb * s, d), BF16),
        compiler_params=_cparams(("parallel", "arbitrary")), name="attn_prompt")(
            qi_tab, ki_tab, q, kb, vb, c_rows, c_t)


def _attn_sample_kernel(q_ref, ck_ref, cv_ref, kn_ref, vn_ref, cq_ref, ckc_ref, ckn_ref, o_ref, *scr, hs, nkb):
    j = pl.program_id(1)
    w = hs * HEAD_DIM
    ng = q_ref.shape[1] // w

    @pl.when(j == 0)
    def _():
        _flash_init(cq_ref[0], hs, *scr)

    def sweep(kslab, vslab, ck, causal, kv_t):
        for g in range(ng):
            _flash_slab(g, _stack_heads(q_ref[:, g * w:(g + 1) * w]), kslab(g), vslab(g),
                        [ck[g * hs + i:g * hs + i + 1, :] for i in range(hs)], causal, *scr, kv_t=kv_t)

    @pl.when(j < nkb)
    def _():
        sweep(lambda g: ck_ref[0, g * w:(g + 1) * w, :].astype(BF16),
              lambda g: cv_ref[0, g * w:(g + 1) * w, :].astype(BF16), ckc_ref[0], None, True)

    @pl.when(j == nkb)
    def _():
        sweep(lambda g: kn_ref[:, g * w:(g + 1) * w], lambda g: vn_ref[:, g * w:(g + 1) * w], ckn_ref[0], 0, False)
        _flash_finish(o_ref, hs, scr[1])


def _attn_sample(q, kb, vb, cache_kt, cache_vt, c_new, c_past_t, c_new_t, n_prompt):
    d = q.shape[1]
    nh = d // HEAD_DIM
    nb, _, plen = cache_kt.shape
    lq = c_new.shape[1]
    hs = DEC_HS
    tk = min(DEC_TK, plen)
    nkb = plen // tk
    r0 = n_prompt // lq
    new_rows = pl.BlockSpec((lq, d), lambda b, j: (r0 + b, 0))
    cache = pl.BlockSpec((1, d, tk), lambda b, j: (b, 0, jnp.minimum(j, nkb - 1)))
    kern = functools.partial(_attn_sample_kernel, hs=hs, nkb=nkb)
    return pl.pallas_call(
        kern, grid=(nb, nkb + 1),
        in_specs=[new_rows, cache, cache, new_rows, new_rows,
                  pl.BlockSpec((1, lq, nh), lambda b, j: (b, 0, 0)),
                  pl.BlockSpec((1, nh, tk), lambda b, j: (b, 0, jnp.minimum(j, nkb - 1))),
                  pl.BlockSpec((1, nh, lq), lambda b, j: (b, 0, 0))],
        out_specs=pl.BlockSpec((lq, d), lambda b, j: (b, 0)),
        out_shape=jax.ShapeDtypeStruct((nb * lq, d), BF16),
        scratch_shapes=_flash_scratch(nh // hs, hs * lq, hs * HEAD_DIM),
        compiler_params=_cparams(("parallel", "arbitrary")), name="attn_sample")(
            q, cache_kt, cache_vt, kb, vb, c_new, c_past_t, c_new_t)


def kernel(x_prompt, x_sample, cache_k, cache_v, cache_logf, state_ssm_re, state_ssm_im, ffn_norm, w_ffn_gate, w_ffn_up, w_ffn_down, mix_norm, ssm_a_re, ssm_a_im, ssm_log_dt, ssm_b_re, ssm_b_im, ssm_c_re, ssm_c_im, ssm_d, w_glu_a, w_glu_b, kv_norm, w_kvf, b_f, k_norm, w_q, q_norm, w_o):
    bp, s, d = x_prompt.shape
    bs, lq, _ = x_sample.shape
    plen = cache_k.shape[1]
    nh = d // HEAD_DIM
    assert ffn_norm.shape[0] == 2 and ssm_a_re.shape[0] == 1 and w_q.shape[0] == 1
    n_prompt, n_sample = bp * s, bs * lq
    bf = lambda w: w.astype(BF16)

    def ffn(xs, l, j, **kw):
        return _ffn(xs, ffn_norm[l, j], w_ffn_gate, w_ffn_up, w_ffn_down, (l, j), **kw)

    x, u = ffn([x_prompt.reshape(n_prompt, d), x_sample.reshape(n_sample, d)], 0, 0, g2=mix_norm[0], norm_dtype=F32)
    mats = _s5_params(ssm_log_dt[0], ssm_a_re[0], ssm_a_im[0], ssm_b_re[0], ssm_b_im[0], ssm_c_re[0], ssm_c_im[0])
    z, (p_re, p_im, s_re, s_im) = _s5_layer(u, n_prompt, bp, bs, state_ssm_re[:, 0], state_ssm_im[:, 0], mats, ssm_d[0])
    x = _glu(x, z, bf(w_glu_a[0]), bf(w_glu_b[0]))
    x, hn = ffn([x], 0, 1, g2=kv_norm)

    kt_p, k_s, vt_p, v_s, lft_p, lft_s, kb, vb = _kvf(hn, bp, s, lq, bf(w_kvf[:, :2 * d]), bf(w_kvf[:, 2 * d:]), b_f, k_norm)

    def cumsum_t(lft, init):
        b, _, l = lft.shape
        init = jnp.zeros((b * nh, 1), F32) if init is None else init.reshape(b * nh, 1)
        return _cumsum_rows(lft.reshape(b * nh, l), init).reshape(b, nh, l)

    cp_t = cumsum_t(lft_p, None)
    cpast_t = cumsum_t(jnp.swapaxes(cache_logf.astype(F32), 1, 2), None)
    cnew_t = cumsum_t(lft_s, cpast_t[:, :, -1])

    x, un = ffn([x], 1, 0, g2=mix_norm[1])
    q = _qproj(un, bf(w_q[0]), (jnp.tile(q_norm[0], nh) * (1.0 / math.sqrt(HEAD_DIM))).reshape(1, d))
    a_p = _attn_prompt(q, kb, vb, jnp.swapaxes(cp_t, 1, 2).reshape(n_prompt, nh), cp_t, bp, s)
    pos_minor = lambda c: jnp.transpose(c, (0, 2, 3, 1)).reshape(bs, d, plen)
    a_s = _attn_sample(q, kb, vb, pos_minor(cache_k), pos_minor(cache_v),
                       jnp.swapaxes(cnew_t, 1, 2), cpast_t, cnew_t, n_prompt)
    x = _oproj(x, a_p, a_s, bf(w_o[0]))
    (y_p, y_s), _ = ffn([x], 1, 1, split_out=(n_prompt, n_sample))

    cache_p = lambda a: jnp.transpose(a.reshape(bp, nh, HEAD_DIM, s), (0, 3, 1, 2))
    cache_s = lambda a: a.reshape(bs, lq, nh, HEAD_DIM)
    return (y_p.reshape(bp, s, d), y_s.reshape(bs, lq, d), p_re, p_im, cache_p(kt_p), cache_p(vt_p),
            jnp.swapaxes(lft_p, 1, 2), s_re, s_im, cache_s(k_s), cache_s(v_s), jnp.swapaxes(lft_s, 1, 2))
```

```python
import functools
import math

import jax
import jax.numpy as jnp
import numpy as np
from jax import lax
from jax.experimental import pallas as pl
from jax.experimental.pallas import tpu as pltpu

F32 = jnp.float32
BF16 = jnp.bfloat16

EPS = 1e-6
HEAD_DIM = 64
SSM_GROUP = 16
S5_CHUNK = 16
LANES = 128
_SLAB_GROUPS = LANES // SSM_GROUP
NEG_BIG = -1e30
VMEM_LIMIT = 57 * 1024 * 1024

FFN_TM = 1024
FFN_TF = 512
TOK_TM = 512
ATT_TQ = 256
ATT_TK = 512
ATT_HS = 2
DEC_TK = 1024
DEC_HS = 4


def _cparams(sem):
    return pltpu.CompilerParams(dimension_semantics=sem, vmem_limit_bytes=VMEM_LIMIT)


def _rms_scale(x):
    return lax.rsqrt(jnp.mean(x * x, axis=-1, keepdims=True) + EPS)


def _dot(a, b):
    return jnp.dot(a, b, preferred_element_type=F32)


def _split2(x):
    hi = x.astype(BF16)
    lo = (x - hi.astype(F32)).astype(BF16)
    return hi, lo


def _split3(x):
    hi = x.astype(BF16)
    r1 = x - hi.astype(F32)
    mid = r1.astype(BF16)
    return hi, mid, (r1 - mid.astype(F32)).astype(BF16)


def _part_maps(npb):
    first = lambda i, *_: (jnp.minimum(i, npb - 1), 0)
    second = lambda i, *_: (jnp.maximum(i - npb, 0), 0)
    return first, second


def _ffn_kernel(*refs, nf, npb, n_in, n_out, with_norm):
    x_refs, refs = refs[:n_in], refs[n_in:]
    g_ref, wg_ref, wu_ref, wd_ref = refs[:4]
    refs = refs[4:]
    g2_ref = refs[0] if with_norm else None
    refs = refs[with_norm:]
    o_refs, refs = refs[:n_out], refs[n_out:]
    o2_ref = refs[0] if with_norm else None
    h_scr, acc_scr = refs[with_norm:]
    i, f = pl.program_id(0), pl.program_id(1)

    def load_x():
        if n_in == 1:
            return x_refs[0][...]
        return jnp.where(i < npb, x_refs[0][...], x_refs[1][...])

    @pl.when(f == 0)
    def _():
        x = load_x()
        h_scr[...] = (x * _rms_scale(x) * g_ref[...]).astype(BF16)
        acc_scr[...] = jnp.zeros_like(acc_scr)

    h = h_scr[...]
    a = _dot(h, wg_ref[...].astype(BF16))
    b = _dot(h, wu_ref[...].astype(BF16))
    t = (a * jax.nn.sigmoid(a)) * b
    acc_scr[...] += _dot(t.astype(BF16), wd_ref[...].astype(BF16))

    @pl.when(f == nf - 1)
    def _():
        y = load_x() + 0.5 * acc_scr[...]
        if n_out == 1:
            o_refs[0][...] = y
        else:
            @pl.when(i < npb)
            def _():
                o_refs[0][...] = y

            @pl.when(i >= npb)
            def _():
                o_refs[1][...] = y
        if with_norm:
            o2_ref[...] = (y * _rms_scale(y) * g2_ref[...]).astype(o2_ref.dtype)


def _ffn(xs, g, wg, wu, wd, lj, g2=None, split_out=None, norm_dtype=BF16):
    d = xs[0].shape[1]
    n = sum(x.shape[0] for x in xs)
    dff = wg.shape[-1]
    l, j = lj
    tm, tf = min(FFN_TM, n), min(FFN_TF, dff)
    nf = dff // tf
    n_first = xs[0].shape[0] if len(xs) == 2 else (split_out[0] if split_out else n)
    assert all(x.shape[0] % tm == 0 for x in xs) and n_first % tm == 0
    npb = n_first // tm
    first, second = _part_maps(npb)
    with_norm = g2 is not None
    row = pl.BlockSpec((tm, d), lambda i, f: (i, 0))
    vec = pl.BlockSpec((1, d), lambda i, f: (0, 0))
    parts = [pl.BlockSpec((tm, d), first), pl.BlockSpec((tm, d), second)]
    in_specs = (parts if len(xs) == 2 else [row]) + [
        vec, pl.BlockSpec((None, None, d, tf), lambda i, f: (l, j, 0, f)),
        pl.BlockSpec((None, None, d, tf), lambda i, f: (l, j, 0, f)),
        pl.BlockSpec((None, None, tf, d), lambda i, f: (l, j, f, 0))]
    args = list(xs) + [g.reshape(1, d), wg, wu, wd]
    if split_out:
        out_shape = [jax.ShapeDtypeStruct((m, d), F32) for m in split_out]
        out_specs = list(parts)
    else:
        out_shape = [jax.ShapeDtypeStruct((n, d), F32)]
        out_specs = [row]
    n_out = len(out_shape)
    if with_norm:
        in_specs.append(vec)
        args.append(g2.reshape(1, d))
        out_shape.append(jax.ShapeDtypeStruct((n, d), norm_dtype))
        out_specs.append(row)
    kern = functools.partial(_ffn_kernel, nf=nf, npb=npb, n_in=len(xs), n_out=n_out, with_norm=with_norm)
    res = pl.pallas_call(
        kern, grid=(n // tm, nf), in_specs=in_specs, out_specs=out_specs, out_shape=out_shape,
        scratch_shapes=[pltpu.VMEM((tm, d), BF16), pltpu.VMEM((tm, d), F32)],
        compiler_params=_cparams(("arbitrary", "arbitrary")), name="ffn")(*args)
    y = tuple(res[:n_out]) if split_out else res[0]
    return y, (res[n_out] if with_norm else None)


def _s5_param_kernel(*refs):
    lax.fori_loop(0, _SLAB_GROUPS, functools.partial(_s5_param_group, refs), 0)


def _s5_param_group(refs, slot, carry):
    (ldt_ref, arc_ref, aic_ref, arr_ref, air_ref, btr_ref, bti_ref, ctr_ref, cti_ref,
     m_ref, sbr_ref, sbi_ref, car_ref, cai_ref, a16r_ref, a16i_ref) = refs
    t, c = S5_CHUNK, SSM_GROUP
    dt = jnp.exp(ldt_ref[slot])

    def cpow(ar, ai, n):
        mag = jnp.exp(ar * dt)
        pr, pi = mag * jnp.cos(ai * dt), mag * jnp.sin(ai * dt)
        n = jnp.asarray(n, jnp.int32)
        shape = jnp.broadcast_shapes(ar.shape, n.shape)
        re, im = jnp.ones(shape, F32), jnp.zeros(shape, F32)
        for b in range(t.bit_length()):
            bit = ((n >> b) & 1) == 1
            re, im = jnp.where(bit, re * pr - im * pi, re), jnp.where(bit, re * pi + im * pr, im)
            pr, pi = pr * pr - pi * pi, 2.0 * pr * pi
        return re, im

    def step_of(pos):
        seg = pos // c
        return (seg // _SLAB_GROUPS) * _SLAB_GROUPS + ((seg - slot) & (_SLAB_GROUPS - 1))

    arc, aic = arc_ref[slot], aic_ref[slot]
    arr, air = arr_ref[slot], air_ref[slot]
    p = arc.shape[0]
    lane_pc = lax.broadcasted_iota(jnp.int32, (p, t * c), 1)
    cr, ci = ctr_ref[slot], cti_ref[slot]
    pr, pi = cpow(arc, aic, step_of(lane_pc) + 1)
    car_ref[slot] = (cr * pr - ci * pi).astype(BF16)
    cai_ref[slot] = (-(cr * pi + ci * pr)).astype(BF16)
    pr, pi = cpow(arc, aic, lane_pc // c)
    csr = cr * pr - ci * pi
    csi = cr * pi + ci * pr

    abr, abi = cpow(arr, air, 1)
    xr, xi = abr - 1.0, abi
    den = arr * arr + air * air
    qr = (xr * arr + xi * air) / den
    qi = (xi * arr - xr * air) / den
    btr, bti = btr_ref[slot], bti_ref[slot]
    bbr = qr * btr - qi * bti
    bbi = qr * bti + qi * btr
    iidx = step_of(lax.broadcasted_iota(jnp.int32, (t * c, p), 0))
    pr, pi = cpow(arr, air, (t - 1) - iidx)
    sbr_ref[slot] = (bbr * pr - bbi * pi).astype(BF16)
    sbi_ref[slot] = (bbr * pi + bbi * pr).astype(BF16)
    pr, pi = cpow(arr, air, t)
    a16r_ref[slot] = pr
    a16i_ref[slot] = pi

    hp = lax.Precision.HIGHEST
    kt = (jnp.dot(bbr[:c], csr, precision=hp, preferred_element_type=F32)
          - jnp.dot(bbi[:c], csi, precision=hp, preferred_element_type=F32))
    lane = lax.broadcasted_iota(jnp.int32, (c, t * c), 1)
    for i in range(t):
        blk = kt if i == 0 else pltpu.roll(kt, i * c, 1)
        blk = jnp.where(lane >= i * c, blk, 0.0)
        blk = jnp.concatenate([pltpu.roll(blk[:, h * LANES:(h + 1) * LANES], slot * c, 1)
                               for h in range(t * c // LANES)], axis=1)
        seg = (i // _SLAB_GROUPS) * _SLAB_GROUPS + ((i + slot) & (_SLAB_GROUPS - 1))
        m_ref[slot, pl.ds(pl.multiple_of(seg * c, c), c), :] = blk.astype(BF16)
    return carry


def _s5_params(log_dt, a_re, a_im, b_re, b_im, c_re, c_im):
    g, p = a_re.shape
    t, c = S5_CHUNK, SSM_GROUP
    tc = t * c

    def tile_b(b):
        return jnp.broadcast_to(jnp.swapaxes(b, 1, 2)[:, None], (g, t, c, p)).reshape(g, tc, p)

    def tile_c(cm):
        return jnp.broadcast_to(jnp.swapaxes(cm, 1, 2)[:, :, None], (g, p, t, c)).reshape(g, p, tc)

    def spec(*shape):
        return pl.BlockSpec((_SLAB_GROUPS,) + shape, lambda i: (i, 0, 0))

    assert g % _SLAB_GROUPS == 0
    return pl.pallas_call(
        _s5_param_kernel, grid=(g // _SLAB_GROUPS,),
        in_specs=[spec(1, 1), spec(p, 1), spec(p, 1), spec(1, p), spec(1, p),
                  spec(tc, p), spec(tc, p), spec(p, tc), spec(p, tc)],
        out_specs=[spec(tc, tc), spec(tc, p), spec(tc, p), spec(p, tc), spec(p, tc), spec(1, p), spec(1, p)],
        out_shape=[jax.ShapeDtypeStruct((g, tc, tc), BF16),
                   jax.ShapeDtypeStruct((g, tc, p), BF16), jax.ShapeDtypeStruct((g, tc, p), BF16),
                   jax.ShapeDtypeStruct((g, p, tc), BF16), jax.ShapeDtypeStruct((g, p, tc), BF16),
                   jax.ShapeDtypeStruct((g, 1, p), F32), jax.ShapeDtypeStruct((g, 1, p), F32)],
        compiler_params=_cparams(("parallel",)), name="s5_params")(
            log_dt.reshape(g, 1, 1), a_re.reshape(g, p, 1), a_im.reshape(g, p, 1),
            a_re.reshape(g, 1, p), a_im.reshape(g, 1, p),
            tile_b(b_re), tile_b(b_im), tile_c(c_re), tile_c(c_im))


def _gelu_tanh(y):
    return 0.5 * y * (1.0 + jnp.tanh(math.sqrt(2.0 / math.pi) * (y + 0.044715 * (y * y * y))))


def _pick_segments(v, first):
    seg = lax.broadcasted_iota(jnp.int32, v[0].shape, 1) // SSM_GROUP
    out = v[(7 + first) % 8]
    for s in range(6, -1, -1):
        out = jnp.where(seg == s, v[(s + first) % 8], out)
    return out


def _s5_main_kernel(u_ref, m_ref, sbr_ref, sbi_ref, car_ref, cai_ref, a16r_ref, a16i_ref,
                    h0r_ref, h0i_ref, d_ref, z_ref, hpr_ref, hpi_ref, hsr_ref, hsi_ref,
                    u2_scr, sr_scr, si_scr, hr_scr, hi_scr, *, bp, nkp, bs, nks):
    t = S5_CHUNK
    npq = bp * nkp // 8
    nq = npq + bs * nks // 8
    sp = nkp * t

    def token0(q):
        return jnp.where(q < npq, (q % bp) * sp + (q // bp) * (8 * t), bp * sp + (q - npq) * (8 * t))

    def gather(q, carry):
        tok, rows = token0(q), pl.ds(pl.multiple_of(q * 8, 8), 8)
        for h in range(2):
            v = [u_ref[pl.ds(tok + 8 * h + i, 8, stride=t), :] for i in range(8)]
            v = [x if i == 0 else pltpu.roll(x, i * SSM_GROUP, 1) for i, x in enumerate(v)]
            for gi in range(8):
                u2_scr[gi, rows, h * LANES:(h + 1) * LANES] = _pick_segments(v, 8 - gi)
        return carry

    lax.fori_loop(0, nq, gather, 0, unroll=4)

    def group(gi, carry):
        u = u2_scr[gi]
        ub = u.astype(BF16)
        sr_scr[...] = _dot(ub, sbr_ref[gi])
        si_scr[...] = _dot(ub, sbi_ref[gi])
        ar, ai = a16r_ref[gi], a16i_ref[gi]

        def step(rows, hr, hi):
            hr_scr[rows, :] = hr
            hi_scr[rows, :] = hi
            return (ar * hr - ai * hi + sr_scr[rows, :], ar * hi + ai * hr + si_scr[rows, :])

        hr = hi = jnp.zeros((bp, ar.shape[1]), F32)
        for k in range(nkp):
            hr, hi = step(pl.ds((k // 8) * bp * 8 + k % 8, bp, stride=8), hr, hi)
        hpr_ref[gi] = hr
        hpi_ref[gi] = hi
        hr, hi = h0r_ref[gi], h0i_ref[gi]
        for k in range(nks):
            hr, hi = step(pl.ds(bp * nkp + k, bs, stride=nks), hr, hi)
        hsr_ref[gi] = hr
        hsi_ref[gi] = hi

        y = (_dot(ub, m_ref[gi]) + _dot(hr_scr[...].astype(BF16), car_ref[gi])
             + _dot(hi_scr[...].astype(BF16), cai_ref[gi]) + d_ref[gi] * u)
        u2_scr[gi] = _gelu_tanh(y)
        return carry

    lax.fori_loop(0, 8, group, 0)

    def scatter(q, carry):
        tok, rows = token0(q), pl.ds(pl.multiple_of(q * 8, 8), 8)
        for h in range(2):
            v = [u2_scr[gi, rows, h * LANES:(h + 1) * LANES] for gi in range(8)]
            for i in range(8):
                w = _pick_segments(v, 8 - i)
                w = w if i == 0 else pltpu.roll(w, LANES - i * SSM_GROUP, 1)
                z_ref[pl.ds(tok + 8 * h + i, 8, stride=t), :] = w
        return carry

    lax.fori_loop(0, nq, scatter, 0, unroll=4)


def _s5_layer(u, n_prompt, bp, bs, h0_re, h0_im, mats, d_skip):
    n, d = u.shape
    t, c = S5_CHUNK, SSM_GROUP
    g, tc = d // c, t * c
    gs = LANES // c
    p = h0_re.shape[-1]
    nkp = n_prompt // (bp * t)
    nks = (n - n_prompt) // (bs * t)
    r = nkp * bp + nks * bs
    assert gs == 8 and t == 16 and bp % 8 == 0 and nkp % 8 == 0 and (bs * nks) % 8 == 0

    m, sbr, sbi, car, cai, a16r, a16i = mats
    d_t = jnp.broadcast_to(d_skip.reshape(g, 1, 1, c), (g, 1, t, c)).reshape(g, 1, tc)
    h0r = jnp.swapaxes(h0_re, 0, 1)
    h0i = jnp.swapaxes(h0_im, 0, 1)

    def spec(*shape):
        return pl.BlockSpec((gs,) + shape, lambda i: (i, 0, 0))

    kern = functools.partial(_s5_main_kernel, bp=bp, nkp=nkp, bs=bs, nks=nks)
    z, hpr, hpi, hsr, hsi = pl.pallas_call(
        kern, grid=(g // gs,),
        in_specs=[pl.BlockSpec((n, LANES), lambda i: (0, i), pipeline_mode=pl.Buffered(1)),
                  spec(tc, tc), spec(tc, p), spec(tc, p), spec(p, tc), spec(p, tc),
                  spec(1, p), spec(1, p), spec(bs, p), spec(bs, p), spec(1, tc)],
        out_specs=[pl.BlockSpec((n, LANES), lambda i: (0, i)),
                   spec(bp, p), spec(bp, p), spec(bs, p), spec(bs, p)],
        out_shape=[jax.ShapeDtypeStruct((n, d), F32),
                   jax.ShapeDtypeStruct((g, bp, p), F32), jax.ShapeDtypeStruct((g, bp, p), F32),
                   jax.ShapeDtypeStruct((g, bs, p), F32), jax.ShapeDtypeStruct((g, bs, p), F32)],
        scratch_shapes=[pltpu.VMEM((gs, r, tc), F32)] + [pltpu.VMEM((r, p), F32) for _ in range(4)],
        compiler_params=_cparams(("arbitrary",)), name="s5_main")(
            u, m, sbr, sbi, car, cai, a16r, a16i, h0r, h0i, d_t)
    states = tuple(jnp.swapaxes(h, 0, 1)[:, None] for h in (hpr, hpi, hsr, hsi))
    return z, states


def _glu_kernel(x_ref, z_ref, wa_ref, wb_ref, o_ref):
    z = z_ref[...].astype(BF16)
    o_ref[...] = x_ref[...] + _dot(z, wa_ref[...]) * jax.nn.sigmoid(_dot(z, wb_ref[...]))


def _glu(x, z, wa, wb):
    n, d = x.shape
    tm = min(TOK_TM, n)
    row = pl.BlockSpec((tm, d), lambda i: (i, 0))
    mat = pl.BlockSpec((d, d), lambda i: (0, 0))
    return pl.pallas_call(
        _glu_kernel, grid=(n // tm,), in_specs=[row, row, mat, mat], out_specs=row,
        out_shape=jax.ShapeDtypeStruct((n, d), F32),
        compiler_params=_cparams(("parallel",)), name="glu")(x, z, wa, wb)


def _head_norm(x, e_ref, et_ref):
    hi, lo = _split2(x * x)
    ms = (_dot(hi, e_ref[...]) + _dot(lo, e_ref[...])) * (1.0 / HEAD_DIM)
    hi, lo = _split2(lax.rsqrt(ms + EPS))
    return _dot(hi, et_ref[...]) + _dot(lo, et_ref[...])


def _head_indicator(d):
    nh = d // HEAD_DIM
    e = (np.arange(d)[:, None] // HEAD_DIM == np.arange(LANES)[None, :]).astype(np.float32)
    assert nh <= LANES
    return jnp.asarray(e, BF16), jnp.asarray(e.T, BF16)


def _kvf_kernel(h_ref, wkv_ref, wf_ref, bf_ref, kn_ref, e_ref, et_ref,
                kp_ref, ks_ref, vp_ref, vs_ref, lp_ref, ls_ref, kb_ref, vb_ref, *, d, nh, npb):
    i = pl.program_id(0)
    h = h_ref[...]
    tm = h.shape[0]
    pkv = _dot(h, wkv_ref[...])
    kraw, v = pkv[:, :d], pkv[:, d:]
    k = kraw * _head_norm(kraw, e_ref, et_ref) * kn_ref[...]
    kb_ref[...] = k.astype(BF16)
    vb_ref[...] = v.astype(BF16)
    pf = _dot(h, wf_ref[...]) + bf_ref[...]
    lft = (jnp.minimum(pf, 0.0) - jnp.log1p(jnp.exp(-jnp.abs(pf)))).T[:nh]

    @pl.when(i < npb)
    def _():
        kp_ref[0] = k.T
        vp_ref[0] = v.T
        lp_ref[0] = lft

    @pl.when(i >= npb)
    def _():
        for hd in range(nh):
            rows = pl.ds(hd, tm, stride=nh)
            ks_ref[rows, :] = k[:, hd * HEAD_DIM:(hd + 1) * HEAD_DIM]
            vs_ref[rows, :] = v[:, hd * HEAD_DIM:(hd + 1) * HEAD_DIM]
        lq = ls_ref.shape[2]
        for j in range(tm // lq):
            ls_ref[j] = lft[:, j * lq:(j + 1) * lq]


def _kvf(hn, bp, s, lq, w_kv, w_f, b_f, k_norm):
    n, d = hn.shape
    nh = d // HEAD_DIM
    tm = min(TOK_TM, s)
    n_prompt = bp * s
    n_s = n - n_prompt
    assert s % tm == 0 and n_s % tm == 0 and tm % lq == 0
    npb, spb = n_prompt // tm, s // tm
    e, et = _head_indicator(d)
    row = pl.BlockSpec((tm, d), lambda i: (i, 0))
    const = lambda *sh: pl.BlockSpec(sh, lambda i: (0, 0))
    prompt_t = lambda rows: pl.BlockSpec(
        (1, rows, tm), lambda i: (jnp.minimum(i, npb - 1) // spb, 0, jnp.minimum(i, npb - 1) % spb))
    second = lambda i: (jnp.maximum(i - npb, 0), 0)
    kern = functools.partial(_kvf_kernel, d=d, nh=nh, npb=npb)
    return pl.pallas_call(
        kern, grid=(n // tm,),
        in_specs=[row, const(d, 2 * d), const(d, LANES), const(1, LANES), const(1, d), const(d, LANES), const(LANES, d)],
        out_specs=[prompt_t(d), pl.BlockSpec((tm * nh, HEAD_DIM), second),
                   prompt_t(d), pl.BlockSpec((tm * nh, HEAD_DIM), second),
                   prompt_t(nh), pl.BlockSpec((tm // lq, nh, lq), lambda i: (jnp.maximum(i - npb, 0), 0, 0)), row, row],
        out_shape=[jax.ShapeDtypeStruct((bp, d, s), F32), jax.ShapeDtypeStruct((n_s * nh, HEAD_DIM), F32),
                   jax.ShapeDtypeStruct((bp, d, s), F32), jax.ShapeDtypeStruct((n_s * nh, HEAD_DIM), F32),
                   jax.ShapeDtypeStruct((bp, nh, s), F32), jax.ShapeDtypeStruct((n_s // lq, nh, lq), F32),
                   jax.ShapeDtypeStruct((n, d), BF16), jax.ShapeDtypeStruct((n, d), BF16)],
        compiler_params=_cparams(("arbitrary",)), name="kvf")(
            hn, w_kv, jnp.pad(w_f, ((0, 0), (0, LANES - nh))), jnp.pad(b_f, (0, LANES - nh)).reshape(1, LANES),
            jnp.tile(k_norm, nh).reshape(1, d), e, et)


def _qproj_kernel(u_ref, wq_ref, qn_ref, e_ref, et_ref, q_ref):
    q = _dot(u_ref[...], wq_ref[...])
    q_ref[...] = (q * _head_norm(q, e_ref, et_ref) * qn_ref[...]).astype(BF16)


def _qproj(un, wq, qn_scaled):
    n, d = un.shape
    tm = min(TOK_TM, n)
    e, et = _head_indicator(d)
    row = pl.BlockSpec((tm, d), lambda i: (i, 0))
    const = lambda *s: pl.BlockSpec(s, lambda i: (0, 0))
    return pl.pallas_call(
        _qproj_kernel, grid=(n // tm,),
        in_specs=[row, const(d, d), const(1, d), const(d, LANES), const(LANES, d)], out_specs=row,
        out_shape=jax.ShapeDtypeStruct((n, d), BF16),
        compiler_params=_cparams(("parallel",)), name="qproj")(un, wq, qn_scaled, e, et)


def _oproj_kernel(x_ref, ap_ref, as_ref, wo_ref, o_ref, *, npb):
    a = jnp.where(pl.program_id(0) < npb, ap_ref[...], as_ref[...])
    o_ref[...] = x_ref[...] + _dot(a, wo_ref[...])


def _oproj(x, a_p, a_s, wo):
    n, d = x.shape
    tm = min(TOK_TM, n)
    assert a_p.shape[0] % tm == 0 and a_s.shape[0] % tm == 0
    npb = a_p.shape[0] // tm
    first, second = _part_maps(npb)
    row = pl.BlockSpec((tm, d), lambda i: (i, 0))
    return pl.pallas_call(
        functools.partial(_oproj_kernel, npb=npb), grid=(n // tm,),
        in_specs=[row, pl.BlockSpec((tm, d), first), pl.BlockSpec((tm, d), second),
                  pl.BlockSpec((d, d), lambda i: (0, 0))],
        out_specs=row, out_shape=jax.ShapeDtypeStruct((n, d), F32),
        compiler_params=_cparams(("parallel",)), name="oproj")(x, a_p, a_s, wo)


def _cumsum_kernel(x_ref, init_ref, o_ref, *, nblk, bw):
    tri = (lax.broadcasted_iota(jnp.int32, (bw, bw), 0) <= lax.broadcasted_iota(jnp.int32, (bw, bw), 1)).astype(BF16)
    carry = init_ref[...]
    for j in range(nblk):
        x = x_ref[:, j * bw:(j + 1) * bw]
        hi, mid, lo = _split3(x)
        cs = _dot(hi, tri) + _dot(mid, tri) + _dot(lo, tri) + carry
        o_ref[:, j * bw:(j + 1) * bw] = cs
        carry = cs[:, bw - 1:bw]


def _cumsum_rows(x, init):
    r, w = x.shape
    bw = min(256, w)
    kern = functools.partial(_cumsum_kernel, nblk=w // bw, bw=bw)
    return pl.pallas_call(
        kern, grid=(1,),
        in_specs=[pl.BlockSpec((r, w), lambda i: (0, 0)), pl.BlockSpec((r, 1), lambda i: (0, 0))],
        out_specs=pl.BlockSpec((r, w), lambda i: (0, 0)), out_shape=jax.ShapeDtypeStruct((r, w), F32),
        compiler_params=_cparams(("arbitrary",)), name="cumsum")(x, init)


def _lanes(x, w):
    if w <= LANES:
        return x[:, :w]
    return jnp.concatenate([x] * (w // LANES), axis=1)


def _head_masks(shape):
    head = lax.broadcasted_iota(jnp.int32, shape, 1) // HEAD_DIM
    return [head == i for i in range(shape[1] // HEAD_DIM)]


def _stack_heads(qs):
    zero = jnp.zeros_like(qs)
    return jnp.concatenate([jnp.where(mk, qs, zero) for mk in _head_masks(qs.shape)], axis=0)


def _flash_init(cq, hs, m_scr, acc_scr, cq_scr):
    tq = cq.shape[0]
    m_scr[...] = jnp.full_like(m_scr, NEG_BIG)
    acc_scr[...] = jnp.zeros_like(acc_scr)
    for g in range(cq_scr.shape[0]):
        for i in range(hs):
            h = g * hs + i
            cq_scr[g, i * tq:(i + 1) * tq, :] = jnp.broadcast_to(cq[:, h:h + 1], (tq, LANES))


def _flash_slab(g, q_st, ks, vs, ck_rows, causal, m_scr, acc_scr, cq_scr, kv_t=False):
    hs = len(ck_rows)
    tq, tk = q_st.shape[0] // hs, ck_rows[0].shape[1]
    nt = (((1,), (1,)), ((), ()))
    s = _dot(q_st, ks) if kv_t else lax.dot_general(q_st, ks, nt, preferred_element_type=F32)
    t = jnp.concatenate([s[i * tq:(i + 1) * tq] - ck_rows[i] for i in range(hs)], axis=0)
    if causal is not None:
        keep = lax.broadcasted_iota(jnp.int32, (tq, tk), 1) <= lax.broadcasted_iota(jnp.int32, (tq, tk), 0) + causal
        t = jnp.where(jnp.concatenate([keep] * hs, axis=0), t, NEG_BIG)
    cq = cq_scr[g]
    m_prev = m_scr[g]
    m_new = jnp.maximum(m_prev, jnp.max(t, axis=1, keepdims=True) + cq)
    alpha = jnp.exp(m_prev - m_new)
    pe = jnp.exp(t - _lanes(m_new - cq, tk))
    w = hs * HEAD_DIM
    if w == LANES and not kv_t:
        pv = _dot(pe.astype(BF16), jnp.concatenate([vs, jnp.ones((tk, LANES), BF16)], axis=1))
    else:
        pv = lax.dot_general(pe.astype(BF16), vs, nt, preferred_element_type=F32) if kv_t else _dot(pe.astype(BF16), vs)
        pv = jnp.concatenate([pv, jnp.broadcast_to(jnp.sum(pe, axis=1, keepdims=True), (hs * tq, LANES))], axis=1)
    acc_scr[g] = _lanes(alpha, w + LANES) * acc_scr[g] + pv
    m_scr[g] = m_new


def _flash_finish(o_ref, hs, acc_scr):
    tq = o_ref.shape[0]
    w = hs * HEAD_DIM
    masks = _head_masks((tq, w))
    for g in range(acc_scr.shape[0]):
        acc = acc_scr[g]
        o = acc[:, :w] / _lanes(acc[:, w:], w)
        out = o[(hs - 1) * tq:]
        for i in range(hs - 2, -1, -1):
            out = jnp.where(masks[i], o[i * tq:(i + 1) * tq], out)
        o_ref[:, g * w:(g + 1) * w] = out.astype(o_ref.dtype)


def _flash_scratch(ng, rows, w):
    return [pltpu.VMEM((ng, rows, LANES), F32), pltpu.VMEM((ng, rows, w + LANES), F32),
            pltpu.VMEM((ng, rows, LANES), F32)]


def _attn_prompt_kernel(qi_tab, ki_tab, q_ref, k_ref, v_ref, cq_ref, ck_ref, o_ref, *scr, hs):
    step = pl.program_id(1)
    qi, ki = qi_tab[step], ki_tab[step]
    tq, tk = q_ref.shape[0], k_ref.shape[0]
    w = hs * HEAD_DIM
    ng = q_ref.shape[1] // w
    last = (qi * tq + tq - 1) // tk

    @pl.when(ki == 0)
    def _():
        _flash_init(cq_ref[...], hs, *scr)

    def sweep(causal, nkeys):
        ck = ck_ref[0]
        for g in range(ng):
            sl = slice(g * w, (g + 1) * w)
            _flash_slab(g, _stack_heads(q_ref[:, sl]), k_ref[:nkeys, sl], v_ref[:nkeys, sl],
                        [ck[g * hs + i:g * hs + i + 1, :nkeys] for i in range(hs)], causal, *scr)

    @pl.when(ki < last)
    def _():
        sweep(None, tk)

    for off in range(0, tk, tq):
        @pl.when((ki == last) & (qi * tq - ki * tk == off))
        def _():
            sweep(off, off + tq)
            _flash_finish(o_ref, hs, scr[1])


def _attn_prompt(q, kb, vb, c_rows, c_t, nb, s):
    d = q.shape[1]
    nh = d // HEAD_DIM
    hs = ATT_HS
    tq, tk = min(ATT_TQ, s), min(ATT_TK, s)
    assert tk % tq == 0 and s % tk == 0
    nq, nk = s // tq, s // tk
    pairs = [(i, j) for i in range(nq) for j in range((i * tq + tq - 1) // tk + 1)]
    qi_tab = jnp.asarray([p[0] for p in pairs], jnp.int32)
    ki_tab = jnp.asarray([p[1] for p in pairs], jnp.int32)
    grid_spec = pltpu.PrefetchScalarGridSpec(
        num_scalar_prefetch=2, grid=(nb, len(pairs)),
        in_specs=[pl.BlockSpec((tq, d), lambda b, p, qt, kt: (b * nq + qt[p], 0)),
                  pl.BlockSpec((tk, d), lambda b, p, qt, kt: (b * nk + kt[p], 0)),
                  pl.BlockSpec((tk, d), lambda b, p, qt, kt: (b * nk + kt[p], 0)),
                  pl.BlockSpec((tq, nh), lambda b, p, qt, kt: (b * nq + qt[p], 0)),
                  pl.BlockSpec((1, nh, tk), lambda b, p, qt, kt: (b, 0, kt[p]))],
        out_specs=pl.BlockSpec((tq, d), lambda b, p, qt, kt: (b * nq + qt[p], 0)),
        scratch_shapes=_flash_scratch(nh // hs, hs * tq, hs * HEAD_DIM))
    return pl.pallas_call(
        functools.partial(_attn_prompt_kernel, hs=hs), grid_spec=grid_spec,
        out_shape=jax.ShapeDtypeStruct((nb * s, d), BF16),
        compiler_params=_cparams(("parallel", "arbitrary")), name="attn_prompt")(
            qi_tab, ki_tab, q, kb, vb, c_rows, c_t)


def _attn_sample_kernel(q_ref, ck_ref, cv_ref, kn_ref, vn_ref, cq_ref, ckc_ref, ckn_ref, o_ref, *scr, hs, nkb):
    j = pl.program_id(1)
    w = hs * HEAD_DIM
    ng = q_ref.shape[1] // w

    @pl.when(j == 0)
    def _():
        _flash_init(cq_ref[0], hs, *scr)

    def sweep(kslab, vslab, ck, causal, kv_t):
        for g in range(ng):
            _flash_slab(g, _stack_heads(q_ref[:, g * w:(g + 1) * w]), kslab(g), vslab(g),
                        [ck[g * hs + i:g * hs + i + 1, :] for i in range(hs)], causal, *scr, kv_t=kv_t)

    @pl.when(j < nkb)
    def _():
        sweep(lambda g: ck_ref[0, g * w:(g + 1) * w, :].astype(BF16),
              lambda g: cv_ref[0, g * w:(g + 1) * w, :].astype(BF16), ckc_ref[0], None, True)

    @pl.when(j == nkb)
    def _():
        sweep(lambda g: kn_ref[:, g * w:(g + 1) * w], lambda g: vn_ref[:, g * w:(g + 1) * w], ckn_ref[0], 0, False)
        _flash_finish(o_ref, hs, scr[1])


def _attn_sample(q, kb, vb, cache_kt, cache_vt, c_new, c_past_t, c_new_t, n_prompt):
    d = q.shape[1]
    nh = d // HEAD_DIM
    nb, _, plen = cache_kt.shape
    lq = c_new.shape[1]
    hs = DEC_HS
    tk = min(DEC_TK, plen)
    nkb = plen // tk
    r0 = n_prompt // lq
    new_rows = pl.BlockSpec((lq, d), lambda b, j: (r0 + b, 0))
    cache = pl.BlockSpec((1, d, tk), lambda b, j: (b, 0, jnp.minimum(j, nkb - 1)))
    kern = functools.partial(_attn_sample_kernel, hs=hs, nkb=nkb)
    return pl.pallas_call(
        kern, grid=(nb, nkb + 1),
        in_specs=[new_rows, cache, cache, new_rows, new_rows,
                  pl.BlockSpec((1, lq, nh), lambda b, j: (b, 0, 0)),
                  pl.BlockSpec((1, nh, tk), lambda b, j: (b, 0, jnp.minimum(j, nkb - 1))),
                  pl.BlockSpec((1, nh, lq), lambda b, j: (b, 0, 0))],
        out_specs=pl.BlockSpec((lq, d), lambda b, j: (b, 0)),
        out_shape=jax.ShapeDtypeStruct((nb * lq, d), BF16),
        scratch_shapes=_flash_scratch(nh // hs, hs * lq, hs * HEAD_DIM),
        compiler_params=_cparams(("parallel", "arbitrary")), name="attn_sample")(
            q, cache_kt, cache_vt, kb, vb, c_new, c_past_t, c_new_t)


def kernel(x_prompt, x_sample, cache_k, cache_v, cache_logf, state_ssm_re, state_ssm_im, ffn_norm, w_ffn_gate, w_ffn_up, w_ffn_down, mix_norm, ssm_a_re, ssm_a_im, ssm_log_dt, ssm_b_re, ssm_b_im, ssm_c_re, ssm_c_im, ssm_d, w_glu_a, w_glu_b, kv_norm, w_kvf, b_f, k_norm, w_q, q_norm, w_o):
    bp, s, d = x_prompt.shape
    bs, lq, _ = x_sample.shape
    plen = cache_k.shape[1]
    nh = d // HEAD_DIM
    assert ffn_norm.shape[0] == 2 and ssm_a_re.shape[0] == 1 and w_q.shape[0] == 1
    n_prompt, n_sample = bp * s, bs * lq
    bf = lambda w: w.astype(BF16)

    def ffn(xs, l, j, **kw):
        return _ffn(xs, ffn_norm[l, j], w_ffn_gate, w_ffn_up, w_ffn_down, (l, j), **kw)

    x, u = ffn([x_prompt.reshape(n_prompt, d), x_sample.reshape(n_sample, d)], 0, 0, g2=mix_norm[0], norm_dtype=F32)
    mats = _s5_params(ssm_log_dt[0], ssm_a_re[0], ssm_a_im[0], ssm_b_re[0], ssm_b_im[0], ssm_c_re[0], ssm_c_im[0])
    z, (p_re, p_im, s_re, s_im) = _s5_layer(u, n_prompt, bp, bs, state_ssm_re[:, 0], state_ssm_im[:, 0], mats, ssm_d[0])
    x = _glu(x, z, bf(w_glu_a[0]), bf(w_glu_b[0]))
    x, hn = ffn([x], 0, 1, g2=kv_norm)

    kt_p, k_s, vt_p, v_s, lft_p, lft_s, kb, vb = _kvf(hn, bp, s, lq, bf(w_kvf[:, :2 * d]), bf(w_kvf[:, 2 * d:]), b_f, k_norm)

    def cumsum_t(lft, init):
        b, _, l = lft.shape
        init = jnp.zeros((b * nh, 1), F32) if init is None else init.reshape(b * nh, 1)
        return _cumsum_rows(lft.reshape(b * nh, l), init).reshape(b, nh, l)

    cp_t = cumsum_t(lft_p, None)
    cpast_t = cumsum_t(jnp.swapaxes(cache_logf.astype(F32), 1, 2), None)
    cnew_t = cumsum_t(lft_s, cpast_t[:, :, -1])

    x, un = ffn([x], 1, 0, g2=mix_norm[1])
    q = _qproj(un, bf(w_q[0]), (jnp.tile(q_norm[0], nh) * (1.0 / math.sqrt(HEAD_DIM))).reshape(1, d))
    a_p = _attn_prompt(q, kb, vb, jnp.swapaxes(cp_t, 1, 2).reshape(n_prompt, nh), cp_t, bp, s)
    pos_minor = lambda c: jnp.transpose(c, (0, 2, 3, 1)).reshape(bs, d, plen)
    a_s = _attn_sample(q, kb, vb, pos_minor(cache_k), pos_minor(cache_v),
                       jnp.swapaxes(cnew_t, 1, 2), cpast_t, cnew_t, n_prompt)
    x = _oproj(x, a_p, a_s, bf(w_o[0]))
    (y_p, y_s), _ = ffn([x], 1, 1, split_out=(n_prompt, n_sample))

    cache_p = lambda a: jnp.transpose(a.reshape(bp, nh, HEAD_DIM, s), (0, 3, 1, 2))
    cache_s = lambda a: a.reshape(bs, lq, nh, HEAD_DIM)
    return (y_p.reshape(bp, s, d), y_s.reshape(bs, lq, d), p_re, p_im, cache_p(kt_p), cache_p(vt_p),
            jnp.swapaxes(lft_p, 1, 2), s_re, s_im, cache_s(k_s), cache_s(v_s), jnp.swapaxes(lft_s, 1, 2))
```

```python
import functools
import math

import jax
import jax.numpy as jnp
import numpy as np
from jax import lax
from jax.experimental import pallas as pl
from jax.experimental.pallas import tpu as pltpu

F32 = jnp.float32
BF16 = jnp.bfloat16

EPS = 1e-6
HEAD_DIM = 64
SSM_GROUP = 16
S5_CHUNK = 16
LANES = 128
_SLAB_GROUPS = LANES // SSM_GROUP
NEG_BIG = -1e30
VMEM_LIMIT = 57 * 1024 * 1024

FFN_TM = 1024
FFN_TF = 512
TOK_TM = 512
ATT_TQ = 256
ATT_TK = 512
ATT_HS = 2
DEC_TK = 1024
DEC_HS = 4


def _cparams(sem):
    return pltpu.CompilerParams(dimension_semantics=sem, vmem_limit_bytes=VMEM_LIMIT)


def _rms_scale(x):
    return lax.rsqrt(jnp.mean(x * x, axis=-1, keepdims=True) + EPS)


def _dot(a, b):
    return jnp.dot(a, b, preferred_element_type=F32)


def _split2(x):
    hi = x.astype(BF16)
    lo = (x - hi.astype(F32)).astype(BF16)
    return hi, lo


def _split3(x):
    hi = x.astype(BF16)
    r1 = x - hi.astype(F32)
    mid = r1.astype(BF16)
    return hi, mid, (r1 - mid.astype(F32)).astype(BF16)


def _part_maps(npb):
    first = lambda i, *_: (jnp.minimum(i, npb - 1), 0)
    second = lambda i, *_: (jnp.maximum(i - npb, 0), 0)
    return first, second


def _ffn_kernel(*refs, nf, npb, n_in, n_out, with_norm):
    x_refs, refs = refs[:n_in], refs[n_in:]
    g_ref, wg_ref, wu_ref, wd_ref = refs[:4]
    refs = refs[4:]
    g2_ref = refs[0] if with_norm else None
    refs = refs[with_norm:]
    o_refs, refs = refs[:n_out], refs[n_out:]
    o2_ref = refs[0] if with_norm else None
    h_scr, acc_scr = refs[with_norm:]
    i, f = pl.program_id(0), pl.program_id(1)

    def load_x():
        if n_in == 1:
            return x_refs[0][...]
        return jnp.where(i < npb, x_refs[0][...], x_refs[1][...])

    @pl.when(f == 0)
    def _():
        x = load_x()
        h_scr[...] = (x * _rms_scale(x) * g_ref[...]).astype(BF16)
        acc_scr[...] = jnp.zeros_like(acc_scr)

    h = h_scr[...]
    a = _dot(h, wg_ref[...].astype(BF16))
    b = _dot(h, wu_ref[...].astype(BF16))
    t = (a * jax.nn.sigmoid(a)) * b
    acc_scr[...] += _dot(t.astype(BF16), wd_ref[...].astype(BF16))

    @pl.when(f == nf - 1)
    def _():
        y = load_x() + 0.5 * acc_scr[...]
        if n_out == 1:
            o_refs[0][...] = y
        else:
            @pl.when(i < npb)
            def _():
                o_refs[0][...] = y

            @pl.when(i >= npb)
            def _():
                o_refs[1][...] = y
        if with_norm:
            o2_ref[...] = (y * _rms_scale(y) * g2_ref[...]).astype(o2_ref.dtype)


def _ffn(xs, g, wg, wu, wd, lj, g2=None, split_out=None, norm_dtype=BF16):
    d = xs[0].shape[1]
    n = sum(x.shape[0] for x in xs)
    dff = wg.shape[-1]
    l, j = lj
    tm, tf = min(FFN_TM, n), min(FFN_TF, dff)
    nf = dff // tf
    n_first = xs[0].shape[0] if len(xs) == 2 else (split_out[0] if split_out else n)
    assert all(x.shape[0] % tm == 0 for x in xs) and n_first % tm == 0
    npb = n_first // tm
    first, second = _part_maps(npb)
    with_norm = g2 is not None
    row = pl.BlockSpec((tm, d), lambda i, f: (i, 0))
    vec = pl.BlockSpec((1, d), lambda i, f: (0, 0))
    parts = [pl.BlockSpec((tm, d), first), pl.BlockSpec((tm, d), second)]
    in_specs = (parts if len(xs) == 2 else [row]) + [
        vec, pl.BlockSpec((None, None, d, tf), lambda i, f: (l, j, 0, f)),
        pl.BlockSpec((None, None, d, tf), lambda i, f: (l, j, 0, f)),
        pl.BlockSpec((None, None, tf, d), lambda i, f: (l, j, f, 0))]
    args = list(xs) + [g.reshape(1, d), wg, wu, wd]
    if split_out:
        out_shape = [jax.ShapeDtypeStruct((m, d), F32) for m in split_out]
        out_specs = list(parts)
    else:
        out_shape = [jax.ShapeDtypeStruct((n, d), F32)]
        out_specs = [row]
    n_out = len(out_shape)
    if with_norm:
        in_specs.append(vec)
        args.append(g2.reshape(1, d))
        out_shape.append(jax.ShapeDtypeStruct((n, d), norm_dtype))
        out_specs.append(row)
    kern = functools.partial(_ffn_kernel, nf=nf, npb=npb, n_in=len(xs), n_out=n_out, with_norm=with_norm)
    res = pl.pallas_call(
        kern, grid=(n // tm, nf), in_specs=in_specs, out_specs=out_specs, out_shape=out_shape,
        scratch_shapes=[pltpu.VMEM((tm, d), BF16), pltpu.VMEM((tm, d), F32)],
        compiler_params=_cparams(("arbitrary", "arbitrary")), name="ffn")(*args)
    y = tuple(res[:n_out]) if split_out else res[0]
    return y, (res[n_out] if with_norm else None)


def _s5_param_kernel(*refs):
    lax.fori_loop(0, _SLAB_GROUPS, functools.partial(_s5_param_group, refs), 0)


def _s5_param_group(refs, slot, carry):
    (ldt_ref, arc_ref, aic_ref, arr_ref, air_ref, btr_ref, bti_ref, ctr_ref, cti_ref,
     m_ref, sbr_ref, sbi_ref, car_ref, cai_ref, a16r_ref, a16i_ref) = refs
    t, c = S5_CHUNK, SSM_GROUP
    dt = jnp.exp(ldt_ref[slot])

    def cpow(ar, ai, n):
        mag = jnp.exp(ar * dt)
        pr, pi = mag * jnp.cos(ai * dt), mag * jnp.sin(ai * dt)
        n = jnp.asarray(n, jnp.int32)
        shape = jnp.broadcast_shapes(ar.shape, n.shape)
        re, im = jnp.ones(shape, F32), jnp.zeros(shape, F32)
        for b in range(t.bit_length()):
            bit = ((n >> b) & 1) == 1
            re, im = jnp.where(bit, re * pr - im * pi, re), jnp.where(bit, re * pi + im * pr, im)
            pr, pi = pr * pr - pi * pi, 2.0 * pr * pi
        return re, im

    def step_of(pos):
        seg = pos // c
        return (seg // _SLAB_GROUPS) * _SLAB_GROUPS + ((seg - slot) & (_SLAB_GROUPS - 1))

    arc, aic = arc_ref[slot], aic_ref[slot]
    arr, air = arr_ref[slot], air_ref[slot]
    p = arc.shape[0]
    lane_pc = lax.broadcasted_iota(jnp.int32, (p, t * c), 1)
    cr, ci = ctr_ref[slot], cti_ref[slot]
    pr, pi = cpow(arc, aic, step_of(lane_pc) + 1)
    car_ref[slot] = (cr * pr - ci * pi).astype(BF16)
    cai_ref[slot] = (-(cr * pi + ci * pr)).astype(BF16)
    pr, pi = cpow(arc, aic, lane_pc // c)
    csr = cr * pr - ci * pi
    csi = cr * pi + ci * pr

    abr, abi = cpow(arr, air, 1)
    xr, xi = abr - 1.0, abi
    den = arr * arr + air * air
    qr = (xr * arr + xi * air) / den
    qi = (xi * arr - xr * air) / den
    btr, bti = btr_ref[slot], bti_ref[slot]
    bbr = qr * btr - qi * bti
    bbi = qr * bti + qi * btr
    iidx = step_of(lax.broadcasted_iota(jnp.int32, (t * c, p), 0))
    pr, pi = cpow(arr, air, (t - 1) - iidx)
    sbr_ref[slot] = (bbr * pr - bbi * pi).astype(BF16)
    sbi_ref[slot] = (bbr * pi + bbi * pr).astype(BF16)
    pr, pi = cpow(arr, air, t)
    a16r_ref[slot] = pr
    a16i_ref[slot] = pi

    hp = lax.Precision.HIGHEST
    kt = (jnp.dot(bbr[:c], csr, precision=hp, preferred_element_type=F32)
          - jnp.dot(bbi[:c], csi, precision=hp, preferred_element_type=F32))
    lane = lax.broadcasted_iota(jnp.int32, (c, t * c), 1)
    for i in range(t):
        blk = kt if i == 0 else pltpu.roll(kt, i * c, 1)
        blk = jnp.where(lane >= i * c, blk, 0.0)
        blk = jnp.concatenate([pltpu.roll(blk[:, h * LANES:(h + 1) * LANES], slot * c, 1)
                               for h in range(t * c // LANES)], axis=1)
        seg = (i // _SLAB_GROUPS) * _SLAB_GROUPS + ((i + slot) & (_SLAB_GROUPS - 1))
        m_ref[slot, pl.ds(pl.multiple_of(seg * c, c), c), :] = blk.astype(BF16)
    return carry


def _s5_params(log_dt, a_re, a_im, b_re, b_im, c_re, c_im):
    g, p = a_re.shape
    t, c = S5_CHUNK, SSM_GROUP
    tc = t * c

    def tile_b(b):
        return jnp.broadcast_to(jnp.swapaxes(b, 1, 2)[:, None], (g, t, c, p)).reshape(g, tc, p)

    def tile_c(cm):
        return jnp.broadcast_to(jnp.swapaxes(cm, 1, 2)[:, :, None], (g, p, t, c)).reshape(g, p, tc)

    def spec(*shape):
        return pl.BlockSpec((_SLAB_GROUPS,) + shape, lambda i: (i, 0, 0))

    assert g % _SLAB_GROUPS == 0
    return pl.pallas_call(
        _s5_param_kernel, grid=(g // _SLAB_GROUPS,),
        in_specs=[spec(1, 1), spec(p, 1), spec(p, 1), spec(1, p), spec(1, p),
                  spec(tc, p), spec(tc, p), spec(p, tc), spec(p, tc)],
        out_specs=[spec(tc, tc), spec(tc, p), spec(tc, p), spec(p, tc), spec(p, tc), spec(1, p), spec(1, p)],
        out_shape=[jax.ShapeDtypeStruct((g, tc, tc), BF16),
                   jax.ShapeDtypeStruct((g, tc, p), BF16), jax.ShapeDtypeStruct((g, tc, p), BF16),
                   jax.ShapeDtypeStruct((g, p, tc), BF16), jax.ShapeDtypeStruct((g, p, tc), BF16),
                   jax.ShapeDtypeStruct((g, 1, p), F32), jax.ShapeDtypeStruct((g, 1, p), F32)],
        compiler_params=_cparams(("parallel",)), name="s5_params")(
            log_dt.reshape(g, 1, 1), a_re.reshape(g, p, 1), a_im.reshape(g, p, 1),
            a_re.reshape(g, 1, p), a_im.reshape(g, 1, p),
            tile_b(b_re), tile_b(b_im), tile_c(c_re), tile_c(c_im))


def _gelu_tanh(y):
    return 0.5 * y * (1.0 + jnp.tanh(math.sqrt(2.0 / math.pi) * (y + 0.044715 * (y * y * y))))


def _pick_segments(v, first):
    seg = lax.broadcasted_iota(jnp.int32, v[0].shape, 1) // SSM_GROUP
    out = v[(7 + first) % 8]
    for s in range(6, -1, -1):
        out = jnp.where(seg == s, v[(s + first) % 8], out)
    return out


def _s5_main_kernel(u_ref, m_ref, sbr_ref, sbi_ref, car_ref, cai_ref, a16r_ref, a16i_ref,
                    h0r_ref, h0i_ref, d_ref, z_ref, hpr_ref, hpi_ref, hsr_ref, hsi_ref,
                    u2_scr, sr_scr, si_scr, hr_scr, hi_scr, *, bp, nkp, bs, nks):
    t = S5_CHUNK
    npq = bp * nkp // 8
    nq = npq + bs * nks // 8
    sp = nkp * t

    def token0(q):
        return jnp.where(q < npq, (q % bp) * sp + (q // bp) * (8 * t), bp * sp + (q - npq) * (8 * t))

    def gather(q, carry):
        tok, rows = token0(q), pl.ds(pl.multiple_of(q * 8, 8), 8)
        for h in range(2):
            v = [u_ref[pl.ds(tok + 8 * h + i, 8, stride=t), :] for i in range(8)]
            v = [x if i == 0 else pltpu.roll(x, i * SSM_GROUP, 1) for i, x in enumerate(v)]
            for gi in range(8):
                u2_scr[gi, rows, h * LANES:(h + 1) * LANES] = _pick_segments(v, 8 - gi)
        return carry

    lax.fori_loop(0, nq, gather, 0, unroll=4)

    def group(gi, buf):
        sr, si, hrs, his = sr_scr.at[buf], si_scr.at[buf], hr_scr.at[buf], hi_scr.at[buf]
        u = u2_scr[gi]
        ub = u.astype(BF16)
        sr[...] = _dot(ub, sbr_ref[gi])
        si[...] = _dot(ub, sbi_ref[gi])
        ar, ai = a16r_ref[gi], a16i_ref[gi]

        def step(rows, hr, hi):
            hrs[rows, :] = hr
            his[rows, :] = hi
            return (ar * hr - ai * hi + sr[rows, :], ar * hi + ai * hr + si[rows, :])

        hr = hi = jnp.zeros((bp, ar.shape[1]), F32)
        for k in range(nkp):
            hr, hi = step(pl.ds((k // 8) * bp * 8 + k % 8, bp, stride=8), hr, hi)
        hpr_ref[gi] = hr
        hpi_ref[gi] = hi
        hr, hi = h0r_ref[gi], h0i_ref[gi]
        for k in range(nks):
            hr, hi = step(pl.ds(bp * nkp + k, bs, stride=nks), hr, hi)
        hsr_ref[gi] = hr
        hsi_ref[gi] = hi

        y = (_dot(ub, m_ref[gi]) + _dot(hrs[...].astype(BF16), car_ref[gi])
             + _dot(his[...].astype(BF16), cai_ref[gi]) + d_ref[gi] * u)
        u2_scr[gi] = _gelu_tanh(y)

    def group_pair(j, carry):
        group(2 * j, 0)
        group(2 * j + 1, 1)
        return carry

    lax.fori_loop(0, _SLAB_GROUPS // 2, group_pair, 0)

    def scatter(q, carry):
        tok, rows = token0(q), pl.ds(pl.multiple_of(q * 8, 8), 8)
        for h in range(2):
            v = [u2_scr[gi, rows, h * LANES:(h + 1) * LANES] for gi in range(8)]
            for i in range(8):
                w = _pick_segments(v, 8 - i)
                w = w if i == 0 else pltpu.roll(w, LANES - i * SSM_GROUP, 1)
                z_ref[pl.ds(tok + 8 * h + i, 8, stride=t), :] = w
        return carry

    lax.fori_loop(0, nq, scatter, 0, unroll=4)


def _s5_layer(u, n_prompt, bp, bs, h0_re, h0_im, mats, d_skip):
    n, d = u.shape
    t, c = S5_CHUNK, SSM_GROUP
    g, tc = d // c, t * c
    gs = LANES // c
    p = h0_re.shape[-1]
    nkp = n_prompt // (bp * t)
    nks = (n - n_prompt) // (bs * t)
    r = nkp * bp + nks * bs
    assert gs == 8 and t == 16 and bp % 8 == 0 and nkp % 8 == 0 and (bs * nks) % 8 == 0

    m, sbr, sbi, car, cai, a16r, a16i = mats
    d_t = jnp.broadcast_to(d_skip.reshape(g, 1, 1, c), (g, 1, t, c)).reshape(g, 1, tc)
    h0r = jnp.swapaxes(h0_re, 0, 1)
    h0i = jnp.swapaxes(h0_im, 0, 1)

    def spec(*shape):
        return pl.BlockSpec((gs,) + shape, lambda i: (i, 0, 0))

    kern = functools.partial(_s5_main_kernel, bp=bp, nkp=nkp, bs=bs, nks=nks)
    z, hpr, hpi, hsr, hsi = pl.pallas_call(
        kern, grid=(g // gs,),
        in_specs=[pl.BlockSpec((n, LANES), lambda i: (0, i), pipeline_mode=pl.Buffered(1)),
                  spec(tc, tc), spec(tc, p), spec(tc, p), spec(p, tc), spec(p, tc),
                  spec(1, p), spec(1, p), spec(bs, p), spec(bs, p), spec(1, tc)],
        out_specs=[pl.BlockSpec((n, LANES), lambda i: (0, i)),
                   spec(bp, p), spec(bp, p), spec(bs, p), spec(bs, p)],
        out_shape=[jax.ShapeDtypeStruct((n, d), F32),
                   jax.ShapeDtypeStruct((g, bp, p), F32), jax.ShapeDtypeStruct((g, bp, p), F32),
                   jax.ShapeDtypeStruct((g, bs, p), F32), jax.ShapeDtypeStruct((g, bs, p), F32)],
        scratch_shapes=[pltpu.VMEM((gs, r, tc), F32)] + [pltpu.VMEM((2, r, p), F32) for _ in range(4)],
        compiler_params=_cparams(("arbitrary",)), name="s5_main")(
            u, m, sbr, sbi, car, cai, a16r, a16i, h0r, h0i, d_t)
    states = tuple(jnp.swapaxes(h, 0, 1)[:, None] for h in (hpr, hpi, hsr, hsi))
    return z, states


def _glu_kernel(x_ref, z_ref, wa_ref, wb_ref, o_ref):
    z = z_ref[...].astype(BF16)
    o_ref[...] = x_ref[...] + _dot(z, wa_ref[...]) * jax.nn.sigmoid(_dot(z, wb_ref[...]))


def _glu(x, z, wa, wb):
    n, d = x.shape
    tm = min(TOK_TM, n)
    row = pl.BlockSpec((tm, d), lambda i: (i, 0))
    mat = pl.BlockSpec((d, d), lambda i: (0, 0))
    return pl.pallas_call(
        _glu_kernel, grid=(n // tm,), in_specs=[row, row, mat, mat], out_specs=row,
        out_shape=jax.ShapeDtypeStruct((n, d), F32),
        compiler_params=_cparams(("parallel",)), name="glu")(x, z, wa, wb)


def _head_norm(x, e_ref, et_ref):
    ms = _dot((x * x).astype(BF16), e_ref[...]) * (1.0 / HEAD_DIM)
    hi, lo = _split2(lax.rsqrt(ms + EPS))
    return _dot(hi, et_ref[...]) + _dot(lo, et_ref[...])


def _head_indicator(d):
    nh = d // HEAD_DIM
    e = (np.arange(d)[:, None] // HEAD_DIM == np.arange(LANES)[None, :]).astype(np.float32)
    assert nh <= LANES
    return jnp.asarray(e, BF16), jnp.asarray(e.T, BF16)


def _kvf_kernel(h_ref, wkv_ref, wf_ref, bf_ref, kn_ref, e_ref, et_ref,
                kp_ref, ks_ref, vp_ref, vs_ref, lp_ref, ls_ref, kb_ref, vb_ref, *, d, nh, npb):
    i = pl.program_id(0)
    h = h_ref[...]
    tm = h.shape[0]
    pkv = _dot(h, wkv_ref[...])
    kraw, v = pkv[:, :d], pkv[:, d:]
    k = kraw * _head_norm(kraw, e_ref, et_ref) * kn_ref[...]
    kb_ref[...] = k.astype(BF16)
    vb_ref[...] = v.astype(BF16)
    pf = _dot(h, wf_ref[...]) + bf_ref[...]
    lft = (jnp.minimum(pf, 0.0) - jnp.log1p(jnp.exp(-jnp.abs(pf)))).T[:nh]

    @pl.when(i < npb)
    def _():
        kp_ref[0] = k.T
        vp_ref[0] = v.T
        lp_ref[0] = lft

    @pl.when(i >= npb)
    def _():
        for hd in range(nh):
            rows = pl.ds(hd, tm, stride=nh)
            ks_ref[rows, :] = k[:, hd * HEAD_DIM:(hd + 1) * HEAD_DIM]
            vs_ref[rows, :] = v[:, hd * HEAD_DIM:(hd + 1) * HEAD_DIM]
        lq = ls_ref.shape[2]
        for j in range(tm // lq):
            ls_ref[j] = lft[:, j * lq:(j + 1) * lq]


def _kvf(hn, bp, s, lq, w_kv, w_f, b_f, k_norm):
    n, d = hn.shape
    nh = d // HEAD_DIM
    tm = min(TOK_TM, s)
    n_prompt = bp * s
    n_s = n - n_prompt
    assert s % tm == 0 and n_s % tm == 0 and tm % lq == 0
    npb, spb = n_prompt // tm, s // tm
    e, et = _head_indicator(d)
    row = pl.BlockSpec((tm, d), lambda i: (i, 0))
    const = lambda *sh: pl.BlockSpec(sh, lambda i: (0, 0))
    prompt_t = lambda rows: pl.BlockSpec(
        (1, rows, tm), lambda i: (jnp.minimum(i, npb - 1) // spb, 0, jnp.minimum(i, npb - 1) % spb))
    second = lambda i: (jnp.maximum(i - npb, 0), 0)
    kern = functools.partial(_kvf_kernel, d=d, nh=nh, npb=npb)
    return pl.pallas_call(
        kern, grid=(n // tm,),
        in_specs=[row, const(d, 2 * d), const(d, LANES), const(1, LANES), const(1, d), const(d, LANES), const(LANES, d)],
        out_specs=[prompt_t(d), pl.BlockSpec((tm * nh, HEAD_DIM), second),
                   prompt_t(d), pl.BlockSpec((tm * nh, HEAD_DIM), second),
                   prompt_t(nh), pl.BlockSpec((tm // lq, nh, lq), lambda i: (jnp.maximum(i - npb, 0), 0, 0)), row, row],
        out_shape=[jax.ShapeDtypeStruct((bp, d, s), F32), jax.ShapeDtypeStruct((n_s * nh, HEAD_DIM), F32),
                   jax.ShapeDtypeStruct((bp, d, s), F32), jax.ShapeDtypeStruct((n_s * nh, HEAD_DIM), F32),
                   jax.ShapeDtypeStruct((bp, nh, s), F32), jax.ShapeDtypeStruct((n_s // lq, nh, lq), F32),
                   jax.ShapeDtypeStruct((n, d), BF16), jax.ShapeDtypeStruct((n, d), BF16)],
        compiler_params=_cparams(("arbitrary",)), name="kvf")(
            hn, w_kv, jnp.pad(w_f, ((0, 0), (0, LANES - nh))), jnp.pad(b_f, (0, LANES - nh)).reshape(1, LANES),
            jnp.tile(k_norm, nh).reshape(1, d), e, et)


def _qproj_kernel(u_ref, wq_ref, qn_ref, e_ref, et_ref, q_ref):
    q = _dot(u_ref[...], wq_ref[...])
    q_ref[...] = (q * _head_norm(q, e_ref, et_ref) * qn_ref[...]).astype(BF16)


def _qproj(un, wq, qn_scaled):
    n, d = un.shape
    tm = min(TOK_TM, n)
    e, et = _head_indicator(d)
    row = pl.BlockSpec((tm, d), lambda i: (i, 0))
    const = lambda *s: pl.BlockSpec(s, lambda i: (0, 0))
    return pl.pallas_call(
        _qproj_kernel, grid=(n // tm,),
        in_specs=[row, const(d, d), const(1, d), const(d, LANES), const(LANES, d)], out_specs=row,
        out_shape=jax.ShapeDtypeStruct((n, d), BF16),
        compiler_params=_cparams(("parallel",)), name="qproj")(un, wq, qn_scaled, e, et)


def _oproj_kernel(x_ref, ap_ref, as_ref, wo_ref, o_ref, *, npb):
    a = jnp.where(pl.program_id(0) < npb, ap_ref[...], as_ref[...])
    o_ref[...] = x_ref[...] + _dot(a, wo_ref[...])


def _oproj(x, a_p, a_s, wo):
    n, d = x.shape
    tm = min(TOK_TM, n)
    assert a_p.shape[0] % tm == 0 and a_s.shape[0] % tm == 0
    npb = a_p.shape[0] // tm
    first, second = _part_maps(npb)
    row = pl.BlockSpec((tm, d), lambda i: (i, 0))
    return pl.pallas_call(
        functools.partial(_oproj_kernel, npb=npb), grid=(n // tm,),
        in_specs=[row, pl.BlockSpec((tm, d), first), pl.BlockSpec((tm, d), second),
                  pl.BlockSpec((d, d), lambda i: (0, 0))],
        out_specs=row, out_shape=jax.ShapeDtypeStruct((n, d), F32),
        compiler_params=_cparams(("parallel",)), name="oproj")(x, a_p, a_s, wo)


def _cumsum_kernel(x_ref, init_ref, o_ref, *, nblk, bw):
    tri = (lax.broadcasted_iota(jnp.int32, (bw, bw), 0) <= lax.broadcasted_iota(jnp.int32, (bw, bw), 1)).astype(BF16)
    carry = init_ref[...]
    for j in range(nblk):
        x = x_ref[:, j * bw:(j + 1) * bw]
        hi, mid, lo = _split3(x)
        cs = _dot(hi, tri) + _dot(mid, tri) + _dot(lo, tri) + carry
        o_ref[:, j * bw:(j + 1) * bw] = cs
        carry = cs[:, bw - 1:bw]


def _cumsum_rows(x, init):
    r, w = x.shape
    bw = min(256, w)
    kern = functools.partial(_cumsum_kernel, nblk=w // bw, bw=bw)
    return pl.pallas_call(
        kern, grid=(1,),
        in_specs=[pl.BlockSpec((r, w), lambda i: (0, 0)), pl.BlockSpec((r, 1), lambda i: (0, 0))],
        out_specs=pl.BlockSpec((r, w), lambda i: (0, 0)), out_shape=jax.ShapeDtypeStruct((r, w), F32),
        compiler_params=_cparams(("arbitrary",)), name="cumsum")(x, init)


def _lanes(x, w):
    if w <= LANES:
        return x[:, :w]
    return jnp.concatenate([x] * (w // LANES), axis=1)


def _head_masks(shape):
    head = lax.broadcasted_iota(jnp.int32, shape, 1) // HEAD_DIM
    return [head == i for i in range(shape[1] // HEAD_DIM)]


def _stack_heads(qs):
    zero = jnp.zeros_like(qs)
    return jnp.concatenate([jnp.where(mk, qs, zero) for mk in _head_masks(qs.shape)], axis=0)


def _flash_init(cq, hs, m_scr, acc_scr, cq_scr):
    tq = cq.shape[0]
    m_scr[...] = jnp.full_like(m_scr, NEG_BIG)
    acc_scr[...] = jnp.zeros_like(acc_scr)
    for g in range(cq_scr.shape[0]):
        for i in range(hs):
            h = g * hs + i
            cq_scr[g, i * tq:(i + 1) * tq, :] = jnp.broadcast_to(cq[:, h:h + 1], (tq, LANES))


def _flash_slab(g, q_st, ks, vs, ck_rows, causal, m_scr, acc_scr, cq_scr, kv_t=False):
    hs = len(ck_rows)
    tq, tk = q_st.shape[0] // hs, ck_rows[0].shape[1]
    nt = (((1,), (1,)), ((), ()))
    s = _dot(q_st, ks) if kv_t else lax.dot_general(q_st, ks, nt, preferred_element_type=F32)
    t = jnp.concatenate([s[i * tq:(i + 1) * tq] - ck_rows[i] for i in range(hs)], axis=0)
    if causal is not None:
        keep = lax.broadcasted_iota(jnp.int32, (tq, tk), 1) <= lax.broadcasted_iota(jnp.int32, (tq, tk), 0) + causal
        t = jnp.where(jnp.concatenate([keep] * hs, axis=0), t, NEG_BIG)
    cq = cq_scr[g]
    m_prev = m_scr[g]
    m_new = jnp.maximum(m_prev, jnp.max(t, axis=1, keepdims=True) + cq)
    alpha = jnp.exp(m_prev - m_new)
    pe = jnp.exp(t - _lanes(m_new - cq, tk))
    w = hs * HEAD_DIM
    if w == LANES and not kv_t:
        pv = _dot(pe.astype(BF16), jnp.concatenate([vs, jnp.ones((tk, LANES), BF16)], axis=1))
    else:
        pv = lax.dot_general(pe.astype(BF16), vs, nt, preferred_element_type=F32) if kv_t else _dot(pe.astype(BF16), vs)
        pv = jnp.concatenate([pv, jnp.broadcast_to(jnp.sum(pe, axis=1, keepdims=True), (hs * tq, LANES))], axis=1)
    acc_scr[g] = _lanes(alpha, w + LANES) * acc_scr[g] + pv
    m_scr[g] = m_new


def _flash_finish(o_ref, hs, acc_scr):
    tq = o_ref.shape[0]
    w = hs * HEAD_DIM
    masks = _head_masks((tq, w))
    for g in range(acc_scr.shape[0]):
        acc = acc_scr[g]
        o = acc[:, :w] / _lanes(acc[:, w:], w)
        out = o[(hs - 1) * tq:]
        for i in range(hs - 2, -1, -1):
            out = jnp.where(masks[i], o[i * tq:(i + 1) * tq], out)
        o_ref[:, g * w:(g + 1) * w] = out.astype(o_ref.dtype)


def _flash_scratch(ng, rows, w):
    return [pltpu.VMEM((ng, rows, LANES), F32), pltpu.VMEM((ng, rows, w + LANES), F32),
            pltpu.VMEM((ng, rows, LANES), F32)]


def _attn_prompt_kernel(qi_tab, ki_tab, q_ref, k_ref, v_ref, cq_ref, ck_ref, o_ref, *scr, hs):
    step = pl.program_id(1)
    qi, ki = qi_tab[step], ki_tab[step]
    tq, tk = q_ref.shape[0], k_ref.shape[0]
    w = hs * HEAD_DIM
    ng = q_ref.shape[1] // w
    last = (qi * tq + tq - 1) // tk

    @pl.when(ki == 0)
    def _():
        _flash_init(cq_ref[...], hs, *scr)

    def sweep(causal, nkeys):
        ck = ck_ref[0]
        for g in range(ng):
            sl = slice(g * w, (g + 1) * w)
            _flash_slab(g, _stack_heads(q_ref[:, sl]), k_ref[:nkeys, sl], v_ref[:nkeys, sl],
                        [ck[g * hs + i:g * hs + i + 1, :nkeys] for i in range(hs)], causal, *scr)

    @pl.when(ki < last)
    def _():
        sweep(None, tk)

    for off in range(0, tk, tq):
        @pl.when((ki == last) & (qi * tq - ki * tk == off))
        def _():
            sweep(off, off + tq)
            _flash_finish(o_ref, hs, scr[1])


def _attn_prompt(q, kb, vb, c_rows, c_t, nb, s):
    d = q.shape[1]
    nh = d // HEAD_DIM
    hs = ATT_HS
    tq, tk = min(ATT_TQ, s), min(ATT_TK, s)
    assert tk % tq == 0 and s % tk == 0
    nq, nk = s // tq, s // tk
    pairs = [(i, j) for i in range(nq) for j in range((i * tq + tq - 1) // tk + 1)]
    qi_tab = jnp.asarray([p[0] for p in pairs], jnp.int32)
    ki_tab = jnp.asarray([p[1] for p in pairs], jnp.int32)
    grid_spec = pltpu.PrefetchScalarGridSpec(
        num_scalar_prefetch=2, grid=(nb, len(pairs)),
        in_specs=[pl.BlockSpec((tq, d), lambda b, p, qt, kt: (b * nq + qt[p], 0)),
                  pl.BlockSpec((tk, d), lambda b, p, qt, kt: (b * nk + kt[p], 0)),
                  pl.BlockSpec((tk, d), lambda b, p, qt, kt: (b * nk + kt[p], 0)),
                  pl.BlockSpec((tq, nh), lambda b, p, qt, kt: (b * nq + qt[p], 0)),
                  pl.BlockSpec((1, nh, tk), lambda b, p, qt, kt: (b, 0, kt[p]))],
        out_specs=pl.BlockSpec((tq, d), lambda b, p, qt, kt: (b * nq + qt[p], 0)),
        scratch_shapes=_flash_scratch(nh // hs, hs * tq, hs * HEAD_DIM))
    return pl.pallas_call(
        functools.partial(_attn_prompt_kernel, hs=hs), grid_spec=grid_spec,
        out_shape=jax.ShapeDtypeStruct((nb * s, d), BF16),
        compiler_params=_cparams(("parallel", "arbitrary")), name="attn_prompt")(
            qi_tab, ki_tab, q, kb, vb, c_rows, c_t)


def _attn_sample_kernel(q_ref, ck_ref, cv_ref, kn_ref, vn_ref, cq_ref, ckc_ref, ckn_ref, o_ref, *scr, hs, nkb):
    j = pl.program_id(1)
    w = hs * HEAD_DIM
    ng = q_ref.shape[1] // w

    @pl.when(j == 0)
    def _():
        _flash_init(cq_ref[0], hs, *scr)

    def sweep(kslab, vslab, ck, causal, kv_t):
        for g in range(ng):
            _flash_slab(g, _stack_heads(q_ref[:, g * w:(g + 1) * w]), kslab(g), vslab(g),
                        [ck[g * hs + i:g * hs + i + 1, :] for i in range(hs)], causal, *scr, kv_t=kv_t)

    @pl.when(j < nkb)
    def _():
        sweep(lambda g: ck_ref[0, g * w:(g + 1) * w, :].astype(BF16),
              lambda g: cv_ref[0, g * w:(g + 1) * w, :].astype(BF16), ckc_ref[0], None, True)

    @pl.when(j == nkb)
    def _():
        sweep(lambda g: kn_ref[:, g * w:(g + 1) * w], lambda g: vn_ref[:, g * w:(g + 1) * w], ckn_ref[0], 0, False)
        _flash_finish(o_ref, hs, scr[1])


def _attn_sample(q, kb, vb, cache_kt, cache_vt, c_new, c_past_t, c_new_t, n_prompt):
    d = q.shape[1]
    nh = d // HEAD_DIM
    nb, _, plen = cache_kt.shape
    lq = c_new.shape[1]
    hs = DEC_HS
    tk = min(DEC_TK, plen)
    nkb = plen // tk
    r0 = n_prompt // lq
    new_rows = pl.BlockSpec((lq, d), lambda b, j: (r0 + b, 0))
    cache = pl.BlockSpec((1, d, tk), lambda b, j: (b, 0, jnp.minimum(j, nkb - 1)))
    kern = functools.partial(_attn_sample_kernel, hs=hs, nkb=nkb)
    return pl.pallas_call(
        kern, grid=(nb, nkb + 1),
        in_specs=[new_rows, cache, cache, new_rows, new_rows,
                  pl.BlockSpec((1, lq, nh), lambda b, j: (b, 0, 0)),
                  pl.BlockSpec((1, nh, tk), lambda b, j: (b, 0, jnp.minimum(j, nkb - 1))),
                  pl.BlockSpec((1, nh, lq), lambda b, j: (b, 0, 0))],
        out_specs=pl.BlockSpec((lq, d), lambda b, j: (b, 0)),
        out_shape=jax.ShapeDtypeStruct((nb * lq, d), BF16),
        scratch_shapes=_flash_scratch(nh // hs, hs * lq, hs * HEAD_DIM),
        compiler_params=_cparams(("parallel", "arbitrary")), name="attn_sample")(
            q, cache_kt, cache_vt, kb, vb, c_new, c_past_t, c_new_t)


def kernel(x_prompt, x_sample, cache_k, cache_v, cache_logf, state_ssm_re, state_ssm_im, ffn_norm, w_ffn_gate, w_ffn_up, w_ffn_down, mix_norm, ssm_a_re, ssm_a_im, ssm_log_dt, ssm_b_re, ssm_b_im, ssm_c_re, ssm_c_im, ssm_d, w_glu_a, w_glu_b, kv_norm, w_kvf, b_f, k_norm, w_q, q_norm, w_o):
    bp, s, d = x_prompt.shape
    bs, lq, _ = x_sample.shape
    plen = cache_k.shape[1]
    nh = d // HEAD_DIM
    assert ffn_norm.shape[0] == 2 and ssm_a_re.shape[0] == 1 and w_q.shape[0] == 1
    n_prompt, n_sample = bp * s, bs * lq
    bf = lambda w: w.astype(BF16)

    def ffn(xs, l, j, **kw):
        return _ffn(xs, ffn_norm[l, j], w_ffn_gate, w_ffn_up, w_ffn_down, (l, j), **kw)

    x, u = ffn([x_prompt.reshape(n_prompt, d), x_sample.reshape(n_sample, d)], 0, 0, g2=mix_norm[0], norm_dtype=F32)
    mats = _s5_params(ssm_log_dt[0], ssm_a_re[0], ssm_a_im[0], ssm_b_re[0], ssm_b_im[0], ssm_c_re[0], ssm_c_im[0])
    z, (p_re, p_im, s_re, s_im) = _s5_layer(u, n_prompt, bp, bs, state_ssm_re[:, 0], state_ssm_im[:, 0], mats, ssm_d[0])
    x = _glu(x, z, bf(w_glu_a[0]), bf(w_glu_b[0]))
    x, hn = ffn([x], 0, 1, g2=kv_norm)

    kt_p, k_s, vt_p, v_s, lft_p, lft_s, kb, vb = _kvf(hn, bp, s, lq, bf(w_kvf[:, :2 * d]), bf(w_kvf[:, 2 * d:]), b_f, k_norm)

    def cumsum_t(lft, init):
        b, _, l = lft.shape
        init = jnp.zeros((b * nh, 1), F32) if init is None else init.reshape(b * nh, 1)
        return _cumsum_rows(lft.reshape(b * nh, l), init).reshape(b, nh, l)

    cp_t = cumsum_t(lft_p, None)
    cpast_t = cumsum_t(jnp.swapaxes(cache_logf.astype(F32), 1, 2), None)
    cnew_t = cumsum_t(lft_s, cpast_t[:, :, -1])

    x, un = ffn([x], 1, 0, g2=mix_norm[1])
    q = _qproj(un, bf(w_q[0]), (jnp.tile(q_norm[0], nh) * (1.0 / math.sqrt(HEAD_DIM))).reshape(1, d))
    a_p = _attn_prompt(q, kb, vb, jnp.swapaxes(cp_t, 1, 2).reshape(n_prompt, nh), cp_t, bp, s)
    pos_minor = lambda c: jnp.transpose(c, (0, 2, 3, 1)).reshape(bs, d, plen)
    a_s = _attn_sample(q, kb, vb, pos_minor(cache_k), pos_minor(cache_v),
                       jnp.swapaxes(cnew_t, 1, 2), cpast_t, cnew_t, n_prompt)
    x = _oproj(x, a_p, a_s, bf(w_o[0]))
    (y_p, y_s), _ = ffn([x], 1, 1, split_out=(n_prompt, n_sample))

    cache_p = lambda a: jnp.transpose(a.reshape(bp, nh, HEAD_DIM, s), (0, 3, 1, 2))
    cache_s = lambda a: a.reshape(bs, lq, nh, HEAD_DIM)
    return (y_p.reshape(bp, s, d), y_s.reshape(bs, lq, d), p_re, p_im, cache_p(kt_p), cache_p(vt_p),
            jnp.swapaxes(lft_p, 1, 2), s_re, s_im, cache_s(k_s), cache_s(v_s), jnp.swapaxes(lft_s, 1, 2))
```

```python
import functools
import math

import jax
import jax.numpy as jnp
import numpy as np
from jax import lax
from jax.experimental import pallas as pl
from jax.experimental.pallas import tpu as pltpu

F32 = jnp.float32
BF16 = jnp.bfloat16

EPS = 1e-6
HEAD_DIM = 64
SSM_GROUP = 16
S5_CHUNK = 16
LANES = 128
_SLAB_GROUPS = LANES // SSM_GROUP
NEG_BIG = -1e30
LOG2E = math.log2(math.e)
VMEM_LIMIT = 57 * 1024 * 1024

FFN_TM = 1024
FFN_TF = 512
TOK_TM = 512
MAT_TM = 1024
ATT_TQ = 256
ATT_TK = 512
ATT_HS = 2
DEC_TK = 1024
DEC_HS = 4


def _cparams(sem):
    return pltpu.CompilerParams(dimension_semantics=sem, vmem_limit_bytes=VMEM_LIMIT)


def _rms_scale(x):
    return lax.rsqrt(jnp.mean(x * x, axis=-1, keepdims=True) + EPS)


def _dot(a, b):
    return jnp.dot(a, b, preferred_element_type=F32)


def _split2(x):
    hi = x.astype(BF16)
    lo = (x - hi.astype(F32)).astype(BF16)
    return hi, lo


def _split3(x):
    hi = x.astype(BF16)
    r1 = x - hi.astype(F32)
    mid = r1.astype(BF16)
    return hi, mid, (r1 - mid.astype(F32)).astype(BF16)


def _tile(limit, *sizes):
    while any(s % limit for s in sizes):
        limit //= 2
    return limit


def _part_maps(npb):
    first = lambda i, *_: (jnp.minimum(i, npb - 1), 0)
    second = lambda i, *_: (jnp.maximum(i - npb, 0), 0)
    return first, second


def _ffn_kernel(*refs, nf, npb, n_in, n_out, with_norm):
    x_refs, refs = refs[:n_in], refs[n_in:]
    g_ref, wg_ref, wu_ref, wd_ref = refs[:4]
    refs = refs[4:]
    g2_ref = refs[0] if with_norm else None
    refs = refs[with_norm:]
    o_refs, refs = refs[:n_out], refs[n_out:]
    o2_ref = refs[0] if with_norm else None
    h_scr, acc_scr = refs[with_norm:]
    i, f = pl.program_id(0), pl.program_id(1)

    def with_x(cond, fn):
        if n_in == 1:
            pl.when(cond)(lambda: fn(x_refs[0][...]))
        else:
            pl.when(cond & (i < npb))(lambda: fn(x_refs[0][...]))
            pl.when(cond & (i >= npb))(lambda: fn(x_refs[1][...]))

    def start(x):
        h_scr[...] = (x * _rms_scale(x) * g_ref[...]).astype(BF16)
        acc_scr[...] = jnp.zeros_like(acc_scr)

    with_x(f == 0, start)

    h = h_scr[...]
    a = _dot(h, wg_ref[...].astype(BF16))
    b = _dot(h, wu_ref[...].astype(BF16))
    t = (a * jax.nn.sigmoid(a)) * b
    acc_scr[...] += _dot(t.astype(BF16), wd_ref[...].astype(BF16))

    def finish(x):
        y = x + 0.5 * acc_scr[...]
        if n_out == 1:
            o_refs[0][...] = y
        else:
            @pl.when(i < npb)
            def _():
                o_refs[0][...] = y

            @pl.when(i >= npb)
            def _():
                o_refs[1][...] = y
        if with_norm:
            o2_ref[...] = (y * _rms_scale(y) * g2_ref[...]).astype(o2_ref.dtype)

    with_x(f == nf - 1, finish)


def _ffn(xs, g, wg, wu, wd, lj, g2=None, split_out=None, norm_dtype=BF16):
    d = xs[0].shape[1]
    n = sum(x.shape[0] for x in xs)
    dff = wg.shape[-1]
    l, j = lj
    tm, tf = min(FFN_TM, n), min(FFN_TF, dff)
    nf = dff // tf
    n_first = xs[0].shape[0] if len(xs) == 2 else (split_out[0] if split_out else n)
    assert all(x.shape[0] % tm == 0 for x in xs) and n_first % tm == 0
    npb = n_first // tm
    first, second = _part_maps(npb)
    with_norm = g2 is not None
    row = pl.BlockSpec((tm, d), lambda i, f: (i, 0))
    vec = pl.BlockSpec((1, d), lambda i, f: (0, 0))
    parts = [pl.BlockSpec((tm, d), first), pl.BlockSpec((tm, d), second)]
    in_specs = (parts if len(xs) == 2 else [row]) + [
        vec, pl.BlockSpec((None, None, d, tf), lambda i, f: (l, j, 0, f)),
        pl.BlockSpec((None, None, d, tf), lambda i, f: (l, j, 0, f)),
        pl.BlockSpec((None, None, tf, d), lambda i, f: (l, j, f, 0))]
    args = list(xs) + [g.reshape(1, d), wg, wu, wd]
    if split_out:
        out_shape = [jax.ShapeDtypeStruct((m, d), F32) for m in split_out]
        out_specs = list(parts)
    else:
        out_shape = [jax.ShapeDtypeStruct((n, d), F32)]
        out_specs = [row]
    n_out = len(out_shape)
    if with_norm:
        in_specs.append(vec)
        args.append(g2.reshape(1, d))
        out_shape.append(jax.ShapeDtypeStruct((n, d), norm_dtype))
        out_specs.append(row)
    kern = functools.partial(_ffn_kernel, nf=nf, npb=npb, n_in=len(xs), n_out=n_out, with_norm=with_norm)
    res = pl.pallas_call(
        kern, grid=(n // tm, nf), in_specs=in_specs, out_specs=out_specs, out_shape=out_shape,
        scratch_shapes=[pltpu.VMEM((tm, d), BF16), pltpu.VMEM((tm, d), F32)],
        compiler_params=_cparams(("arbitrary", "arbitrary")), name="ffn")(*args)
    y = tuple(res[:n_out]) if split_out else res[0]
    return y, (res[n_out] if with_norm else None)


def _s5_param_kernel(*refs):
    lax.fori_loop(0, _SLAB_GROUPS, functools.partial(_s5_param_group, refs), 0)


def _s5_param_group(refs, slot, carry):
    (ldt_ref, arc_ref, aic_ref, arr_ref, air_ref, btr_ref, bti_ref, ctr_ref, cti_ref,
     m_ref, sbr_ref, sbi_ref, car_ref, cai_ref, a16r_ref, a16i_ref) = refs
    t, c = S5_CHUNK, SSM_GROUP
    dt = jnp.exp(ldt_ref[slot])

    def cpow(ar, ai, n):
        mag = jnp.exp(ar * dt)
        pr, pi = mag * jnp.cos(ai * dt), mag * jnp.sin(ai * dt)
        n = jnp.asarray(n, jnp.int32)
        shape = jnp.broadcast_shapes(ar.shape, n.shape)
        re, im = jnp.ones(shape, F32), jnp.zeros(shape, F32)
        for b in range(t.bit_length()):
            bit = ((n >> b) & 1) == 1
            re, im = jnp.where(bit, re * pr - im * pi, re), jnp.where(bit, re * pi + im * pr, im)
            pr, pi = pr * pr - pi * pi, 2.0 * pr * pi
        return re, im

    def step_of(pos):
        seg = pos // c
        return (seg // _SLAB_GROUPS) * _SLAB_GROUPS + ((seg - slot) & (_SLAB_GROUPS - 1))

    arc, aic = arc_ref[slot], aic_ref[slot]
    arr, air = arr_ref[slot], air_ref[slot]
    p = arc.shape[0]
    lane_pc = lax.broadcasted_iota(jnp.int32, (p, t * c), 1)
    cr, ci = ctr_ref[slot], cti_ref[slot]
    pr, pi = cpow(arc, aic, step_of(lane_pc) + 1)
    car_ref[slot] = (cr * pr - ci * pi).astype(BF16)
    cai_ref[slot] = (-(cr * pi + ci * pr)).astype(BF16)
    pr, pi = cpow(arc, aic, lane_pc // c)
    csr = cr * pr - ci * pi
    csi = cr * pi + ci * pr

    abr, abi = cpow(arr, air, 1)
    xr, xi = abr - 1.0, abi
    den = arr * arr + air * air
    qr = (xr * arr + xi * air) / den
    qi = (xi * arr - xr * air) / den
    btr, bti = btr_ref[slot], bti_ref[slot]
    bbr = qr * btr - qi * bti
    bbi = qr * bti + qi * btr
    iidx = step_of(lax.broadcasted_iota(jnp.int32, (t * c, p), 0))
    pr, pi = cpow(arr, air, (t - 1) - iidx)
    sbr_ref[slot] = (bbr * pr - bbi * pi).astype(BF16)
    sbi_ref[slot] = (bbr * pi + bbi * pr).astype(BF16)
    pr, pi = cpow(arr, air, t)
    a16r_ref[slot] = pr
    a16i_ref[slot] = pi

    hp = lax.Precision.HIGHEST
    kt = (jnp.dot(bbr[:c], csr, precision=hp, preferred_element_type=F32)
          - jnp.dot(bbi[:c], csi, precision=hp, preferred_element_type=F32))
    lane = lax.broadcasted_iota(jnp.int32, (c, t * c), 1)
    for i in range(t):
        blk = kt if i == 0 else pltpu.roll(kt, i * c, 1)
        blk = jnp.where(lane >= i * c, blk, 0.0)
        blk = jnp.concatenate([pltpu.roll(blk[:, h * LANES:(h + 1) * LANES], slot * c, 1)
                               for h in range(t * c // LANES)], axis=1)
        seg = (i // _SLAB_GROUPS) * _SLAB_GROUPS + ((i + slot) & (_SLAB_GROUPS - 1))
        m_ref[slot, pl.ds(pl.multiple_of(seg * c, c), c), :] = blk.astype(BF16)
    return carry


def _s5_params(log_dt, a_re, a_im, b_re, b_im, c_re, c_im):
    g, p = a_re.shape
    t, c = S5_CHUNK, SSM_GROUP
    tc = t * c

    def tile_b(b):
        return jnp.broadcast_to(jnp.swapaxes(b, 1, 2)[:, None], (g, t, c, p)).reshape(g, tc, p)

    def tile_c(cm):
        return jnp.broadcast_to(jnp.swapaxes(cm, 1, 2)[:, :, None], (g, p, t, c)).reshape(g, p, tc)

    def spec(*shape):
        return pl.BlockSpec((_SLAB_GROUPS,) + shape, lambda i: (i, 0, 0))

    assert g % _SLAB_GROUPS == 0
    return pl.pallas_call(
        _s5_param_kernel, grid=(g // _SLAB_GROUPS,),
        in_specs=[spec(1, 1), spec(p, 1), spec(p, 1), spec(1, p), spec(1, p),
                  spec(tc, p), spec(tc, p), spec(p, tc), spec(p, tc)],
        out_specs=[spec(tc, tc), spec(tc, p), spec(tc, p), spec(p, tc), spec(p, tc), spec(1, p), spec(1, p)],
        out_shape=[jax.ShapeDtypeStruct((g, tc, tc), BF16),
                   jax.ShapeDtypeStruct((g, tc, p), BF16), jax.ShapeDtypeStruct((g, tc, p), BF16),
                   jax.ShapeDtypeStruct((g, p, tc), BF16), jax.ShapeDtypeStruct((g, p, tc), BF16),
                   jax.ShapeDtypeStruct((g, 1, p), F32), jax.ShapeDtypeStruct((g, 1, p), F32)],
        compiler_params=_cparams(("parallel",)), name="s5_params")(
            log_dt.reshape(g, 1, 1), a_re.reshape(g, p, 1), a_im.reshape(g, p, 1),
            a_re.reshape(g, 1, p), a_im.reshape(g, 1, p),
            tile_b(b_re), tile_b(b_im), tile_c(c_re), tile_c(c_im))


def _gelu_tanh(y):
    return 0.5 * y * (1.0 + jnp.tanh(math.sqrt(2.0 / math.pi) * (y + 0.044715 * (y * y * y))))


def _pick_segments(v, first):
    seg = lax.broadcasted_iota(jnp.int32, v[0].shape, 1) // SSM_GROUP
    out = v[(7 + first) % 8]
    for s in range(6, -1, -1):
        out = jnp.where(seg == s, v[(s + first) % 8], out)
    return out


def _s5_main_kernel(u_ref, m_ref, sbr_ref, sbi_ref, car_ref, cai_ref, a16r_ref, a16i_ref,
                    h0r_ref, h0i_ref, d_ref, z_ref, hpr_ref, hpi_ref, hsr_ref, hsi_ref,
                    u2_scr, sr_scr, si_scr, hr_scr, hi_scr, *, bp, nkp, bs, nks):
    t = S5_CHUNK
    npq = bp * nkp // 8
    nq = npq + bs * nks // 8
    sp = nkp * t

    def token0(q):
        return jnp.where(q < npq, (q % bp) * sp + (q // bp) * (8 * t), bp * sp + (q - npq) * (8 * t))

    def gather(q, carry):
        tok, rows = token0(q), pl.ds(pl.multiple_of(q * 8, 8), 8)
        for h in range(2):
            v = [u_ref[pl.ds(tok + 8 * h + i, 8, stride=t), :] for i in range(8)]
            v = [x if i == 0 else pltpu.roll(x, i * SSM_GROUP, 1) for i, x in enumerate(v)]
            for gi in range(8):
                u2_scr[gi, rows, h * LANES:(h + 1) * LANES] = _pick_segments(v, 8 - gi)
        return carry

    lax.fori_loop(0, nq, gather, 0, unroll=4)

    def group(gi, buf):
        sr, si, hrs, his = sr_scr.at[buf], si_scr.at[buf], hr_scr.at[buf], hi_scr.at[buf]
        u = u2_scr[gi]
        ub = u.astype(BF16)
        sr[...] = _dot(ub, sbr_ref[gi])
        si[...] = _dot(ub, sbi_ref[gi])
        ar, ai = a16r_ref[gi], a16i_ref[gi]

        def step(rows, hr, hi):
            hrs[rows, :] = hr
            his[rows, :] = hi
            return (ar * hr - ai * hi + sr[rows, :], ar * hi + ai * hr + si[rows, :])

        hr = hi = jnp.zeros((bp, ar.shape[1]), F32)
        for k in range(nkp):
            hr, hi = step(pl.ds((k // 8) * bp * 8 + k % 8, bp, stride=8), hr, hi)
        hpr_ref[gi] = hr
        hpi_ref[gi] = hi
        hr, hi = h0r_ref[gi], h0i_ref[gi]
        for k in range(nks):
            hr, hi = step(pl.ds(bp * nkp + k, bs, stride=nks), hr, hi)
        hsr_ref[gi] = hr
        hsi_ref[gi] = hi

        y = (_dot(ub, m_ref[gi]) + _dot(hrs[...].astype(BF16), car_ref[gi])
             + _dot(his[...].astype(BF16), cai_ref[gi]) + d_ref[gi] * u)
        u2_scr[gi] = _gelu_tanh(y)

    def group_pair(j, carry):
        group(2 * j, 0)
        group(2 * j + 1, 1)
        return carry

    lax.fori_loop(0, _SLAB_GROUPS // 2, group_pair, 0)

    def scatter(q, carry):
        tok, rows = token0(q), pl.ds(pl.multiple_of(q * 8, 8), 8)
        for h in range(2):
            v = [u2_scr[gi, rows, h * LANES:(h + 1) * LANES] for gi in range(8)]
            for i in range(8):
                w = _pick_segments(v, 8 - i)
                w = w if i == 0 else pltpu.roll(w, LANES - i * SSM_GROUP, 1)
                z_ref[pl.ds(tok + 8 * h + i, 8, stride=t), :] = w
        return carry

    lax.fori_loop(0, nq, scatter, 0, unroll=4)


def _s5_layer(u, n_prompt, bp, bs, h0_re, h0_im, mats, d_skip):
    n, d = u.shape
    t, c = S5_CHUNK, SSM_GROUP
    g, tc = d // c, t * c
    gs = LANES // c
    p = h0_re.shape[-1]
    nkp = n_prompt // (bp * t)
    nks = (n - n_prompt) // (bs * t)
    r = nkp * bp + nks * bs
    assert gs == 8 and t == 16 and bp % 8 == 0 and nkp % 8 == 0 and (bs * nks) % 8 == 0

    m, sbr, sbi, car, cai, a16r, a16i = mats
    d_t = jnp.broadcast_to(d_skip.reshape(g, 1, 1, c), (g, 1, t, c)).reshape(g, 1, tc)
    h0r = jnp.swapaxes(h0_re, 0, 1)
    h0i = jnp.swapaxes(h0_im, 0, 1)

    def spec(*shape):
        return pl.BlockSpec((gs,) + shape, lambda i: (i, 0, 0))

    kern = functools.partial(_s5_main_kernel, bp=bp, nkp=nkp, bs=bs, nks=nks)
    z, hpr, hpi, hsr, hsi = pl.pallas_call(
        kern, grid=(g // gs,),
        in_specs=[pl.BlockSpec((n, LANES), lambda i: (0, i), pipeline_mode=pl.Buffered(1)),
                  spec(tc, tc), spec(tc, p), spec(tc, p), spec(p, tc), spec(p, tc),
                  spec(1, p), spec(1, p), spec(bs, p), spec(bs, p), spec(1, tc)],
        out_specs=[pl.BlockSpec((n, LANES), lambda i: (0, i)),
                   spec(bp, p), spec(bp, p), spec(bs, p), spec(bs, p)],
        out_shape=[jax.ShapeDtypeStruct((n, d), F32),
                   jax.ShapeDtypeStruct((g, bp, p), F32), jax.ShapeDtypeStruct((g, bp, p), F32),
                   jax.ShapeDtypeStruct((g, bs, p), F32), jax.ShapeDtypeStruct((g, bs, p), F32)],
        scratch_shapes=[pltpu.VMEM((gs, r, tc), F32)] + [pltpu.VMEM((2, r, p), F32) for _ in range(4)],
        compiler_params=_cparams(("arbitrary",)), name="s5_main")(
            u, m, sbr, sbi, car, cai, a16r, a16i, h0r, h0i, d_t)
    states = tuple(jnp.swapaxes(h, 0, 1)[:, None] for h in (hpr, hpi, hsr, hsi))
    return z, states


def _glu_kernel(x_ref, z_ref, wa_ref, wb_ref, o_ref):
    z = z_ref[...].astype(BF16)
    o_ref[...] = x_ref[...] + _dot(z, wa_ref[...]) * jax.nn.sigmoid(_dot(z, wb_ref[...]))


def _glu(x, z, wa, wb):
    n, d = x.shape
    tm = _tile(MAT_TM, n)
    row = pl.BlockSpec((tm, d), lambda i: (i, 0))
    mat =pl.BlockSpec((d, d), lambda i: (0, 0))
    return pl.pallas_call(
        _glu_kernel, grid=(n // tm,), in_specs=[row, row, mat, mat], out_specs=row,
        out_shape=jax.ShapeDtypeStruct((n, d), F32),
        compiler_params=_cparams(("parallel",)), name="glu")(x, z, wa, wb)


def _head_norm(x, e_ref, et_ref):
    ms = _dot((x * x).astype(BF16), e_ref[...]) * (1.0 / HEAD_DIM)
    hi, lo = _split2(lax.rsqrt(ms + EPS))
    return _dot(hi, et_ref[...]) + _dot(lo, et_ref[...])


def _head_indicator(d):
    nh = d // HEAD_DIM
    e = (np.arange(d)[:, None] // HEAD_DIM == np.arange(LANES)[None, :]).astype(np.float32)
    assert nh <= LANES
    return jnp.asarray(e, BF16), jnp.asarray(e.T, BF16)


def _kvf_kernel(h_ref, wkv_ref, wf_ref, bf_ref, kn_ref, e_ref, et_ref,
                kp_ref, ks_ref, vp_ref, vs_ref, lp_ref, ls_ref, kb_ref, vb_ref, *, d, nh, npb):
    i = pl.program_id(0)
    h = h_ref[...]
    tm = h.shape[0]
    pkv = _dot(h, wkv_ref[...])
    kraw, v = pkv[:, :d], pkv[:, d:]
    k = kraw * _head_norm(kraw, e_ref, et_ref) * kn_ref[...]
    kb_ref[...] = k.astype(BF16)
    vb_ref[...] = v.astype(BF16)
    pf = _dot(h, wf_ref[...]) + bf_ref[...]
    lft = (jnp.minimum(pf, 0.0) - jnp.log1p(jnp.exp(-jnp.abs(pf)))).T[:nh]

    @pl.when(i < npb)
    def _():
        kp_ref[0] = k.T
        vp_ref[0] = v.T
        lp_ref[0] = lft

    @pl.when(i >= npb)
    def _():
        for hd in range(nh):
            rows = pl.ds(hd, tm, stride=nh)
            ks_ref[rows, :] = k[:, hd * HEAD_DIM:(hd + 1) * HEAD_DIM]
            vs_ref[rows, :] = v[:, hd * HEAD_DIM:(hd + 1) * HEAD_DIM]
        lq = ls_ref.shape[2]
        for j in range(tm // lq):
            ls_ref[j] = lft[:, j * lq:(j + 1) * lq]


def _kvf(hn, bp, s, lq, w_kv, w_f, b_f, k_norm):
    n, d = hn.shape
    nh = d // HEAD_DIM
    tm = min(TOK_TM, s)
    n_prompt = bp * s
    n_s = n - n_prompt
    assert s % tm == 0 and n_s % tm == 0 and tm % lq == 0
    npb, spb = n_prompt // tm, s // tm
    e, et = _head_indicator(d)
    row = pl.BlockSpec((tm, d), lambda i: (i, 0))
    const = lambda *sh: pl.BlockSpec(sh, lambda i: (0, 0))
    prompt_t = lambda rows: pl.BlockSpec(
        (1, rows, tm), lambda i: (jnp.minimum(i, npb - 1) // spb, 0, jnp.minimum(i, npb - 1) % spb))
    second = lambda i: (jnp.maximum(i - npb, 0), 0)
    kern = functools.partial(_kvf_kernel, d=d, nh=nh, npb=npb)
    return pl.pallas_call(
        kern, grid=(n // tm,),
        in_specs=[row, const(d, 2 * d), const(d, LANES), const(1, LANES), const(1, d), const(d, LANES), const(LANES, d)],
        out_specs=[prompt_t(d), pl.BlockSpec((tm * nh, HEAD_DIM), second),
                   prompt_t(d), pl.BlockSpec((tm * nh, HEAD_DIM), second),
                   prompt_t(nh), pl.BlockSpec((tm // lq, nh, lq), lambda i: (jnp.maximum(i - npb, 0), 0, 0)), row, row],
        out_shape=[jax.ShapeDtypeStruct((bp, d, s), F32), jax.ShapeDtypeStruct((n_s * nh, HEAD_DIM), F32),
                   jax.ShapeDtypeStruct((bp, d, s), F32), jax.ShapeDtypeStruct((n_s * nh, HEAD_DIM), F32),
                   jax.ShapeDtypeStruct((bp, nh, s), F32), jax.ShapeDtypeStruct((n_s // lq, nh, lq), F32),
                   jax.ShapeDtypeStruct((n, d), BF16), jax.ShapeDtypeStruct((n, d), BF16)],
        compiler_params=_cparams(("arbitrary",)), name="kvf")(
            hn, w_kv, jnp.pad(w_f, ((0, 0), (0, LANES - nh))), jnp.pad(b_f, (0, LANES - nh)).reshape(1, LANES),
            jnp.tile(k_norm, nh).reshape(1, d), e, et)


def _qproj_kernel(u_ref, wq_ref, qn_ref, e_ref, et_ref, q_ref):
    q = _dot(u_ref[...], wq_ref[...])
    q_ref[...] = (q * _head_norm(q, e_ref, et_ref) * qn_ref[...]).astype(BF16)


def _qproj(un, wq, qn_scaled):
    n, d = un.shape
    tm = _tile(MAT_TM, n)
    e, et = _head_indicator(d)
    row = pl.BlockSpec((tm, d), lambda i: (i, 0))
    const = lambda *s: pl.BlockSpec(s, lambda i: (0, 0))
    return pl.pallas_call(
        _qproj_kernel, grid=(n // tm,),
        in_specs=[row, const(d, d), const(1, d), const(d, LANES), const(LANES, d)], out_specs=row,
        out_shape=jax.ShapeDtypeStruct((n, d), BF16),
        compiler_params=_cparams(("parallel",)), name="qproj")(un, wq, qn_scaled, e, et)


def _oproj_kernel(x_ref, ap_ref, as_ref, wo_ref, o_ref, *, npb):
    a = jnp.where(pl.program_id(0) < npb, ap_ref[...], as_ref[...])
    o_ref[...] = x_ref[...] + _dot(a, wo_ref[...])


def _oproj(x, a_p, a_s, wo):
    n, d = x.shape
    tm = _tile(MAT_TM, a_p.shape[0], a_s.shape[0])
    assert a_p.shape[0] % tm == 0 and a_s.shape[0] % tm == 0
    npb = a_p.shape[0] // tm
    first, second = _part_maps(npb)
    row = pl.BlockSpec((tm, d), lambda i: (i, 0))
    return pl.pallas_call(
        functools.partial(_oproj_kernel, npb=npb), grid=(n // tm,),
        in_specs=[row, pl.BlockSpec((tm, d), first), pl.BlockSpec((tm, d), second),
                  pl.BlockSpec((d, d), lambda i: (0, 0))],
        out_specs=row, out_shape=jax.ShapeDtypeStruct((n, d), F32),
        compiler_params=_cparams(("parallel",)), name="oproj")(x, a_p, a_s, wo)


def _cumsum_kernel(x_ref, init_ref, o_ref, *, nblk, bw):
    tri = (lax.broadcasted_iota(jnp.int32, (bw, bw), 0) <= lax.broadcasted_iota(jnp.int32, (bw, bw), 1)).astype(BF16)
    carry = init_ref[...]
    for j in range(nblk):
        x = x_ref[:, j * bw:(j + 1) * bw]
        hi, mid, lo = _split3(x)
        cs = _dot(hi, tri) + _dot(mid, tri) + _dot(lo, tri) + carry
        o_ref[:, j * bw:(j + 1) * bw] = cs
        carry = cs[:, bw - 1:bw]


def _cumsum_rows(x, init):
    r, w = x.shape
    bw = min(256, w)
    kern = functools.partial(_cumsum_kernel, nblk=w // bw, bw=bw)
    return pl.pallas_call(
        kern, grid=(1,),
        in_specs=[pl.BlockSpec((r, w), lambda i: (0, 0)), pl.BlockSpec((r, 1), lambda i: (0, 0))],
        out_specs=pl.BlockSpec((r, w), lambda i: (0, 0)), out_shape=jax.ShapeDtypeStruct((r, w), F32),
        compiler_params=_cparams(("arbitrary",)), name="cumsum")(x, init)


def _lanes(x, w):
    if w <= LANES:
        return x[:, :w]
    return jnp.concatenate([x] * (w // LANES), axis=1)


def _head_masks(shape):
    head = lax.broadcasted_iota(jnp.int32, shape, 1) // HEAD_DIM
    return [head == i for i in range(shape[1] // HEAD_DIM)]


def _stack_heads(qs):
    zero = jnp.zeros_like(qs)
    return jnp.concatenate([jnp.where(mk, qs, zero) for mk in _head_masks(qs.shape)], axis=0)


def _flash_init(cq, hs, m_scr, acc_scr, cq_scr):
    tq = cq.shape[0]
    cq = cq * LOG2E
    m_scr[...] = jnp.full_like(m_scr, NEG_BIG)
    acc_scr[...] = jnp.zeros_like(acc_scr)
    for g in range(cq_scr.shape[0]):
        for i in range(hs):
            h = g * hs + i
            cq_scr[g, i * tq:(i + 1) * tq, :] = jnp.broadcast_to(cq[:, h:h + 1], (tq, LANES))


def _flash_slab(g, hs, q_st, ks, vs, ck_rows, causal, m_scr, acc_scr, cq_scr, kv_t=False):
    tq, tk = q_st.shape[0] // hs, (ks.shape[1] if kv_t else ks.shape[0])
    nt = (((1,), (1,)), ((), ()))
    t = _dot(q_st, ks) if kv_t else lax.dot_general(q_st, ks, nt, preferred_element_type=F32)
    if ck_rows is not None:
        t = jnp.concatenate([t[i * tq:(i + 1) * tq] - ck_rows[i] for i in range(hs)], axis=0)
    if causal is not None:
        keep = lax.broadcasted_iota(jnp.int32, (tq, tk), 1) <= lax.broadcasted_iota(jnp.int32, (tq, tk), 0) + causal
        t = jnp.where(jnp.concatenate([keep] * hs, axis=0), t, NEG_BIG)
    cq = cq_scr[g]
    m_prev = m_scr[g]
    m_new = jnp.maximum(m_prev, jnp.max(t, axis=1, keepdims=True) + cq)
    alpha = jnp.exp2(m_prev - m_new)
    pe = jnp.exp2(t - _lanes(m_new - cq, tk))
    w = hs * HEAD_DIM
    if w == LANES and not kv_t:
        pv = _dot(pe.astype(BF16), jnp.concatenate([vs, jnp.ones((tk, LANES), BF16)], axis=1))
    else:
        pv = lax.dot_general(pe.astype(BF16), vs, nt, preferred_element_type=F32) if kv_t else _dot(pe.astype(BF16), vs)
        pv = jnp.concatenate([pv, jnp.broadcast_to(jnp.sum(pe, axis=1, keepdims=True), (hs * tq, LANES))], axis=1)
    acc_scr[g] = _lanes(alpha, w + LANES) * acc_scr[g] + pv
    m_scr[g] = m_new


def _flash_finish(o_ref, hs, acc_scr):
    tq = o_ref.shape[0]
    w = hs * HEAD_DIM
    masks = _head_masks((tq, w))
    for g in range(acc_scr.shape[0]):
        acc = acc_scr[g]
        o = acc[:, :w] / _lanes(acc[:, w:], w)
        out = o[(hs - 1) * tq:]
        for i in range(hs - 2, -1, -1):
            out = jnp.where(masks[i], o[i * tq:(i + 1) * tq], out)
        o_ref[:, g * w:(g + 1) * w] = out.astype(o_ref.dtype)


def _flash_scratch(ng, rows, w):
    return [pltpu.VMEM((ng, rows, LANES), F32), pltpu.VMEM((ng, rows, w + LANES), F32),
            pltpu.VMEM((ng, rows, LANES), F32)]


def _attn_prompt_kernel(qi_tab, ki_tab, q_ref, k_ref, v_ref, cq_ref, ck_ref, o_ref, *scr, hs):
    step = pl.program_id(1)
    qi, ki = qi_tab[step], ki_tab[step]
    tq, tk = q_ref.shape[0], k_ref.shape[0]
    w = hs * HEAD_DIM
    ng = q_ref.shape[1] // w
    last = (qi * tq + tq - 1) // tk

    @pl.when(ki == 0)
    def _():
        _flash_init(cq_ref[...], hs, *scr)

    def sweep(causal, nkeys):
        ck = ck_ref[0] * LOG2E
        for g in range(ng):
            sl = slice(g * w, (g + 1) * w)
            _flash_slab(g, hs, _stack_heads(q_ref[:, sl]), k_ref[:nkeys, sl], v_ref[:nkeys, sl],
                        [ck[g * hs + i:g * hs + i + 1, :nkeys] for i in range(hs)], causal, *scr)

    @pl.when(ki < last)
    def _():
        sweep(None, tk)

    for off in range(0, tk, tq):
        @pl.when((ki == last) & (qi * tq - ki * tk == off))
        def _():
            sweep(off, off + tq)
            _flash_finish(o_ref, hs, scr[1])


def _attn_prompt(q, kb, vb, c_rows, c_t, nb, s):
    d = q.shape[1]
    nh = d // HEAD_DIM
    hs = ATT_HS
    tq, tk = min(ATT_TQ, s), min(ATT_TK, s)
    assert tk % tq == 0 and s % tk == 0
    nq, nk = s // tq, s // tk
    pairs = [(i, j) for i in range(nq) for j in range((i * tq + tq - 1) // tk + 1)]
    qi_tab = jnp.asarray([p[0] for p in pairs], jnp.int32)
    ki_tab = jnp.asarray([p[1] for p in pairs], jnp.int32)
    grid_spec = pltpu.PrefetchScalarGridSpec(
        num_scalar_prefetch=2, grid=(nb, len(pairs)),
        in_specs=[pl.BlockSpec((tq, d), lambda b, p, qt, kt: (b * nq + qt[p], 0)),
                  pl.BlockSpec((tk, d), lambda b, p, qt, kt: (b * nk + kt[p], 0)),
                  pl.BlockSpec((tk, d), lambda b, p, qt, kt: (b * nk + kt[p], 0)),
                  pl.BlockSpec((tq, nh), lambda b, p, qt, kt: (b * nq + qt[p], 0)),
                  pl.BlockSpec((1, nh, tk), lambda b, p, qt, kt: (b, 0, kt[p]))],
        out_specs=pl.BlockSpec((tq, d), lambda b, p, qt, kt: (b * nq + qt[p], 0)),
        scratch_shapes=_flash_scratch(nh // hs, hs * tq, hs * HEAD_DIM))
    return pl.pallas_call(
        functools.partial(_attn_prompt_kernel, hs=hs), grid_spec=grid_spec,
        out_shape=jax.ShapeDtypeStruct((nb * s, d), BF16),
        compiler_params=_cparams(("parallel", "arbitrary")), name="attn_prompt")(
            qi_tab, ki_tab, q, kb, vb, c_rows, c_t)


def _attn_sample_kernel(q_ref, ck_ref, cv_ref, kn_ref, vn_ref, cq_ref, ckc_ref, ckn_ref, o_ref, *scr, hs, nkb):
    j = pl.program_id(1)
    w = hs * HEAD_DIM
    ng = q_ref.shape[1] // w

    @pl.when(j == 0)
    def _():
        _flash_init(cq_ref[0], hs, *scr)

    def sweep(kslab, vslab, ck, causal, kv_t):
        ck = ck * LOG2E
        for g in range(ng):
            _flash_slab(g, hs, _stack_heads(q_ref[:, g * w:(g + 1) * w]), kslab(g), vslab(g),
                        [ck[g * hs + i:g * hs + i + 1, :] for i in range(hs)], causal, *scr, kv_t=kv_t)

    @pl.when(j < nkb)
    def _():
        sweep(lambda g: ck_ref[0, g * w:(g + 1) * w, :].astype(BF16),
              lambda g: cv_ref[0, g * w:(g + 1) * w, :].astype(BF16), ckc_ref[0], None, True)

    @pl.when(j == nkb)
    def _():
        sweep(lambda g: kn_ref[:, g * w:(g + 1) * w], lambda g: vn_ref[:, g * w:(g + 1) * w], ckn_ref[0], 0, False)
        _flash_finish(o_ref, hs, scr[1])


def _attn_sample(q, kb, vb, cache_kt, cache_vt, c_new, c_past_t, c_new_t, n_prompt):
    d = q.shape[1]
    nh = d // HEAD_DIM
    nb, _, plen = cache_kt.shape
    lq = c_new.shape[1]
    hs = DEC_HS
    tk = min(DEC_TK, plen)
    nkb = plen // tk
    r0 = n_prompt // lq
    new_rows = pl.BlockSpec((lq, d), lambda b, j: (r0 + b, 0))
    cache = pl.BlockSpec((1, d, tk), lambda b, j: (b, 0, jnp.minimum(j, nkb - 1)))
    kern = functools.partial(_attn_sample_kernel, hs=hs, nkb=nkb)
    return pl.pallas_call(
        kern, grid=(nb, nkb + 1),
        in_specs=[new_rows, cache, cache, new_rows, new_rows,
                  pl.BlockSpec((1, lq, nh), lambda b, j: (b, 0, 0)),
                  pl.BlockSpec((1, nh, tk), lambda b, j: (b, 0, jnp.minimum(j, nkb - 1))),
                  pl.BlockSpec((1, nh, lq), lambda b, j: (b, 0, 0))],
        out_specs=pl.BlockSpec((lq, d), lambda b, j: (b, 0)),
        out_shape=jax.ShapeDtypeStruct((nb * lq, d), BF16),
        scratch_shapes=_flash_scratch(nh // hs, hs * lq, hs * HEAD_DIM),
        compiler_params=_cparams(("parallel", "arbitrary")), name="attn_sample")(
            q, cache_kt, cache_vt, kb, vb, c_new, c_past_t, c_new_t)


def kernel(x_prompt, x_sample, cache_k, cache_v, cache_logf, state_ssm_re, state_ssm_im, ffn_norm, w_ffn_gate, w_ffn_up, w_ffn_down, mix_norm, ssm_a_re, ssm_a_im, ssm_log_dt, ssm_b_re, ssm_b_im, ssm_c_re, ssm_c_im, ssm_d, w_glu_a, w_glu_b, kv_norm, w_kvf, b_f, k_norm, w_q, q_norm, w_o):
    bp, s, d = x_prompt.shape
    bs, lq, _ = x_sample.shape
    plen = cache_k.shape[1]
    nh = d // HEAD_DIM
    assert ffn_norm.shape[0] == 2 and ssm_a_re.shape[0] == 1 and w_q.shape[0] == 1
    n_prompt, n_sample = bp * s, bs * lq
    bf = lambda w: w.astype(BF16)

    def ffn(xs, l, j, **kw):
        return _ffn(xs, ffn_norm[l, j], w_ffn_gate, w_ffn_up, w_ffn_down, (l, j), **kw)

    x, u = ffn([x_prompt.reshape(n_prompt, d), x_sample.reshape(n_sample, d)], 0, 0, g2=mix_norm[0], norm_dtype=F32)
    mats = _s5_params(ssm_log_dt[0], ssm_a_re[0], ssm_a_im[0], ssm_b_re[0], ssm_b_im[0], ssm_c_re[0], ssm_c_im[0])
    z, (p_re, p_im, s_re, s_im) = _s5_layer(u, n_prompt, bp, bs, state_ssm_re[:, 0], state_ssm_im[:, 0], mats, ssm_d[0])
    x = _glu(x, z, bf(w_glu_a[0]), bf(w_glu_b[0]))
    x, hn = ffn([x], 0, 1, g2=kv_norm)

    kt_p, k_s, vt_p, v_s, lft_p, lft_s, kb, vb = _kvf(hn, bp, s, lq, bf(w_kvf[:, :2 * d]), bf(w_kvf[:, 2 * d:]), b_f, k_norm)

    def cumsum_t(lft, init):
        b, _, l = lft.shape
        init = jnp.zeros((b * nh, 1), F32) if init is None else init.reshape(b * nh, 1)
        return _cumsum_rows(lft.reshape(b * nh, l), init).reshape(b, nh, l)

    cp_t = cumsum_t(lft_p, None)
    cpast_t = cumsum_t(jnp.swapaxes(cache_logf.astype(F32), 1, 2), None)
    cnew_t = cumsum_t(lft_s, cpast_t[:, :, -1])

    x, un = ffn([x], 1, 0, g2=mix_norm[1])
    q = _qproj(un, bf(w_q[0]), (jnp.tile(q_norm[0], nh) * (LOG2E / math.sqrt(HEAD_DIM))).reshape(1, d))
    a_p = _attn_prompt(q, kb, vb, jnp.swapaxes(cp_t, 1, 2).reshape(n_prompt, nh), cp_t, bp, s)
    pos_minor = lambda c: jnp.transpose(c, (0, 2, 3, 1)).reshape(bs, d, plen)
    a_s = _attn_sample(q, kb, vb, pos_minor(cache_k), pos_minor(cache_v),
                       jnp.swapaxes(cnew_t, 1, 2), cpast_t, cnew_t, n_prompt)
    x = _oproj(x, a_p, a_s, bf(w_o[0]))
    (y_p, y_s), _ = ffn([x], 1, 1, split_out=(n_prompt, n_sample))

    cache_p = lambda a: jnp.transpose(a.reshape(bp, nh, HEAD_DIM, s), (0, 3, 1, 2))
    cache_s = lambda a: a.reshape(bs, lq, nh, HEAD_DIM)
    return (y_p.reshape(bp, s, d), y_s.reshape(bs, lq, d), p_re, p_im, cache_p(kt_p), cache_p(vt_p),
            jnp.swapaxes(lft_p, 1, 2), s_re, s_im, cache_s(k_s), cache_s(v_s), jnp.swapaxes(lft_s, 1, 2))
```

```python
import functools
import math

import jax
import jax.numpy as jnp
import numpy as np
from jax import lax
from jax.experimental import pallas as pl
from jax.experimental.pallas import tpu as pltpu

F32 = jnp.float32
BF16 = jnp.bfloat16

EPS = 1e-6
HEAD_DIM = 64
SSM_GROUP = 16
S5_CHUNK = 16
LANES = 128
_SLAB_GROUPS = LANES // SSM_GROUP
NEG_BIG = -1e30
LOG2E = math.log2(math.e)
VMEM_LIMIT = 57 * 1024 * 1024

FFN_TM = 1024
FFN_TF = 512
TOK_TM = 512
MAT_TM = 1024
ATT_TQ = 256
ATT_TK = 512
ATT_HS = 2
DEC_TK = 2048
DEC_HS = 4


def _cparams(sem):
    return pltpu.CompilerParams(dimension_semantics=sem, vmem_limit_bytes=VMEM_LIMIT)


def _rms_scale(x):
    return lax.rsqrt(jnp.mean(x * x, axis=-1, keepdims=True) + EPS)


def _dot(a, b):
    return jnp.dot(a, b, preferred_element_type=F32)


def _split2(x):
    hi = x.astype(BF16)
    lo = (x - hi.astype(F32)).astype(BF16)
    return hi, lo


def _split3(x):
    hi = x.astype(BF16)
    r1 = x - hi.astype(F32)
    mid = r1.astype(BF16)
    return hi, mid, (r1 - mid.astype(F32)).astype(BF16)


def _tile(limit, *sizes):
    while any(s % limit for s in sizes):
        limit //= 2
    return limit


def _part_maps(npb):
    first = lambda i, *_: (jnp.minimum(i, npb - 1), 0)
    second = lambda i, *_: (jnp.maximum(i - npb, 0), 0)
    return first, second


def _ffn_kernel(*refs, nf, npb, n_in, n_out, with_norm):
    x_refs, refs = refs[:n_in], refs[n_in:]
    g_ref, wg_ref, wu_ref, wd_ref = refs[:4]
    refs = refs[4:]
    g2_ref = refs[0] if with_norm else None
    refs = refs[with_norm:]
    o_refs, refs = refs[:n_out], refs[n_out:]
    o2_ref = refs[0] if with_norm else None
    h_scr, acc_scr = refs[with_norm:]
    i, f = pl.program_id(0), pl.program_id(1)

    def with_x(cond, fn):
        if n_in == 1:
            pl.when(cond)(lambda: fn(x_refs[0][...]))
        else:
            pl.when(cond & (i < npb))(lambda: fn(x_refs[0][...]))
            pl.when(cond & (i >= npb))(lambda: fn(x_refs[1][...]))

    def start(x):
        h_scr[...] = (x * _rms_scale(x) * g_ref[...]).astype(BF16)
        acc_scr[...] = jnp.zeros_like(acc_scr)

    with_x(f == 0, start)

    h = h_scr[...]
    a = _dot(h, wg_ref[...].astype(BF16))
    b = _dot(h, wu_ref[...].astype(BF16))
    t = (a * jax.nn.sigmoid(a)) * b
    acc_scr[...] += _dot(t.astype(BF16), wd_ref[...].astype(BF16))

    def finish(x):
        y = x + 0.5 * acc_scr[...]
        if n_out == 1:
            o_refs[0][...] = y
        else:
            @pl.when(i < npb)
            def _():
                o_refs[0][...] = y

            @pl.when(i >= npb)
            def _():
                o_refs[1][...] = y
        if with_norm:
            o2_ref[...] = (y * _rms_scale(y) * g2_ref[...]).astype(o2_ref.dtype)

    with_x(f == nf - 1, finish)


def _ffn(xs, g, wg, wu, wd, lj, g2=None, split_out=None, norm_dtype=BF16):
    d = xs[0].shape[1]
    n = sum(x.shape[0] for x in xs)
    dff = wg.shape[-1]
    l, j = lj
    tm, tf = min(FFN_TM, n), min(FFN_TF, dff)
    nf = dff // tf
    n_first = xs[0].shape[0] if len(xs) == 2 else (split_out[0] if split_out else n)
    assert all(x.shape[0] % tm == 0 for x in xs) and n_first % tm == 0
    npb = n_first // tm
    first, second = _part_maps(npb)
    with_norm = g2 is not None
    row = pl.BlockSpec((tm, d), lambda i, f: (i, 0))
    vec = pl.BlockSpec((1, d), lambda i, f: (0, 0))
    parts = [pl.BlockSpec((tm, d), first), pl.BlockSpec((tm, d), second)]
    hid = lambda i, f: jnp.where(i % 2 == 0, f, nf - 1 - f)
    in_specs = (parts if len(xs) == 2 else [row]) + [
        vec, pl.BlockSpec((None, None, d, tf), lambda i, f: (l, j, 0, hid(i, f))),
        pl.BlockSpec((None, None, d, tf), lambda i, f: (l, j, 0, hid(i, f))),
        pl.BlockSpec((None, None, tf, d), lambda i, f: (l, j, hid(i, f), 0))]
    args = list(xs) + [g.reshape(1, d), wg, wu, wd]
    if split_out:
        out_shape = [jax.ShapeDtypeStruct((m, d), F32) for m in split_out]
        out_specs = list(parts)
    else:
        out_shape = [jax.ShapeDtypeStruct((n, d), F32)]
        out_specs = [row]
    n_out = len(out_shape)
    if with_norm:
        in_specs.append(vec)
        args.append(g2.reshape(1, d))
        out_shape.append(jax.ShapeDtypeStruct((n, d), norm_dtype))
        out_specs.append(row)
    kern = functools.partial(_ffn_kernel, nf=nf, npb=npb, n_in=len(xs), n_out=n_out, with_norm=with_norm)
    res = pl.pallas_call(
        kern, grid=(n // tm, nf), in_specs=in_specs, out_specs=out_specs, out_shape=out_shape,
        scratch_shapes=[pltpu.VMEM((tm, d), BF16), pltpu.VMEM((tm, d), F32)],
        compiler_params=_cparams(("arbitrary", "arbitrary")), name="ffn")(*args)
    y = tuple(res[:n_out]) if split_out else res[0]
    return y, (res[n_out] if with_norm else None)


def _s5_param_kernel(*refs):
    lax.fori_loop(0, _SLAB_GROUPS, functools.partial(_s5_param_group, refs), 0)


def _s5_param_group(refs, slot, carry):
    (ldt_ref, arc_ref, aic_ref, arr_ref, air_ref, btr_ref, bti_ref, ctr_ref, cti_ref,
     m_ref, sbr_ref, sbi_ref, car_ref, cai_ref, a16r_ref, a16i_ref) = refs
    t, c = S5_CHUNK, SSM_GROUP
    dt = jnp.exp(ldt_ref[slot])

    def cpow(ar, ai, n):
        mag = jnp.exp(ar * dt)
        pr, pi = mag * jnp.cos(ai * dt), mag * jnp.sin(ai * dt)
        n = jnp.asarray(n, jnp.int32)
        shape = jnp.broadcast_shapes(ar.shape, n.shape)
        re, im = jnp.ones(shape, F32), jnp.zeros(shape, F32)
        for b in range(t.bit_length()):
            bit = ((n >> b) & 1) == 1
            re, im = jnp.where(bit, re * pr - im * pi, re), jnp.where(bit, re * pi + im * pr, im)
            pr, pi = pr * pr - pi * pi, 2.0 * pr * pi
        return re, im

    def step_of(pos):
        seg = pos // c
        return (seg // _SLAB_GROUPS) * _SLAB_GROUPS + ((seg - slot) & (_SLAB_GROUPS - 1))

    arc, aic = arc_ref[slot], aic_ref[slot]
    arr, air = arr_ref[slot], air_ref[slot]
    p = arc.shape[0]
    lane_pc = lax.broadcasted_iota(jnp.int32, (p, t * c), 1)
    cr, ci = ctr_ref[slot], cti_ref[slot]
    pr, pi = cpow(arc, aic, step_of(lane_pc) + 1)
    car_ref[slot] = (cr * pr - ci * pi).astype(BF16)
    cai_ref[slot] = (-(cr * pi + ci * pr)).astype(BF16)
    pr, pi = cpow(arc, aic, lane_pc // c)
    csr = cr * pr - ci * pi
    csi = cr * pi + ci * pr

    abr, abi = cpow(arr, air, 1)
    xr, xi = abr - 1.0, abi
    den = arr * arr + air * air
    qr = (xr * arr + xi * air) / den
    qi = (xi * arr - xr * air) / den
    btr, bti = btr_ref[slot], bti_ref[slot]
    bbr = qr * btr - qi * bti
    bbi = qr * bti + qi * btr
    iidx = step_of(lax.broadcasted_iota(jnp.int32, (t * c, p), 0))
    pr, pi = cpow(arr, air, (t - 1) - iidx)
    sbr_ref[slot] = (bbr * pr - bbi * pi).astype(BF16)
    sbi_ref[slot] = (bbr * pi + bbi * pr).astype(BF16)
    pr, pi = cpow(arr, air, t)
    a16r_ref[slot] = pr
    a16i_ref[slot] = pi

    hp = lax.Precision.HIGHEST
    kt = (jnp.dot(bbr[:c], csr, precision=hp, preferred_element_type=F32)
          - jnp.dot(bbi[:c], csi, precision=hp, preferred_element_type=F32))
    lane = lax.broadcasted_iota(jnp.int32, (c, t * c), 1)
    for i in range(t):
        blk = kt if i == 0 else pltpu.roll(kt, i * c, 1)
        blk = jnp.where(lane >= i * c, blk, 0.0)
        blk = jnp.concatenate([pltpu.roll(blk[:, h * LANES:(h + 1) * LANES], slot * c, 1)
                               for h in range(t * c // LANES)], axis=1)
        seg = (i // _SLAB_GROUPS) * _SLAB_GROUPS + ((i + slot) & (_SLAB_GROUPS - 1))
        m_ref[slot, pl.ds(pl.multiple_of(seg * c, c), c), :] = blk.astype(BF16)
    return carry


def _s5_params(log_dt, a_re, a_im, b_re, b_im, c_re, c_im):
    g, p = a_re.shape
    t, c = S5_CHUNK, SSM_GROUP
    tc = t * c

    def tile_b(b):
        return jnp.broadcast_to(jnp.swapaxes(b, 1, 2)[:, None], (g, t, c, p)).reshape(g, tc, p)

    def tile_c(cm):
        return jnp.broadcast_to(jnp.swapaxes(cm, 1, 2)[:, :, None], (g, p, t, c)).reshape(g, p, tc)

    def spec(*shape):
        return pl.BlockSpec((_SLAB_GROUPS,) + shape, lambda i: (i, 0, 0))

    assert g % _SLAB_GROUPS == 0
    return pl.pallas_call(
        _s5_param_kernel, grid=(g // _SLAB_GROUPS,),
        in_specs=[spec(1, 1), spec(p, 1), spec(p, 1), spec(1, p), spec(1, p),
                  spec(tc, p), spec(tc, p), spec(p, tc), spec(p, tc)],
        out_specs=[spec(tc, tc), spec(tc, p), spec(tc, p), spec(p, tc), spec(p, tc), spec(1, p), spec(1, p)],
        out_shape=[jax.ShapeDtypeStruct((g, tc, tc), BF16),
                   jax.ShapeDtypeStruct((g, tc, p), BF16), jax.ShapeDtypeStruct((g, tc, p), BF16),
                   jax.ShapeDtypeStruct((g, p, tc), BF16), jax.ShapeDtypeStruct((g, p, tc), BF16),
                   jax.ShapeDtypeStruct((g, 1, p), F32), jax.ShapeDtypeStruct((g, 1, p), F32)],
        compiler_params=_cparams(("parallel",)), name="s5_params")(
            log_dt.reshape(g, 1, 1), a_re.reshape(g, p, 1), a_im.reshape(g, p, 1),
            a_re.reshape(g, 1, p), a_im.reshape(g, 1, p),
            tile_b(b_re), tile_b(b_im), tile_c(c_re), tile_c(c_im))


def _gelu_tanh(y):
    return 0.5 * y * (1.0 + jnp.tanh(math.sqrt(2.0 / math.pi) * (y + 0.044715 * (y * y * y))))


def _pick_segments(v, first):
    seg = lax.broadcasted_iota(jnp.int32, v[0].shape, 1) // SSM_GROUP
    out = v[(7 + first) % 8]
    for s in range(6, -1, -1):
        out = jnp.where(seg == s, v[(s + first) % 8], out)
    return out


def _s5_main_kernel(u_ref, m_ref, sbr_ref, sbi_ref, car_ref, cai_ref, a16r_ref, a16i_ref,
                    h0r_ref, h0i_ref, d_ref, z_ref, hpr_ref, hpi_ref, hsr_ref, hsi_ref,
                    u2_scr, sr_scr, si_scr, hr_scr, hi_scr, *, bp, nkp, bs, nks):
    t = S5_CHUNK
    npq = bp * nkp // 8
    nq = npq + bs * nks // 8
    sp = nkp * t

    def token0(q):
        return jnp.where(q < npq, (q % bp) * sp + (q // bp) * (8 * t), bp * sp + (q - npq) * (8 * t))

    def gather(q, carry):
        tok, rows = token0(q), pl.ds(pl.multiple_of(q * 8, 8), 8)
        for h in range(2):
            v = [u_ref[pl.ds(tok + 8 * h + i, 8, stride=t), :] for i in range(8)]
            v = [x if i == 0 else pltpu.roll(x, i * SSM_GROUP, 1) for i, x in enumerate(v)]
            for gi in range(8):
                u2_scr[gi, rows, h * LANES:(h + 1) * LANES] = _pick_segments(v, 8 - gi)
        return carry

    lax.fori_loop(0, nq, gather, 0, unroll=4)

    def group(gi, buf):
        sr, si, hrs, his = sr_scr.at[buf], si_scr.at[buf], hr_scr.at[buf], hi_scr.at[buf]
        u = u2_scr[gi]
        ub = u.astype(BF16)
        sr[...] = _dot(ub, sbr_ref[gi])
        si[...] = _dot(ub, sbi_ref[gi])
        ar, ai = a16r_ref[gi], a16i_ref[gi]

        def step(rows, hr, hi):
            hrs[rows, :] = hr
            his[rows, :] = hi
            return (ar * hr - ai * hi + sr[rows, :], ar * hi + ai * hr + si[rows, :])

        hr = hi = jnp.zeros((bp, ar.shape[1]), F32)
        for k in range(nkp):
            hr, hi = step(pl.ds((k // 8) * bp * 8 + k % 8, bp, stride=8), hr, hi)
        hpr_ref[gi] = hr
        hpi_ref[gi] = hi
        hr, hi = h0r_ref[gi], h0i_ref[gi]
        for k in range(nks):
            hr, hi = step(pl.ds(bp * nkp + k, bs, stride=nks), hr, hi)
        hsr_ref[gi] = hr
        hsi_ref[gi] = hi

        y = (_dot(ub, m_ref[gi]) + _dot(hrs[...].astype(BF16), car_ref[gi])
             + _dot(his[...].astype(BF16), cai_ref[gi]) + d_ref[gi] * u)
        u2_scr[gi] = _gelu_tanh(y)

    def group_pair(j, carry):
        group(2 * j, 0)
        group(2 * j + 1, 1)
        return carry

    lax.fori_loop(0, _SLAB_GROUPS // 2, group_pair, 0)

    def scatter(q, carry):
        tok, rows = token0(q), pl.ds(pl.multiple_of(q * 8, 8), 8)
        for h in range(2):
            v = [u2_scr[gi, rows, h * LANES:(h + 1) * LANES] for gi in range(8)]
            for i in range(8):
                w = _pick_segments(v, 8 - i)
                w = w if i == 0 else pltpu.roll(w, LANES - i * SSM_GROUP, 1)
                z_ref[pl.ds(tok + 8 * h + i, 8, stride=t), :] = w
        return carry

    lax.fori_loop(0, nq, scatter, 0, unroll=4)


def _s5_layer(u, n_prompt, bp, bs, h0_re, h0_im, mats, d_skip):
    n, d = u.shape
    t, c = S5_CHUNK, SSM_GROUP
    g, tc = d // c, t * c
    gs = LANES // c
    p = h0_re.shape[-1]
    nkp = n_prompt // (bp * t)
    nks = (n - n_prompt) // (bs * t)
    r = nkp * bp + nks * bs
    assert gs == 8 and t == 16 and bp % 8 == 0 and nkp % 8 == 0 and (bs * nks) % 8 == 0

    m, sbr, sbi, car, cai, a16r, a16i = mats
    d_t = jnp.broadcast_to(d_skip.reshape(g, 1, 1, c), (g, 1, t, c)).reshape(g, 1, tc)
    h0r = jnp.swapaxes(h0_re, 0, 1)
    h0i = jnp.swapaxes(h0_im, 0, 1)

    def spec(*shape):
        return pl.BlockSpec((gs,) + shape, lambda i: (i, 0, 0))

    kern = functools.partial(_s5_main_kernel, bp=bp, nkp=nkp, bs=bs, nks=nks)
    z, hpr, hpi, hsr, hsi = pl.pallas_call(
        kern, grid=(g // gs,),
        in_specs=[pl.BlockSpec((n, LANES), lambda i: (0, i), pipeline_mode=pl.Buffered(1)),
                  spec(tc, tc), spec(tc, p), spec(tc, p), spec(p, tc), spec(p, tc),
                  spec(1, p), spec(1, p), spec(bs, p), spec(bs, p), spec(1, tc)],
        out_specs=[pl.BlockSpec((n, LANES), lambda i: (0, i)),
                   spec(bp, p), spec(bp, p), spec(bs, p), spec(bs, p)],
        out_shape=[jax.ShapeDtypeStruct((n, d), F32),
                   jax.ShapeDtypeStruct((g, bp, p), F32), jax.ShapeDtypeStruct((g, bp, p), F32),
                   jax.ShapeDtypeStruct((g, bs, p), F32), jax.ShapeDtypeStruct((g, bs, p), F32)],
        scratch_shapes=[pltpu.VMEM((gs, r, tc), F32)] + [pltpu.VMEM((2, r, p), F32) for _ in range(4)],
        compiler_params=_cparams(("arbitrary",)), name="s5_main")(
            u, m, sbr, sbi, car, cai, a16r, a16i, h0r, h0i, d_t)
    states = tuple(jnp.swapaxes(h, 0, 1)[:, None] for h in (hpr, hpi, hsr, hsi))
    return z, states


def _glu_kernel(x_ref, z_ref, wa_ref, wb_ref, o_ref):
    z = z_ref[...].astype(BF16)
    o_ref[...] = x_ref[...] + _dot(z, wa_ref[...]) * jax.nn.sigmoid(_dot(z, wb_ref[...]))


def _glu(x, z, wa, wb):
    n, d = x.shape
    tm = _tile(MAT_TM, n)
    row = pl.BlockSpec((tm, d), lambda i: (i, 0))
    mat =pl.BlockSpec((d, d), lambda i: (0, 0))
    return pl.pallas_call(
        _glu_kernel, grid=(n // tm,), in_specs=[row, row, mat, mat], out_specs=row,
        out_shape=jax.ShapeDtypeStruct((n, d), F32),
        compiler_params=_cparams(("parallel",)), name="glu")(x, z, wa, wb)


def _head_norm(x, e_ref, et_ref):
    ms = _dot((x * x).astype(BF16), e_ref[...]) * (1.0 / HEAD_DIM)
    hi, lo = _split2(lax.rsqrt(ms + EPS))
    return _dot(hi, et_ref[...]) + _dot(lo, et_ref[...])


def _head_indicator(d):
    nh = d // HEAD_DIM
    e = (np.arange(d)[:, None] // HEAD_DIM == np.arange(LANES)[None, :]).astype(np.float32)
    assert nh <= LANES
    return jnp.asarray(e, BF16), jnp.asarray(e.T, BF16)


def _kvf_kernel(h_ref, wkv_ref, wf_ref, bf_ref, kn_ref, e_ref, et_ref,
                kp_ref, ks_ref, vp_ref, vs_ref, lp_ref, ls_ref, kb_ref, vb_ref, *, d, nh, npb):
    i = pl.program_id(0)
    h = h_ref[...]
    tm = h.shape[0]
    pkv = _dot(h, wkv_ref[...])
    kraw, v = pkv[:, :d], pkv[:, d:]
    k = kraw * _head_norm(kraw, e_ref, et_ref) * kn_ref[...]
    kb_ref[...] = k.astype(BF16)
    vb_ref[...] = v.astype(BF16)
    pf = _dot(h, wf_ref[...]) + bf_ref[...]
    lft = (jnp.minimum(pf, 0.0) - jnp.log1p(jnp.exp(-jnp.abs(pf)))).T[:nh]

    @pl.when(i < npb)
    def _():
        kp_ref[0] = k.T
        vp_ref[0] = v.T
        lp_ref[0] = lft

    @pl.when(i >= npb)
    def _():
        for hd in range(nh):
            rows = pl.ds(hd, tm, stride=nh)
            ks_ref[rows, :] = k[:, hd * HEAD_DIM:(hd + 1) * HEAD_DIM]
            vs_ref[rows, :] = v[:, hd * HEAD_DIM:(hd + 1) * HEAD_DIM]
        lq = ls_ref.shape[2]
        for j in range(tm // lq):
            ls_ref[j] = lft[:, j * lq:(j + 1) * lq]


def _kvf(hn, bp, s, lq, w_kv, w_f, b_f, k_norm):
    n, d = hn.shape
    nh = d // HEAD_DIM
    tm = min(TOK_TM, s)
    n_prompt = bp * s
    n_s = n - n_prompt
    assert s % tm == 0 and n_s % tm == 0 and tm % lq == 0
    npb, spb = n_prompt // tm, s // tm
    e, et = _head_indicator(d)
    row = pl.BlockSpec((tm, d), lambda i: (i, 0))
    const = lambda *sh: pl.BlockSpec(sh, lambda i: (0, 0))
    prompt_t = lambda rows: pl.BlockSpec(
        (1, rows, tm), lambda i: (jnp.minimum(i, npb - 1) // spb, 0, jnp.minimum(i, npb - 1) % spb))
    second = lambda i: (jnp.maximum(i - npb, 0), 0)
    kern = functools.partial(_kvf_kernel, d=d, nh=nh, npb=npb)
    return pl.pallas_call(
        kern, grid=(n // tm,),
        in_specs=[row, const(d, 2 * d), const(d, LANES), const(1, LANES), const(1, d), const(d, LANES), const(LANES, d)],
        out_specs=[prompt_t(d), pl.BlockSpec((tm * nh, HEAD_DIM), second),
                   prompt_t(d), pl.BlockSpec((tm * nh, HEAD_DIM), second),
                   prompt_t(nh), pl.BlockSpec((tm // lq, nh, lq), lambda i: (jnp.maximum(i - npb, 0), 0, 0)), row, row],
        out_shape=[jax.ShapeDtypeStruct((bp, d, s), F32), jax.ShapeDtypeStruct((n_s * nh, HEAD_DIM), F32),
                   jax.ShapeDtypeStruct((bp, d, s), F32), jax.ShapeDtypeStruct((n_s * nh, HEAD_DIM), F32),
                   jax.ShapeDtypeStruct((bp, nh, s), F32), jax.ShapeDtypeStruct((n_s // lq, nh, lq), F32),
                   jax.ShapeDtypeStruct((n, d), BF16), jax.ShapeDtypeStruct((n, d), BF16)],
        compiler_params=_cparams(("arbitrary",)), name="kvf")(
            hn, w_kv, jnp.pad(w_f, ((0, 0), (0, LANES - nh))), jnp.pad(b_f, (0, LANES - nh)).reshape(1, LANES),
            jnp.tile(k_norm, nh).reshape(1, d), e, et)


def _qproj_kernel(u_ref, wq_ref, qn_ref, e_ref, et_ref, q_ref):
    q = _dot(u_ref[...], wq_ref[...])
    q_ref[...] = (q * _head_norm(q, e_ref, et_ref) * qn_ref[...]).astype(BF16)


def _qproj(un, wq, qn_scaled):
    n, d = un.shape
    tm = _tile(MAT_TM, n)
    e, et = _head_indicator(d)
    row = pl.BlockSpec((tm, d), lambda i: (i, 0))
    const = lambda *s: pl.BlockSpec(s, lambda i: (0, 0))
    return pl.pallas_call(
        _qproj_kernel, grid=(n // tm,),
        in_specs=[row, const(d, d), const(1, d), const(d, LANES), const(LANES, d)], out_specs=row,
        out_shape=jax.ShapeDtypeStruct((n, d), BF16),
        compiler_params=_cparams(("parallel",)), name="qproj")(un, wq, qn_scaled, e, et)


def _oproj_kernel(x_ref, ap_ref, as_ref, wo_ref, o_ref, *, npb):
    a = jnp.where(pl.program_id(0) < npb, ap_ref[...], as_ref[...])
    o_ref[...] = x_ref[...] + _dot(a, wo_ref[...])


def _oproj(x, a_p, a_s, wo):
    n, d = x.shape
    tm = _tile(MAT_TM, a_p.shape[0], a_s.shape[0])
    assert a_p.shape[0] % tm == 0 and a_s.shape[0] % tm == 0
    npb = a_p.shape[0] // tm
    first, second = _part_maps(npb)
    row = pl.BlockSpec((tm, d), lambda i: (i, 0))
    return pl.pallas_call(
        functools.partial(_oproj_kernel, npb=npb), grid=(n // tm,),
        in_specs=[row, pl.BlockSpec((tm, d), first), pl.BlockSpec((tm, d), second),
                  pl.BlockSpec((d, d), lambda i: (0, 0))],
        out_specs=row, out_shape=jax.ShapeDtypeStruct((n, d), F32),
        compiler_params=_cparams(("parallel",)), name="oproj")(x, a_p, a_s, wo)


def _cumsum_kernel(x_ref, init_ref, o_ref, *, nblk, bw):
    tri = (lax.broadcasted_iota(jnp.int32, (bw, bw), 0) <= lax.broadcasted_iota(jnp.int32, (bw, bw), 1)).astype(BF16)
    carry = init_ref[...]
    for j in range(nblk):
        x = x_ref[:, j * bw:(j + 1) * bw]
        hi, mid, lo = _split3(x)
        cs = _dot(hi, tri) + _dot(mid, tri) + _dot(lo, tri) + carry
        o_ref[:, j * bw:(j + 1) * bw] = cs
        carry = cs[:, bw - 1:bw]


def _cumsum_rows(x, init):
    r, w = x.shape
    bw = min(256, w)
    kern = functools.partial(_cumsum_kernel, nblk=w // bw, bw=bw)
    return pl.pallas_call(
        kern, grid=(1,),
        in_specs=[pl.BlockSpec((r, w), lambda i: (0, 0)), pl.BlockSpec((r, 1), lambda i: (0, 0))],
        out_specs=pl.BlockSpec((r, w), lambda i: (0, 0)), out_shape=jax.ShapeDtypeStruct((r, w), F32),
        compiler_params=_cparams(("arbitrary",)), name="cumsum")(x, init)


def _lanes(x, w):
    if w <= LANES:
        return x[:, :w]
    return jnp.concatenate([x] * (w // LANES), axis=1)


def _head_masks(shape):
    head = lax.broadcasted_iota(jnp.int32, shape, 1) // HEAD_DIM
    return [head == i for i in range(shape[1] // HEAD_DIM)]


def _stack_heads(qs):
    zero = jnp.zeros_like(qs)
    return jnp.concatenate([jnp.where(mk, qs, zero) for mk in _head_masks(qs.shape)], axis=0)


def _flash_init(cq, hs, m_scr, acc_scr, cq_scr):
    tq = cq.shape[0]
    cq = cq * LOG2E
    m_scr[...] = jnp.full_like(m_scr, NEG_BIG)
    acc_scr[...] = jnp.zeros_like(acc_scr)
    for g in range(cq_scr.shape[0]):
        for i in range(hs):
            h = g * hs + i
            cq_scr[g, i * tq:(i + 1) * tq, :] = jnp.broadcast_to(cq[:, h:h + 1], (tq, LANES))


def _flash_slab(g, hs, q_st, ks, vs, ck_rows, causal, m_scr, acc_scr, cq_scr, kv_t=False):
    tq, tk = q_st.shape[0] // hs, (ks.shape[1] if kv_t else ks.shape[0])
    nt = (((1,), (1,)), ((), ()))
    t = _dot(q_st, ks) if kv_t else lax.dot_general(q_st, ks, nt, preferred_element_type=F32)
    if ck_rows is not None:
        t = jnp.concatenate([t[i * tq:(i + 1) * tq] - ck_rows[i] for i in range(hs)], axis=0)
    if causal is not None:
        keep = lax.broadcasted_iota(jnp.int32, (tq, tk), 1) <= lax.broadcasted_iota(jnp.int32, (tq, tk), 0) + causal
        t = jnp.where(jnp.concatenate([keep] * hs, axis=0), t, NEG_BIG)
    cq = cq_scr[g]
    m_prev = m_scr[g]
    m_new = jnp.maximum(m_prev, jnp.max(t, axis=1, keepdims=True) + cq)
    alpha = jnp.exp2(m_prev - m_new)
    pe = jnp.exp2(t - _lanes(m_new - cq, tk))
    w = hs * HEAD_DIM
    if w == LANES and not kv_t:
        pv = _dot(pe.astype(BF16), jnp.concatenate([vs, jnp.ones((tk, LANES), BF16)], axis=1))
    else:
        pv = lax.dot_general(pe.astype(BF16), vs, nt, preferred_element_type=F32) if kv_t else _dot(pe.astype(BF16), vs)
        pv = jnp.concatenate([pv, jnp.broadcast_to(jnp.sum(pe, axis=1, keepdims=True), (hs * tq, LANES))], axis=1)
    acc_scr[g] = _lanes(alpha, w + LANES) * acc_scr[g] + pv
    m_scr[g] = m_new


def _flash_finish(o_ref, hs, acc_scr):
    tq = o_ref.shape[0]
    w = hs * HEAD_DIM
    masks = _head_masks((tq, w))
    for g in range(acc_scr.shape[0]):
        acc = acc_scr[g]
        o = acc[:, :w] / _lanes(acc[:, w:], w)
        out = o[(hs - 1) * tq:]
        for i in range(hs - 2, -1, -1):
            out = jnp.where(masks[i], o[i * tq:(i + 1) * tq], out)
        o_ref[:, g * w:(g + 1) * w] = out.astype(o_ref.dtype)


def _flash_scratch(ng, rows, w):
    return [pltpu.VMEM((ng, rows, LANES), F32), pltpu.VMEM((ng, rows, w + LANES), F32),
            pltpu.VMEM((ng, rows, LANES), F32)]


def _attn_prompt_kernel(qi_tab, ki_tab, q_ref, k_ref, v_ref, cq_ref, ck_ref, o_ref, *scr, hs):
    step = pl.program_id(1)
    qi, ki = qi_tab[step], ki_tab[step]
    tq, tk = q_ref.shape[0], k_ref.shape[0]
    w = hs * HEAD_DIM
    ng = q_ref.shape[1] // w
    last = (qi * tq + tq - 1) // tk

    @pl.when(ki == 0)
    def _():
        _flash_init(cq_ref[...], hs, *scr)

    def sweep(causal, nkeys):
        ck = ck_ref[0] * LOG2E
        for g in range(ng):
            sl = slice(g * w, (g + 1) * w)
            _flash_slab(g, hs, _stack_heads(q_ref[:, sl]), k_ref[:nkeys, sl], v_ref[:nkeys, sl],
                        [ck[g * hs + i:g * hs + i + 1, :nkeys] for i in range(hs)], causal, *scr)

    @pl.when(ki < last)
    def _():
        sweep(None, tk)

    for off in range(0, tk, tq):
        @pl.when((ki == last) & (qi * tq - ki * tk == off))
        def _():
            sweep(off, off + tq)
            _flash_finish(o_ref, hs, scr[1])


def _attn_prompt(q, kb, vb, c_rows, c_t, nb, s):
    d = q.shape[1]
    nh = d // HEAD_DIM
    hs = ATT_HS
    tq, tk = min(ATT_TQ, s), min(ATT_TK, s)
    assert tk % tq == 0 and s % tk == 0
    nq, nk = s // tq, s // tk
    pairs = [(i, j) for i in range(nq) for j in range((i * tq + tq - 1) // tk + 1)]
    qi_tab = jnp.asarray([p[0] for p in pairs], jnp.int32)
    ki_tab = jnp.asarray([p[1] for p in pairs], jnp.int32)
    grid_spec = pltpu.PrefetchScalarGridSpec(
        num_scalar_prefetch=2, grid=(nb, len(pairs)),
        in_specs=[pl.BlockSpec((tq, d), lambda b, p, qt, kt: (b * nq + qt[p], 0)),
                  pl.BlockSpec((tk, d), lambda b, p, qt, kt: (b * nk + kt[p], 0)),
                  pl.BlockSpec((tk, d), lambda b, p, qt, kt: (b * nk + kt[p], 0)),
                  pl.BlockSpec((tq, nh), lambda b, p, qt, kt: (b * nq + qt[p], 0)),
                  pl.BlockSpec((1, nh, tk), lambda b, p, qt, kt: (b, 0, kt[p]))],
        out_specs=pl.BlockSpec((tq, d), lambda b, p, qt, kt: (b * nq + qt[p], 0)),
        scratch_shapes=_flash_scratch(nh // hs, hs * tq, hs * HEAD_DIM))
    return pl.pallas_call(
        functools.partial(_attn_prompt_kernel, hs=hs), grid_spec=grid_spec,
        out_shape=jax.ShapeDtypeStruct((nb * s, d), BF16),
        compiler_params=_cparams(("parallel", "arbitrary")), name="attn_prompt")(
            qi_tab, ki_tab, q, kb, vb, c_rows, c_t)


def _attn_sample_kernel(q_ref, ck_ref, cv_ref, kn_ref, vn_ref, cq_ref, ckc_ref, ckn_ref, o_ref, *scr, hs, nkb):
    j = pl.program_id(1)
    w = hs * HEAD_DIM
    ng = q_ref.shape[1] // w

    @pl.when(j == 0)
    def _():
        _flash_init(cq_ref[0], hs, *scr)

    def sweep(kslab, vslab, ck, causal, kv_t):
        ck = ck * LOG2E
        for g in range(ng):
            _flash_slab(g, hs, _stack_heads(q_ref[:, g * w:(g + 1) * w]), kslab(g), vslab(g),
                        [ck[g * hs + i:g * hs + i + 1, :] for i in range(hs)], causal, *scr, kv_t=kv_t)

    @pl.when(j < nkb)
    def _():
        sweep(lambda g: ck_ref[0, g * w:(g + 1) * w, :].astype(BF16),
              lambda g: cv_ref[0, g * w:(g + 1) * w, :].astype(BF16), ckc_ref[0], None, True)

    @pl.when(j == nkb)
    def _():
        sweep(lambda g: kn_ref[:, g * w:(g + 1) * w], lambda g: vn_ref[:, g * w:(g + 1) * w], ckn_ref[0], 0, False)
        _flash_finish(o_ref, hs, scr[1])


def _attn_sample(q, kb, vb, cache_kt, cache_vt, c_new, c_past_t, c_new_t, n_prompt):
    d = q.shape[1]
    nh = d // HEAD_DIM
    nb, _, plen = cache_kt.shape
    lq = c_new.shape[1]
    hs = DEC_HS
    tk = min(DEC_TK, plen)
    nkb = plen // tk
    r0 = n_prompt // lq
    new_rows = pl.BlockSpec((lq, d), lambda b, j: (r0 + b, 0))
    cache = pl.BlockSpec((1, d, tk), lambda b, j: (b, 0, jnp.minimum(j, nkb - 1)))
    kern = functools.partial(_attn_sample_kernel, hs=hs, nkb=nkb)
    return pl.pallas_call(
        kern, grid=(nb, nkb + 1),
        in_specs=[new_rows, cache, cache, new_rows, new_rows,
                  pl.BlockSpec((1, lq, nh), lambda b, j: (b, 0, 0)),
                  pl.BlockSpec((1, nh, tk), lambda b, j: (b, 0, jnp.minimum(j, nkb - 1))),
                  pl.BlockSpec((1, nh, lq), lambda b, j: (b, 0, 0))],
        out_specs=pl.BlockSpec((lq, d), lambda b, j: (b, 0)),
        out_shape=jax.ShapeDtypeStruct((nb * lq, d), BF16),
        scratch_shapes=_flash_scratch(nh // hs, hs * lq, hs * HEAD_DIM),
        compiler_params=_cparams(("parallel", "arbitrary")), name="attn_sample")(
            q, cache_kt, cache_vt, kb, vb, c_new, c_past_t, c_new_t)


def kernel(x_prompt, x_sample, cache_k, cache_v, cache_logf, state_ssm_re, state_ssm_im, ffn_norm, w_ffn_gate, w_ffn_up, w_ffn_down, mix_norm, ssm_a_re, ssm_a_im, ssm_log_dt, ssm_b_re, ssm_b_im, ssm_c_re, ssm_c_im, ssm_d, w_glu_a, w_glu_b, kv_norm, w_kvf, b_f, k_norm, w_q, q_norm, w_o):
    bp, s, d = x_prompt.shape
    bs, lq, _ = x_sample.shape
    plen = cache_k.shape[1]
    nh = d // HEAD_DIM
    assert ffn_norm.shape[0] == 2 and ssm_a_re.shape[0] == 1 and w_q.shape[0] == 1
    n_prompt, n_sample = bp * s, bs * lq
    bf = lambda w: w.astype(BF16)

    def ffn(xs, l, j, **kw):
        return _ffn(xs, ffn_norm[l, j], w_ffn_gate, w_ffn_up, w_ffn_down, (l, j), **kw)

    x, u = ffn([x_prompt.reshape(n_prompt, d), x_sample.reshape(n_sample, d)], 0, 0, g2=mix_norm[0], norm_dtype=F32)
    mats = _s5_params(ssm_log_dt[0], ssm_a_re[0], ssm_a_im[0], ssm_b_re[0], ssm_b_im[0], ssm_c_re[0], ssm_c_im[0])
    z, (p_re, p_im, s_re, s_im) = _s5_layer(u, n_prompt, bp, bs, state_ssm_re[:, 0], state_ssm_im[:, 0], mats, ssm_d[0])
    x = _glu(x, z, bf(w_glu_a[0]), bf(w_glu_b[0]))
    x, hn = ffn([x], 0, 1, g2=kv_norm)

    kt_p, k_s, vt_p, v_s, lft_p, lft_s, kb, vb = _kvf(hn, bp, s, lq, bf(w_kvf[:, :2 * d]), bf(w_kvf[:, 2 * d:]), b_f, k_norm)

    def cumsum_t(lft, init):
        b, _, l = lft.shape
        init = jnp.zeros((b * nh, 1), F32) if init is None else init.reshape(b * nh, 1)
        return _cumsum_rows(lft.reshape(b * nh, l), init).reshape(b, nh, l)

    cp_t = cumsum_t(lft_p, None)
    cpast_t = cumsum_t(jnp.swapaxes(cache_logf.astype(F32), 1, 2), None)
    cnew_t = cumsum_t(lft_s, cpast_t[:, :, -1])

    x, un = ffn([x], 1, 0, g2=mix_norm[1])
    q = _qproj(un, bf(w_q[0]), (jnp.tile(q_norm[0], nh) * (LOG2E / math.sqrt(HEAD_DIM))).reshape(1, d))
    a_p = _attn_prompt(q, kb, vb, jnp.swapaxes(cp_t, 1, 2).reshape(n_prompt, nh), cp_t, bp, s)
    pos_minor = lambda c: jnp.transpose(c, (0, 2, 3, 1)).reshape(bs, d, plen)
    a_s = _attn_sample(q, kb, vb, pos_minor(cache_k), pos_minor(cache_v),
                       jnp.swapaxes(cnew_t, 1, 2), cpast_t, cnew_t, n_prompt)
    x = _oproj(x, a_p, a_s, bf(w_o[0]))
    (y_p, y_s), _ = ffn([x], 1, 1, split_out=(n_prompt, n_sample))

    cache_p = lambda a: jnp.transpose(a.reshape(bp, nh, HEAD_DIM, s), (0, 3, 1, 2))
    cache_s = lambda a: a.reshape(bs, lq, nh, HEAD_DIM)
    return (y_p.reshape(bp, s, d), y_s.reshape(bs, lq, d), p_re, p_im, cache_p(kt_p), cache_p(vt_p),
            jnp.swapaxes(lft_p, 1, 2), s_re, s_im, cache_s(k_s), cache_s(v_s), jnp.swapaxes(lft_s, 1, 2))
```

```python
import functools
import math

import jax
import jax.numpy as jnp
import numpy as np
from jax import lax
from jax.experimental import pallas as pl
from jax.experimental.pallas import tpu as pltpu

F32 = jnp.float32
BF16 = jnp.bfloat16

EPS = 1e-6
HEAD_DIM = 64
SSM_GROUP = 16
S5_CHUNK = 16
LANES = 128
_SLAB_GROUPS = LANES // SSM_GROUP
NEG_BIG = -1e30
LOG2E = math.log2(math.e)
VMEM_LIMIT = 57 * 1024 * 1024

FFN_TM = 1024
FFN_TF = 512
TOK_TM = 512
MAT_TM = 1024
ATT_TQ = 256
ATT_TK = 512
ATT_HS = 2
DEC_TK = 1024
DEC_HS = 4


def _cparams(sem):
    return pltpu.CompilerParams(dimension_semantics=sem, vmem_limit_bytes=VMEM_LIMIT)


def _rms_scale(x):
    return lax.rsqrt(jnp.mean(x * x, axis=-1, keepdims=True) + EPS)


def _dot(a, b):
    return jnp.dot(a, b, preferred_element_type=F32)


def _split2(x):
    hi = x.astype(BF16)
    lo = (x - hi.astype(F32)).astype(BF16)
    return hi, lo


def _split3(x):
    hi = x.astype(BF16)
    r1 = x - hi.astype(F32)
    mid = r1.astype(BF16)
    return hi, mid, (r1 - mid.astype(F32)).astype(BF16)


def _tile(limit, *sizes):
    while any(s % limit for s in sizes):
        limit //= 2
    return limit


def _part_maps(npb):
    first = lambda i, *_: (jnp.minimum(i, npb - 1), 0)
    second = lambda i, *_: (jnp.maximum(i - npb, 0), 0)
    return first, second


def _ffn_kernel(*refs, nf, npb, n_in, n_out, with_norm):
    x_refs, refs = refs[:n_in], refs[n_in:]
    g_ref, wg_ref, wu_ref, wd_ref = refs[:4]
    refs = refs[4:]
    g2_ref = refs[0] if with_norm else None
    refs = refs[with_norm:]
    o_refs, refs = refs[:n_out], refs[n_out:]
    o2_ref = refs[0] if with_norm else None
    h_scr, acc_scr = refs[with_norm:]
    i, f = pl.program_id(0), pl.program_id(1)

    def with_x(cond, fn):
        if n_in == 1:
            pl.when(cond)(lambda: fn(x_refs[0][...]))
        else:
            pl.when(cond & (i < npb))(lambda: fn(x_refs[0][...]))
            pl.when(cond & (i >= npb))(lambda: fn(x_refs[1][...]))

    def start(x):
        h_scr[...] = (x * _rms_scale(x) * g_ref[...]).astype(BF16)
        acc_scr[...] = jnp.zeros_like(acc_scr)

    with_x(f == 0, start)

    h = h_scr[...]
    a = _dot(h, wg_ref[...].astype(BF16))
    b = _dot(h, wu_ref[...].astype(BF16))
    t = (a * jax.nn.sigmoid(a)) * b
    acc_scr[...] += _dot(t.astype(BF16), wd_ref[...].astype(BF16))

    def finish(x):
        y = x + 0.5 * acc_scr[...]
        if n_out == 1:
            o_refs[0][...] = y
        else:
            @pl.when(i < npb)
            def _():
                o_refs[0][...] = y

            @pl.when(i >= npb)
            def _():
                o_refs[1][...] = y
        if with_norm:
            o2_ref[...] = (y * _rms_scale(y) * g2_ref[...]).astype(o2_ref.dtype)

    with_x(f == nf - 1, finish)


def _ffn(xs, g, wg, wu, wd, lj, g2=None, split_out=None, norm_dtype=BF16):
    d = xs[0].shape[1]
    n = sum(x.shape[0] for x in xs)
    dff = wg.shape[-1]
    l, j = lj
    tm, tf = min(FFN_TM, n), min(FFN_TF, dff)
    nf = dff // tf
    n_first = xs[0].shape[0] if len(xs) == 2 else (split_out[0] if split_out else n)
    assert all(x.shape[0] % tm == 0 for x in xs) and n_first % tm == 0
    npb = n_first // tm
    first, second = _part_maps(npb)
    with_norm = g2 is not None
    row = pl.BlockSpec((tm, d), lambda i, f: (i, 0))
    vec = pl.BlockSpec((1, d), lambda i, f: (0, 0))
    parts = [pl.BlockSpec((tm, d), first), pl.BlockSpec((tm, d), second)]
    in_specs = (parts if len(xs) == 2 else [row]) + [
        vec, pl.BlockSpec((None, None, d, tf), lambda i, f: (l, j, 0, f)),
        pl.BlockSpec((None, None, d, tf), lambda i, f: (l, j, 0, f)),
        pl.BlockSpec((None, None, tf, d), lambda i, f: (l, j, f, 0))]
    args = list(xs) + [g.reshape(1, d), wg, wu, wd]
    if split_out:
        out_shape = [jax.ShapeDtypeStruct((m, d), F32) for m in split_out]
        out_specs = list(parts)
    else:
        out_shape = [jax.ShapeDtypeStruct((n, d), F32)]
        out_specs = [row]
    n_out = len(out_shape)
    if with_norm:
        in_specs.append(vec)
        args.append(g2.reshape(1, d))
        out_shape.append(jax.ShapeDtypeStruct((n, d), norm_dtype))
        out_specs.append(row)
    kern = functools.partial(_ffn_kernel, nf=nf, npb=npb, n_in=len(xs), n_out=n_out, with_norm=with_norm)
    res = pl.pallas_call(
        kern, grid=(n // tm, nf), in_specs=in_specs, out_specs=out_specs, out_shape=out_shape,
        scratch_shapes=[pltpu.VMEM((tm, d), BF16), pltpu.VMEM((tm, d), F32)],
        compiler_params=_cparams(("arbitrary", "arbitrary")), name="ffn")(*args)
    y = tuple(res[:n_out]) if split_out else res[0]
    return y, (res[n_out] if with_norm else None)


def _s5_param_kernel(*refs):
    lax.fori_loop(0, _SLAB_GROUPS, functools.partial(_s5_param_group, refs), 0)


def _s5_param_group(refs, slot, carry):
    (ldt_ref, arc_ref, aic_ref, arr_ref, air_ref, btr_ref, bti_ref, ctr_ref, cti_ref,
     m_ref, sbr_ref, sbi_ref, car_ref, cai_ref, a16r_ref, a16i_ref) = refs
    t, c = S5_CHUNK, SSM_GROUP
    dt = jnp.exp(ldt_ref[slot])

    def cpow(ar, ai, n):
        mag = jnp.exp(ar * dt)
        pr, pi = mag * jnp.cos(ai * dt), mag * jnp.sin(ai * dt)
        n = jnp.asarray(n, jnp.int32)
        shape = jnp.broadcast_shapes(ar.shape, n.shape)
        re, im = jnp.ones(shape, F32), jnp.zeros(shape, F32)
        for b in range(t.bit_length()):
            bit = ((n >> b) & 1) == 1
            re, im = jnp.where(bit, re * pr - im * pi, re), jnp.where(bit, re * pi + im * pr, im)
            pr, pi = pr * pr - pi * pi, 2.0 * pr * pi
        return re, im

    def step_of(pos):
        seg = pos // c
        return (seg // _SLAB_GROUPS) * _SLAB_GROUPS + ((seg - slot) & (_SLAB_GROUPS - 1))

    arc, aic = arc_ref[slot], aic_ref[slot]
    arr, air = arr_ref[slot], air_ref[slot]
    p = arc.shape[0]
    lane_pc = lax.broadcasted_iota(jnp.int32, (p, t * c), 1)
    cr, ci = ctr_ref[slot], cti_ref[slot]
    pr, pi = cpow(arc, aic, step_of(lane_pc) + 1)
    car_ref[slot] = (cr * pr - ci * pi).astype(BF16)
    cai_ref[slot] = (-(cr * pi + ci * pr)).astype(BF16)
    pr, pi = cpow(arc, aic, lane_pc // c)
    csr = cr * pr - ci * pi
    csi = cr * pi + ci * pr

    abr, abi = cpow(arr, air, 1)
    xr, xi = abr - 1.0, abi
    den = arr * arr + air * air
    qr = (xr * arr + xi * air) / den
    qi = (xi * arr - xr * air) / den
    btr, bti = btr_ref[slot], bti_ref[slot]
    bbr = qr * btr - qi * bti
    bbi = qr * bti + qi * btr
    iidx = step_of(lax.broadcasted_iota(jnp.int32, (t * c, p), 0))
    pr, pi = cpow(arr, air, (t - 1) - iidx)
    sbr_ref[slot] = (bbr * pr - bbi * pi).astype(BF16)
    sbi_ref[slot] = (bbr * pi + bbi * pr).astype(BF16)
    pr, pi = cpow(arr, air, t)
    a16r_ref[slot] = pr
    a16i_ref[slot] = pi

    hp = lax.Precision.HIGHEST
    kt = (jnp.dot(bbr[:c], csr, precision=hp, preferred_element_type=F32)
          - jnp.dot(bbi[:c], csi, precision=hp, preferred_element_type=F32))
    lane = lax.broadcasted_iota(jnp.int32, (c, t * c), 1)
    for i in range(t):
        blk = kt if i == 0 else pltpu.roll(kt, i * c, 1)
        blk = jnp.where(lane >= i * c, blk, 0.0)
        blk = jnp.concatenate([pltpu.roll(blk[:, h * LANES:(h + 1) * LANES], slot * c, 1)
                               for h in range(t * c // LANES)], axis=1)
        seg = (i // _SLAB_GROUPS) * _SLAB_GROUPS + ((i + slot) & (_SLAB_GROUPS - 1))
        m_ref[slot, pl.ds(pl.multiple_of(seg * c, c), c), :] = blk.astype(BF16)
    return carry


def _s5_params(log_dt, a_re, a_im, b_re, b_im, c_re, c_im):
    g, p = a_re.shape
    t, c = S5_CHUNK, SSM_GROUP
    tc = t * c

    def tile_b(b):
        return jnp.broadcast_to(jnp.swapaxes(b, 1, 2)[:, None], (g, t, c, p)).reshape(g, tc, p)

    def tile_c(cm):
        return jnp.broadcast_to(jnp.swapaxes(cm, 1, 2)[:, :, None], (g, p, t, c)).reshape(g, p, tc)

    def spec(*shape):
        return pl.BlockSpec((_SLAB_GROUPS,) + shape, lambda i: (i, 0, 0))

    assert g % _SLAB_GROUPS == 0
    return pl.pallas_call(
        _s5_param_kernel, grid=(g // _SLAB_GROUPS,),
        in_specs=[spec(1, 1), spec(p, 1), spec(p, 1), spec(1, p), spec(1, p),
                  spec(tc, p), spec(tc, p), spec(p, tc), spec(p, tc)],
        out_specs=[spec(tc, tc), spec(tc, p), spec(tc, p), spec(p, tc), spec(p, tc), spec(1, p), spec(1, p)],
        out_shape=[jax.ShapeDtypeStruct((g, tc, tc), BF16),
                   jax.ShapeDtypeStruct((g, tc, p), BF16), jax.ShapeDtypeStruct((g, tc, p), BF16),
                   jax.ShapeDtypeStruct((g, p, tc), BF16), jax.ShapeDtypeStruct((g, p, tc), BF16),
                   jax.ShapeDtypeStruct((g, 1, p), F32), jax.ShapeDtypeStruct((g, 1, p), F32)],
        compiler_params=_cparams(("parallel",)), name="s5_params")(
            log_dt.reshape(g, 1, 1), a_re.reshape(g, p, 1), a_im.reshape(g, p, 1),
            a_re.reshape(g, 1, p), a_im.reshape(g, 1, p),
            tile_b(b_re), tile_b(b_im), tile_c(c_re), tile_c(c_im))


def _gelu_tanh(y):
    return 0.5 * y * (1.0 + jnp.tanh(math.sqrt(2.0 / math.pi) * (y + 0.044715 * (y * y * y))))


def _pick_segments(v, first):
    seg = lax.broadcasted_iota(jnp.int32, v[0].shape, 1) // SSM_GROUP
    out = v[(7 + first) % 8]
    for s in range(6, -1, -1):
        out = jnp.where(seg == s, v[(s + first) % 8], out)
    return out


def _s5_main_kernel(u_ref, m_ref, sbr_ref, sbi_ref, car_ref, cai_ref, a16r_ref, a16i_ref,
                    h0r_ref, h0i_ref, d_ref, z_ref, hpr_ref, hpi_ref, hsr_ref, hsi_ref,
                    u2_scr, sr_scr, si_scr, hr_scr, hi_scr, *, bp, nkp, bs, nks):
    t = S5_CHUNK
    npq = bp * nkp // 8
    nq = npq + bs * nks // 8
    sp = nkp * t

    def token0(q):
        return jnp.where(q < npq, (q % bp) * sp + (q // bp) * (8 * t), bp * sp + (q - npq) * (8 * t))

    def gather(q, carry):
        tok, rows = token0(q), pl.ds(pl.multiple_of(q * 8, 8), 8)
        for h in range(2):
            v = [u_ref[pl.ds(tok + 8 * h + i, 8, stride=t), :] for i in range(8)]
            v = [x if i == 0 else pltpu.roll(x, i * SSM_GROUP, 1) for i, x in enumerate(v)]
            for gi in range(8):
                u2_scr[gi, rows, h * LANES:(h + 1) * LANES] = _pick_segments(v, 8 - gi)
        return carry

    lax.fori_loop(0, nq, gather, 0, unroll=4)

    def group(gi, buf):
        sr, si, hrs, his = sr_scr.at[buf], si_scr.at[buf], hr_scr.at[buf], hi_scr.at[buf]
        u = u2_scr[gi]
        ub = u.astype(BF16)
        sr[...] = _dot(ub, sbr_ref[gi])
        si[...] = _dot(ub, sbi_ref[gi])
        ar, ai = a16r_ref[gi], a16i_ref[gi]

        def step(rows, hr, hi):
            hrs[rows, :] = hr
            his[rows, :] = hi
            return (ar * hr - ai * hi + sr[rows, :], ar * hi + ai * hr + si[rows, :])

        hr = hi = jnp.zeros((bp, ar.shape[1]), F32)
        for k in range(nkp):
            hr, hi = step(pl.ds((k // 8) * bp * 8 + k % 8, bp, stride=8), hr, hi)
        hpr_ref[gi] = hr
        hpi_ref[gi] = hi
        hr, hi = h0r_ref[gi], h0i_ref[gi]
        for k in range(nks):
            hr, hi = step(pl.ds(bp * nkp + k, bs, stride=nks), hr, hi)
        hsr_ref[gi] = hr
        hsi_ref[gi] = hi

        y = (_dot(ub, m_ref[gi]) + _dot(hrs[...].astype(BF16), car_ref[gi])
             + _dot(his[...].astype(BF16), cai_ref[gi]) + d_ref[gi] * u)
        u2_scr[gi] = _gelu_tanh(y)

    def group_pair(j, carry):
        group(2 * j, 0)
        group(2 * j + 1, 1)
        return carry

    lax.fori_loop(0, _SLAB_GROUPS // 2, group_pair, 0)

    def scatter(q, carry):
        tok, rows = token0(q), pl.ds(pl.multiple_of(q * 8, 8), 8)
        for h in range(2):
            v = [u2_scr[gi, rows, h * LANES:(h + 1) * LANES] for gi in range(8)]
            for i in range(8):
                w = _pick_segments(v, 8 - i)
                w = w if i == 0 else pltpu.roll(w, LANES - i * SSM_GROUP, 1)
                z_ref[pl.ds(tok + 8 * h + i, 8, stride=t), :] = w
        return carry

    lax.fori_loop(0, nq, scatter, 0, unroll=4)


def _s5_layer(u, n_prompt, bp, bs, h0_re, h0_im, mats, d_skip):
    n, d = u.shape
    t, c = S5_CHUNK, SSM_GROUP
    g, tc = d // c, t * c
    gs = LANES // c
    p = h0_re.shape[-1]
    nkp = n_prompt // (bp * t)
    nks = (n - n_prompt) // (bs * t)
    r = nkp * bp + nks * bs
    assert gs == 8 and t == 16 and bp % 8 == 0 and nkp % 8 == 0 and (bs * nks) % 8 == 0

    m, sbr, sbi, car, cai, a16r, a16i = mats
    d_t = jnp.broadcast_to(d_skip.reshape(g, 1, 1, c), (g, 1, t, c)).reshape(g, 1, tc)
    h0r = jnp.swapaxes(h0_re, 0, 1)
    h0i = jnp.swapaxes(h0_im, 0, 1)

    def spec(*shape):
        return pl.BlockSpec((gs,) + shape, lambda i: (i, 0, 0))

    kern = functools.partial(_s5_main_kernel, bp=bp, nkp=nkp, bs=bs, nks=nks)
    z, hpr, hpi, hsr, hsi = pl.pallas_call(
        kern, grid=(g // gs,),
        in_specs=[pl.BlockSpec((n, LANES), lambda i: (0, i), pipeline_mode=pl.Buffered(1)),
                  spec(tc, tc), spec(tc, p), spec(tc, p), spec(p, tc), spec(p, tc),
                  spec(1, p), spec(1, p), spec(bs, p), spec(bs, p), spec(1, tc)],
        out_specs=[pl.BlockSpec((n, LANES), lambda i: (0, i)),
                   spec(bp, p), spec(bp, p), spec(bs, p), spec(bs, p)],
        out_shape=[jax.ShapeDtypeStruct((n, d), F32),
                   jax.ShapeDtypeStruct((g, bp, p), F32), jax.ShapeDtypeStruct((g, bp, p), F32),
                   jax.ShapeDtypeStruct((g, bs, p), F32), jax.ShapeDtypeStruct((g, bs, p), F32)],
        scratch_shapes=[pltpu.VMEM((gs, r, tc), F32)] + [pltpu.VMEM((2, r, p), F32) for _ in range(4)],
        compiler_params=_cparams(("arbitrary",)), name="s5_main")(
            u, m, sbr, sbi, car, cai, a16r, a16i, h0r, h0i, d_t)
    states = tuple(jnp.swapaxes(h, 0, 1)[:, None] for h in (hpr, hpi, hsr, hsi))
    return z, states


def _glu_kernel(x_ref, z_ref, wa_ref, wb_ref, o_ref):
    z = z_ref[...].astype(BF16)
    o_ref[...] = x_ref[...] + _dot(z, wa_ref[...]) * jax.nn.sigmoid(_dot(z, wb_ref[...]))


def _glu(x, z, wa, wb):
    n, d = x.shape
    tm = _tile(MAT_TM, n)
    row = pl.BlockSpec((tm, d), lambda i: (i, 0))
    mat =pl.BlockSpec((d, d), lambda i: (0, 0))
    return pl.pallas_call(
        _glu_kernel, grid=(n // tm,), in_specs=[row, row, mat, mat], out_specs=row,
        out_shape=jax.ShapeDtypeStruct((n, d), F32),
        compiler_params=_cparams(("parallel",)), name="glu")(x, z, wa, wb)


def _head_norm(x, e_ref, et_ref):
    ms = _dot((x * x).astype(BF16), e_ref[...]) * (1.0 / HEAD_DIM)
    hi, lo = _split2(lax.rsqrt(ms + EPS))
    return _dot(hi, et_ref[...]) + _dot(lo, et_ref[...])


def _head_indicator(d):
    nh = d // HEAD_DIM
    e = (np.arange(d)[:, None] // HEAD_DIM == np.arange(LANES)[None, :]).astype(np.float32)
    assert nh <= LANES
    return jnp.asarray(e, BF16), jnp.asarray(e.T, BF16)


def _kvf_kernel(h_ref, wkv_ref, wf_ref, bf_ref, kn_ref, e_ref, et_ref,
                kp_ref, ks_ref, vp_ref, vs_ref, lp_ref, ls_ref, kb_ref, vb_ref, *, d, nh, npb):
    i = pl.program_id(0)
    h = h_ref[...]
    tm = h.shape[0]
    pkv = _dot(h, wkv_ref[...])
    kraw, v = pkv[:, :d], pkv[:, d:]
    k = kraw * _head_norm(kraw, e_ref, et_ref) * kn_ref[...]
    kb_ref[...] = k.astype(BF16)
    vb_ref[...] = v.astype(BF16)
    pf = _dot(h, wf_ref[...]) + bf_ref[...]
    lft = (jnp.minimum(pf, 0.0) - jnp.log1p(jnp.exp(-jnp.abs(pf)))).T[:nh]

    @pl.when(i < npb)
    def _():
        kp_ref[0] = k.T
        vp_ref[0] = v.T
        lp_ref[0] = lft

    @pl.when(i >= npb)
    def _():
        for hd in range(nh):
            rows = pl.ds(hd, tm, stride=nh)
            ks_ref[rows, :] = k[:, hd * HEAD_DIM:(hd + 1) * HEAD_DIM]
            vs_ref[rows, :] = v[:, hd * HEAD_DIM:(hd + 1) * HEAD_DIM]
        lq = ls_ref.shape[2]
        for j in range(tm // lq):
            ls_ref[j] = lft[:, j * lq:(j + 1) * lq]


def _kvf(hn, bp, s, lq, w_kv, w_f, b_f, k_norm):
    n, d = hn.shape
    nh = d // HEAD_DIM
    tm = min(TOK_TM, s)
    n_prompt = bp * s
    n_s = n - n_prompt
    assert s % tm == 0 and n_s % tm == 0 and tm % lq == 0
    npb, spb = n_prompt // tm, s // tm
    e, et = _head_indicator(d)
    row = pl.BlockSpec((tm, d), lambda i: (i, 0))
    const = lambda *sh: pl.BlockSpec(sh, lambda i: (0, 0))
    prompt_t = lambda rows: pl.BlockSpec(
        (1, rows, tm), lambda i: (jnp.minimum(i, npb - 1) // spb, 0, jnp.minimum(i, npb - 1) % spb))
    second = lambda i: (jnp.maximum(i - npb, 0), 0)
    kern = functools.partial(_kvf_kernel, d=d, nh=nh, npb=npb)
    return pl.pallas_call(
        kern, grid=(n // tm,),
        in_specs=[row, const(d, 2 * d), const(d, LANES), const(1, LANES), const(1, d), const(d, LANES), const(LANES, d)],
        out_specs=[prompt_t(d), pl.BlockSpec((tm * nh, HEAD_DIM), second),
                   prompt_t(d), pl.BlockSpec((tm * nh, HEAD_DIM), second),
                   prompt_t(nh), pl.BlockSpec((tm // lq, nh, lq), lambda i: (jnp.maximum(i - npb, 0), 0, 0)), row, row],
        out_shape=[jax.ShapeDtypeStruct((bp, d, s), F32), jax.ShapeDtypeStruct((n_s * nh, HEAD_DIM), F32),
                   jax.ShapeDtypeStruct((bp, d, s), F32), jax.ShapeDtypeStruct((n_s * nh, HEAD_DIM), F32),
                   jax.ShapeDtypeStruct((bp, nh, s), F32), jax.ShapeDtypeStruct((n_s // lq, nh, lq), F32),
                   jax.ShapeDtypeStruct((n, d), BF16), jax.ShapeDtypeStruct((n, d), BF16)],
        compiler_params=_cparams(("arbitrary",)), name="kvf")(
            hn, w_kv, jnp.pad(w_f, ((0, 0), (0, LANES - nh))), jnp.pad(b_f, (0, LANES - nh)).reshape(1, LANES),
            jnp.tile(k_norm, nh).reshape(1, d), e, et)


def _qproj_kernel(u_ref, wq_ref, qn_ref, e_ref, et_ref, q_ref):
    q = _dot(u_ref[...], wq_ref[...])
    q_ref[...] = (q * _head_norm(q, e_ref, et_ref) * qn_ref[...]).astype(BF16)


def _qproj(un, wq, qn_scaled):
    n, d = un.shape
    tm = _tile(MAT_TM, n)
    e, et = _head_indicator(d)
    row = pl.BlockSpec((tm, d), lambda i: (i, 0))
    const = lambda *s: pl.BlockSpec(s, lambda i: (0, 0))
    return pl.pallas_call(
        _qproj_kernel, grid=(n // tm,),
        in_specs=[row, const(d, d), const(1, d), const(d, LANES), const(LANES, d)], out_specs=row,
        out_shape=jax.ShapeDtypeStruct((n, d), BF16),
        compiler_params=_cparams(("parallel",)), name="qproj")(un, wq, qn_scaled, e, et)


def _oproj_kernel(x_ref, ap_ref, as_ref, wo_ref, o_ref, *, npb):
    a = jnp.where(pl.program_id(0) < npb, ap_ref[...], as_ref[...])
    o_ref[...] = x_ref[...] + _dot(a, wo_ref[...])


def _oproj(x, a_p, a_s, wo):
    n, d = x.shape
    tm = _tile(MAT_TM, a_p.shape[0], a_s.shape[0])
    assert a_p.shape[0] % tm == 0 and a_s.shape[0] % tm == 0
    npb = a_p.shape[0] // tm
    first, second = _part_maps(npb)
    row = pl.BlockSpec((tm, d), lambda i: (i, 0))
    return pl.pallas_call(
        functools.partial(_oproj_kernel, npb=npb), grid=(n // tm,),
        in_specs=[row, pl.BlockSpec((tm, d), first), pl.BlockSpec((tm, d), second),
                  pl.BlockSpec((d, d), lambda i: (0, 0))],
        out_specs=row, out_shape=jax.ShapeDtypeStruct((n, d), F32),
        compiler_params=_cparams(("parallel",)), name="oproj")(x, a_p, a_s, wo)


def _cumsum_kernel(x_ref, init_ref, o_ref, *, nblk, bw):
    tri = (lax.broadcasted_iota(jnp.int32, (bw, bw), 0) <= lax.broadcasted_iota(jnp.int32, (bw, bw), 1)).astype(BF16)
    carry = init_ref[...]
    for j in range(nblk):
        x = x_ref[:, j * bw:(j + 1) * bw]
        hi, mid, lo = _split3(x)
        cs = _dot(hi, tri) + _dot(mid, tri) + _dot(lo, tri) + carry
        o_ref[:, j * bw:(j + 1) * bw] = cs
        carry = cs[:, bw - 1:bw]


def _cumsum_rows(x, init):
    r, w = x.shape
    bw = min(256, w)
    kern = functools.partial(_cumsum_kernel, nblk=w // bw, bw=bw)
    return pl.pallas_call(
        kern, grid=(1,),
        in_specs=[pl.BlockSpec((r, w), lambda i: (0, 0)), pl.BlockSpec((r, 1), lambda i: (0, 0))],
        out_specs=pl.BlockSpec((r, w), lambda i: (0, 0)), out_shape=jax.ShapeDtypeStruct((r, w), F32),
        compiler_params=_cparams(("arbitrary",)), name="cumsum")(x, init)


def _lanes(x, w):
    if w <= LANES:
        return x[:, :w]
    return jnp.concatenate([x] * (w // LANES), axis=1)


def _head_masks(shape):
    head = lax.broadcasted_iota(jnp.int32, shape, 1) // HEAD_DIM
    return [head == i for i in range(shape[1] // HEAD_DIM)]


def _stack_heads(qs):
    zero = jnp.zeros_like(qs)
    return jnp.concatenate([jnp.where(mk, qs, zero) for mk in _head_masks(qs.shape)], axis=0)


def _flash_init(cq, hs, m_scr, acc_scr, cq_scr):
    tq = cq.shape[0]
    cq = cq * LOG2E
    m_scr[...] = jnp.full_like(m_scr, NEG_BIG)
    acc_scr[...] = jnp.zeros_like(acc_scr)
    for g in range(cq_scr.shape[0]):
        for i in range(hs):
            h = g * hs + i
            cq_scr[g, i * tq:(i + 1) * tq, :] = jnp.broadcast_to(cq[:, h:h + 1], (tq, LANES))


def _flash_slab(g, hs, q_st, ks, vs, ck_rows, causal, m_scr, acc_scr, cq_scr, kv_t=False):
    tq, tk = q_st.shape[0] // hs, (ks.shape[1] if kv_t else ks.shape[0])
    nt = (((1,), (1,)), ((), ()))
    t = _dot(q_st, ks) if kv_t else lax.dot_general(q_st, ks, nt, preferred_element_type=F32)
    if ck_rows is not None:
        t = jnp.concatenate([t[i * tq:(i + 1) * tq] - ck_rows[i] for i in range(hs)], axis=0)
    if causal is not None:
        keep = lax.broadcasted_iota(jnp.int32, (tq, tk), 1) <= lax.broadcasted_iota(jnp.int32, (tq, tk), 0) + causal
        t = jnp.where(jnp.concatenate([keep] * hs, axis=0), t, NEG_BIG)
    cq = cq_scr[g]
    m_prev = m_scr[g]
    m_new = jnp.maximum(m_prev, jnp.max(t, axis=1, keepdims=True) + cq)
    alpha = jnp.exp2(m_prev - m_new)
    pe = jnp.exp2(t - _lanes(m_new - cq, tk))
    w = hs * HEAD_DIM
    if w == LANES and not kv_t:
        pv = _dot(pe.astype(BF16), jnp.concatenate([vs, jnp.ones((tk, LANES), BF16)], axis=1))
    else:
        pv = lax.dot_general(pe.astype(BF16), vs, nt, preferred_element_type=F32) if kv_t else _dot(pe.astype(BF16), vs)
        pv = jnp.concatenate([pv, jnp.broadcast_to(jnp.sum(pe, axis=1, keepdims=True), (hs * tq, LANES))], axis=1)
    acc_scr[g] = _lanes(alpha, w + LANES) * acc_scr[g] + pv
    m_scr[g] = m_new


def _flash_finish(o_ref, hs, acc_scr):
    tq = o_ref.shape[0]
    w = hs * HEAD_DIM
    masks = _head_masks((tq, w))
    for g in range(acc_scr.shape[0]):
        acc = acc_scr[g]
        o = acc[:, :w] / _lanes(acc[:, w:], w)
        out = o[(hs - 1) * tq:]
        for i in range(hs - 2, -1, -1):
            out = jnp.where(masks[i], o[i * tq:(i + 1) * tq], out)
        o_ref[:, g * w:(g + 1) * w] = out.astype(o_ref.dtype)


def _flash_scratch(ng, rows, w):
    return [pltpu.VMEM((ng, rows, LANES), F32), pltpu.VMEM((ng, rows, w + LANES), F32),
            pltpu.VMEM((ng, rows, LANES), F32)]


def _attn_prompt_kernel(qi_tab, ki_tab, q_ref, k_ref, v_ref, cq_ref, ck_ref, o_ref, *scr, hs):
    step = pl.program_id(1)
    qi, ki = qi_tab[step], ki_tab[step]
    tq, tk = q_ref.shape[0], k_ref.shape[0]
    w = hs * HEAD_DIM
    ng = q_ref.shape[1] // w
    last = (qi * tq + tq - 1) // tk

    @pl.when(ki == 0)
    def _():
        _flash_init(cq_ref[...], hs, *scr)

    def sweep(causal, nkeys):
        ck = ck_ref[0] * LOG2E
        for g in range(ng):
            sl = slice(g * w, (g + 1) * w)
            _flash_slab(g, hs, _stack_heads(q_ref[:, sl]), k_ref[:nkeys, sl], v_ref[:nkeys, sl],
                        [ck[g * hs + i:g * hs + i + 1, :nkeys] for i in range(hs)], causal, *scr)

    @pl.when(ki < last)
    def _():
        sweep(None, tk)

    for off in range(0, tk, tq):
        @pl.when((ki == last) & (qi * tq - ki * tk == off))
        def _():
            sweep(off, off + tq)
            _flash_finish(o_ref, hs, scr[1])


def _attn_prompt(q, kb, vb, c_rows, c_t, nb, s):
    d = q.shape[1]
    nh = d // HEAD_DIM
    hs = ATT_HS
    tq, tk = min(ATT_TQ, s), min(ATT_TK, s)
    assert tk % tq == 0 and s % tk == 0
    nq, nk = s // tq, s // tk
    pairs = [(i, j) for i in range(nq) for j in range((i * tq + tq - 1) // tk + 1)]
    qi_tab = jnp.asarray([p[0] for p in pairs], jnp.int32)
    ki_tab = jnp.asarray([p[1] for p in pairs], jnp.int32)
    grid_spec = pltpu.PrefetchScalarGridSpec(
        num_scalar_prefetch=2, grid=(nb, len(pairs)),
        in_specs=[pl.BlockSpec((tq, d), lambda b, p, qt, kt: (b * nq + qt[p], 0)),
                  pl.BlockSpec((tk, d), lambda b, p, qt, kt: (b * nk + kt[p], 0)),
                  pl.BlockSpec((tk, d), lambda b, p, qt, kt: (b * nk + kt[p], 0)),
                  pl.BlockSpec((tq, nh), lambda b, p, qt, kt: (b * nq + qt[p], 0)),
                  pl.BlockSpec((1, nh, tk), lambda b, p, qt, kt: (b, 0, kt[p]))],
        out_specs=pl.BlockSpec((tq, d), lambda b, p, qt, kt: (b * nq + qt[p], 0)),
        scratch_shapes=_flash_scratch(nh // hs, hs * tq, hs * HEAD_DIM))
    return pl.pallas_call(
        functools.partial(_attn_prompt_kernel, hs=hs), grid_spec=grid_spec,
        out_shape=jax.ShapeDtypeStruct((nb * s, d), BF16),
        compiler_params=_cparams(("parallel", "arbitrary")), name="attn_prompt")(
            qi_tab, ki_tab, q, kb, vb, c_rows, c_t)


def _attn_sample_kernel(q_ref, ck_ref, cv_ref, kn_ref, vn_ref, cq_ref, ckc_ref, ckn_ref, o_ref, *scr, hs, nkb):
    j = pl.program_id(1)
    w = hs * HEAD_DIM
    ng = q_ref.shape[1] // w

    @pl.when(j == 0)
    def _():
        _flash_init(cq_ref[0], hs, *scr)

    def sweep(kslab, vslab, ck, causal, kv_t):
        ck = ck * LOG2E
        for g in range(ng):
            _flash_slab(g, hs, _stack_heads(q_ref[:, g * w:(g + 1) * w]), kslab(g), vslab(g),
                        [ck[g * hs + i:g * hs + i + 1, :] for i in range(hs)], causal, *scr, kv_t=kv_t)

    sweep(lambda g: ck_ref[0, g * w:(g + 1) * w, :].astype(BF16),
          lambda g: cv_ref[0, g * w:(g + 1) * w, :].astype(BF16), ckc_ref[0], None, True)

    @pl.when(j == nkb - 1)
    def _():
        sweep(lambda g: kn_ref[:, g * w:(g + 1) * w], lambda g: vn_ref[:, g * w:(g + 1) * w], ckn_ref[0], 0, False)
        _flash_finish(o_ref, hs, scr[1])


def _attn_sample(q, kb, vb, cache_kt, cache_vt, c_new, c_past_t, c_new_t, n_prompt):
    d = q.shape[1]
    nh = d // HEAD_DIM
    nb, _, plen = cache_kt.shape
    lq = c_new.shape[1]
    hs = DEC_HS
    tk = min(DEC_TK, plen)
    nkb = plen // tk
    r0 = n_prompt // lq
    new_rows = pl.BlockSpec((lq, d), lambda b, j: (r0 + b, 0))
    cache = pl.BlockSpec((1, d, tk), lambda b, j: (b, 0, j))
    kern = functools.partial(_attn_sample_kernel, hs=hs, nkb=nkb)
    return pl.pallas_call(
        kern, grid=(nb, nkb),
        in_specs=[new_rows, cache, cache, new_rows, new_rows,
                  pl.BlockSpec((1, lq, nh), lambda b, j: (b, 0, 0)),
                  pl.BlockSpec((1, nh, tk), lambda b, j: (b, 0, j)),
                  pl.BlockSpec((1, nh, lq), lambda b, j: (b, 0, 0))],
        out_specs=pl.BlockSpec((lq, d), lambda b, j: (b, 0)),
        out_shape=jax.ShapeDtypeStruct((nb * lq, d), BF16),
        scratch_shapes=_flash_scratch(nh // hs, hs * lq, hs * HEAD_DIM),
        compiler_params=_cparams(("parallel", "arbitrary")), name="attn_sample")(
            q, cache_kt, cache_vt, kb, vb, c_new, c_past_t, c_new_t)


def kernel(x_prompt, x_sample, cache_k, cache_v, cache_logf, state_ssm_re, state_ssm_im, ffn_norm, w_ffn_gate, w_ffn_up, w_ffn_down, mix_norm, ssm_a_re, ssm_a_im, ssm_log_dt, ssm_b_re, ssm_b_im, ssm_c_re, ssm_c_im, ssm_d, w_glu_a, w_glu_b, kv_norm, w_kvf, b_f, k_norm, w_q, q_norm, w_o):
    bp, s, d = x_prompt.shape
    bs, lq, _ = x_sample.shape
    plen = cache_k.shape[1]
    nh = d // HEAD_DIM
    assert ffn_norm.shape[0] == 2 and ssm_a_re.shape[0] == 1 and w_q.shape[0] == 1
    n_prompt, n_sample = bp * s, bs * lq
    bf = lambda w: w.astype(BF16)

    def ffn(xs, l, j, **kw):
        return _ffn(xs, ffn_norm[l, j], w_ffn_gate, w_ffn_up, w_ffn_down, (l, j), **kw)

    x, u = ffn([x_prompt.reshape(n_prompt, d), x_sample.reshape(n_sample, d)], 0, 0, g2=mix_norm[0], norm_dtype=F32)
    mats = _s5_params(ssm_log_dt[0], ssm_a_re[0], ssm_a_im[0], ssm_b_re[0], ssm_b_im[0], ssm_c_re[0], ssm_c_im[0])
    z, (p_re, p_im, s_re, s_im) = _s5_layer(u, n_prompt, bp, bs, state_ssm_re[:, 0], state_ssm_im[:, 0], mats, ssm_d[0])
    x = _glu(x, z, bf(w_glu_a[0]), bf(w_glu_b[0]))
    x, hn = ffn([x], 0, 1, g2=kv_norm)

    kt_p, k_s, vt_p, v_s, lft_p, lft_s, kb, vb = _kvf(hn, bp, s, lq, bf(w_kvf[:, :2 * d]), bf(w_kvf[:, 2 * d:]), b_f, k_norm)

    def cumsum_t(lft, init):
        b, _, l = lft.shape
        init = jnp.zeros((b * nh, 1), F32) if init is None else init.reshape(b * nh, 1)
        return _cumsum_rows(lft.reshape(b * nh, l), init).reshape(b, nh, l)

    cp_t = cumsum_t(lft_p, None)
    cpast_t = cumsum_t(jnp.swapaxes(cache_logf.astype(F32), 1, 2), None)
    cnew_t = cumsum_t(lft_s, cpast_t[:, :, -1])

    x, un = ffn([x], 1, 0, g2=mix_norm[1])
    q = _qproj(un, bf(w_q[0]), (jnp.tile(q_norm[0], nh) * (LOG2E / math.sqrt(HEAD_DIM))).reshape(1, d))
    a_p = _attn_prompt(q, kb, vb, jnp.swapaxes(cp_t, 1, 2).reshape(n_prompt, nh), cp_t, bp, s)
    pos_minor = lambda c: jnp.transpose(c, (0, 2, 3, 1)).reshape(bs, d, plen)
    a_s = _attn_sample(q, kb, vb, pos_minor(cache_k), pos_minor(cache_v),
                       jnp.swapaxes(cnew_t, 1, 2), cpast_t, cnew_t, n_prompt)
    x = _oproj(x, a_p, a_s, bf(w_o[0]))
    (y_p, y_s), _ = ffn([x], 1, 1, split_out=(n_prompt, n_sample))

    cache_p = lambda a: jnp.transpose(a.reshape(bp, nh, HEAD_DIM, s), (0, 3, 1, 2))
    cache_s = lambda a: a.reshape(bs, lq, nh, HEAD_DIM)
    return (y_p.reshape(bp, s, d), y_s.reshape(bs, lq, d), p_re, p_im, cache_p(kt_p), cache_p(vt_p),
            jnp.swapaxes(lft_p, 1, 2), s_re, s_im, cache_s(k_s), cache_s(v_s), jnp.swapaxes(lft_s, 1, 2))
```

```python
import functools
import math

import jax
import jax.numpy as jnp
import numpy as np
from jax import lax
from jax.experimental import pallas as pl
from jax.experimental.pallas import tpu as pltpu

F32 = jnp.float32
BF16 = jnp.bfloat16

EPS = 1e-6
HEAD_DIM = 64
SSM_GROUP = 16
S5_CHUNK = 16
LANES = 128
_SLAB_GROUPS = LANES // SSM_GROUP
NEG_BIG = -1e30
LOG2E = math.log2(math.e)
VMEM_LIMIT = 57 * 1024 * 1024

FFN_TM = 1024
FFN_TF = 512
TOK_TM = 512
MAT_TM = 1024
ATT_TQ = 256
ATT_TK = 512
ATT_HS = 2
DEC_TK = 1024
DEC_HS = 4


def _cparams(sem):
    return pltpu.CompilerParams(dimension_semantics=sem, vmem_limit_bytes=VMEM_LIMIT)


def _rms_scale(x):
    return lax.rsqrt(jnp.mean(x * x, axis=-1, keepdims=True) + EPS)


def _dot(a, b):
    return jnp.dot(a, b, preferred_element_type=F32)


def _split2(x):
    hi = x.astype(BF16)
    lo = (x - hi.astype(F32)).astype(BF16)
    return hi, lo


def _split3(x):
    hi = x.astype(BF16)
    r1 = x - hi.astype(F32)
    mid = r1.astype(BF16)
    return hi, mid, (r1 - mid.astype(F32)).astype(BF16)


def _tile(limit, *sizes):
    while any(s % limit for s in sizes):
        limit //= 2
    return limit


def _part_maps(npb):
    first = lambda i, *_: (jnp.minimum(i, npb - 1), 0)
    second = lambda i, *_: (jnp.maximum(i - npb, 0), 0)
    return first, second


def _ffn_kernel(*refs, nf, npb, n_in, n_out, with_norm):
    x_refs, refs = refs[:n_in], refs[n_in:]
    g_ref, wg_ref, wu_ref, wd_ref = refs[:4]
    refs = refs[4:]
    g2_ref = refs[0] if with_norm else None
    refs = refs[with_norm:]
    o_refs, refs = refs[:n_out], refs[n_out:]
    o2_ref = refs[0] if with_norm else None
    h_scr, acc_scr = refs[with_norm:]
    i, f = pl.program_id(0), pl.program_id(1)

    def with_x(cond, fn):
        if n_in == 1:
            pl.when(cond)(lambda: fn(x_refs[0][...]))
        else:
            pl.when(cond & (i < npb))(lambda: fn(x_refs[0][...]))
            pl.when(cond & (i >= npb))(lambda: fn(x_refs[1][...]))

    def start(x):
        h_scr[...] = (x * _rms_scale(x) * g_ref[...]).astype(BF16)
        acc_scr[...] = jnp.zeros_like(acc_scr)

    with_x(f == 0, start)

    h = h_scr[...]
    a = _dot(h, wg_ref[...].astype(BF16))
    b = _dot(h, wu_ref[...].astype(BF16))
    t = (a * jax.nn.sigmoid(a)) * b
    acc_scr[...] += _dot(t.astype(BF16), wd_ref[...].astype(BF16))

    def finish(x):
        y = x + 0.5 * acc_scr[...]
        if n_out == 1:
            o_refs[0][...] = y
        else:
            @pl.when(i < npb)
            def _():
                o_refs[0][...] = y

            @pl.when(i >= npb)
            def _():
                o_refs[1][...] = y
        if with_norm:
            o2_ref[...] = (y * _rms_scale(y) * g2_ref[...]).astype(o2_ref.dtype)

    with_x(f == nf - 1, finish)


def _ffn(xs, g, wg, wu, wd, lj, g2=None, split_out=None, norm_dtype=BF16):
    d = xs[0].shape[1]
    n = sum(x.shape[0] for x in xs)
    dff = wg.shape[-1]
    l, j = lj
    tm, tf = min(FFN_TM, n), min(FFN_TF, dff)
    nf = dff // tf
    n_first = xs[0].shape[0] if len(xs) == 2 else (split_out[0] if split_out else n)
    assert all(x.shape[0] % tm == 0 for x in xs) and n_first % tm == 0
    npb = n_first // tm
    first, second = _part_maps(npb)
    with_norm = g2 is not None
    row = pl.BlockSpec((tm, d), lambda i, f: (i, 0))
    vec = pl.BlockSpec((1, d), lambda i, f: (0, 0))
    parts = [pl.BlockSpec((tm, d), first), pl.BlockSpec((tm, d), second)]
    in_specs = (parts if len(xs) == 2 else [row]) + [
        vec, pl.BlockSpec((None, None, d, tf), lambda i, f: (l, j, 0, f)),
        pl.BlockSpec((None, None, d, tf), lambda i, f: (l, j, 0, f)),
        pl.BlockSpec((None, None, tf, d), lambda i, f: (l, j, f, 0))]
    args = list(xs) + [g.reshape(1, d), wg, wu, wd]
    if split_out:
        out_shape = [jax.ShapeDtypeStruct((m, d), F32) for m in split_out]
        out_specs = list(parts)
    else:
        out_shape = [jax.ShapeDtypeStruct((n, d), F32)]
        out_specs = [row]
    n_out = len(out_shape)
    if with_norm:
        in_specs.append(vec)
        args.append(g2.reshape(1, d))
        out_shape.append(jax.ShapeDtypeStruct((n, d), norm_dtype))
        out_specs.append(row)
    kern = functools.partial(_ffn_kernel, nf=nf, npb=npb, n_in=len(xs), n_out=n_out, with_norm=with_norm)
    res = pl.pallas_call(
        kern, grid=(n // tm, nf), in_specs=in_specs, out_specs=out_specs, out_shape=out_shape,
        scratch_shapes=[pltpu.VMEM((tm, d), BF16), pltpu.VMEM((tm, d), F32)],
        compiler_params=_cparams(("arbitrary", "arbitrary")), name="ffn")(*args)
    y = tuple(res[:n_out]) if split_out else res[0]
    return y, (res[n_out] if with_norm else None)


def _s5_param_kernel(*refs):
    lax.fori_loop(0, _SLAB_GROUPS, functools.partial(_s5_param_group, refs), 0)


def _s5_param_group(refs, slot, carry):
    (ldt_ref, arc_ref, aic_ref, arr_ref, air_ref, btr_ref, bti_ref, ctr_ref, cti_ref,
     m_ref, sbr_ref, sbi_ref, car_ref, cai_ref, a16r_ref, a16i_ref) = refs
    t, c = S5_CHUNK, SSM_GROUP
    dt = jnp.exp(ldt_ref[slot])

    def cpow(ar, ai, n):
        mag = jnp.exp(ar * dt)
        pr, pi = mag * jnp.cos(ai * dt), mag * jnp.sin(ai * dt)
        n = jnp.asarray(n, jnp.int32)
        shape = jnp.broadcast_shapes(ar.shape, n.shape)
        re, im = jnp.ones(shape, F32), jnp.zeros(shape, F32)
        for b in range(t.bit_length()):
            bit = ((n >> b) & 1) == 1
            re, im = jnp.where(bit, re * pr - im * pi, re), jnp.where(bit, re * pi + im * pr, im)
            pr, pi = pr * pr - pi * pi, 2.0 * pr * pi
        return re, im

    def step_of(pos):
        seg = pos // c
        return (seg // _SLAB_GROUPS) * _SLAB_GROUPS + ((seg - slot) & (_SLAB_GROUPS - 1))

    arc, aic = arc_ref[slot], aic_ref[slot]
    arr, air = arr_ref[slot], air_ref[slot]
    p = arc.shape[0]
    lane_pc = lax.broadcasted_iota(jnp.int32, (p, t * c), 1)
    cr, ci = ctr_ref[slot], cti_ref[slot]
    pr, pi = cpow(arc, aic, step_of(lane_pc) + 1)
    car_ref[slot] = (cr * pr - ci * pi).astype(BF16)
    cai_ref[slot] = (-(cr * pi + ci * pr)).astype(BF16)
    pr, pi = cpow(arc, aic, lane_pc // c)
    csr = cr * pr - ci * pi
    csi = cr * pi + ci * pr

    abr, abi = cpow(arr, air, 1)
    xr, xi = abr - 1.0, abi
    den = arr * arr + air * air
    qr = (xr * arr + xi * air) / den
    qi = (xi * arr - xr * air) / den
    btr, bti = btr_ref[slot], bti_ref[slot]
    bbr = qr * btr - qi * bti
    bbi = qr * bti + qi * btr
    iidx = step_of(lax.broadcasted_iota(jnp.int32, (t * c, p), 0))
    pr, pi = cpow(arr, air, (t - 1) - iidx)
    sbr_ref[slot] = (bbr * pr - bbi * pi).astype(BF16)
    sbi_ref[slot] = (bbr * pi + bbi * pr).astype(BF16)
    pr, pi = cpow(arr, air, t)
    a16r_ref[slot] = pr
    a16i_ref[slot] = pi

    hp = lax.Precision.HIGHEST
    kt = (jnp.dot(bbr[:c], csr, precision=hp, preferred_element_type=F32)
          - jnp.dot(bbi[:c], csi, precision=hp, preferred_element_type=F32))
    lane = lax.broadcasted_iota(jnp.int32, (c, t * c), 1)
    for i in range(t):
        blk = kt if i == 0 else pltpu.roll(kt, i * c, 1)
        blk = jnp.where(lane >= i * c, blk, 0.0)
        blk = jnp.concatenate([pltpu.roll(blk[:, h * LANES:(h + 1) * LANES], slot * c, 1)
                               for h in range(t * c // LANES)], axis=1)
        seg = (i // _SLAB_GROUPS) * _SLAB_GROUPS + ((i + slot) & (_SLAB_GROUPS - 1))
        m_ref[slot, pl.ds(pl.multiple_of(seg * c, c), c), :] = blk.astype(BF16)
    return carry


def _s5_params(log_dt, a_re, a_im, b_re, b_im, c_re, c_im):
    g, p = a_re.shape
    t, c = S5_CHUNK, SSM_GROUP
    tc = t * c

    def tile_b(b):
        return jnp.broadcast_to(jnp.swapaxes(b, 1, 2)[:, None], (g, t, c, p)).reshape(g, tc, p)

    def tile_c(cm):
        return jnp.broadcast_to(jnp.swapaxes(cm, 1, 2)[:, :, None], (g, p, t, c)).reshape(g, p, tc)

    def spec(*shape):
        return pl.BlockSpec((_SLAB_GROUPS,) + shape, lambda i: (i, 0, 0))

    assert g % _SLAB_GROUPS == 0
    return pl.pallas_call(
        _s5_param_kernel, grid=(g // _SLAB_GROUPS,),
        in_specs=[spec(1, 1), spec(p, 1), spec(p, 1), spec(1, p), spec(1, p),
                  spec(tc, p), spec(tc, p), spec(p, tc), spec(p, tc)],
        out_specs=[spec(tc, tc), spec(tc, p), spec(tc, p), spec(p, tc), spec(p, tc), spec(1, p), spec(1, p)],
        out_shape=[jax.ShapeDtypeStruct((g, tc, tc), BF16),
                   jax.ShapeDtypeStruct((g, tc, p), BF16), jax.ShapeDtypeStruct((g, tc, p), BF16),
                   jax.ShapeDtypeStruct((g, p, tc), BF16), jax.ShapeDtypeStruct((g, p, tc), BF16),
                   jax.ShapeDtypeStruct((g, 1, p), F32), jax.ShapeDtypeStruct((g, 1, p), F32)],
        compiler_params=_cparams(("parallel",)), name="s5_params")(
            log_dt.reshape(g, 1, 1), a_re.reshape(g, p, 1), a_im.reshape(g, p, 1),
            a_re.reshape(g, 1, p), a_im.reshape(g, 1, p),
            tile_b(b_re), tile_b(b_im), tile_c(c_re), tile_c(c_im))


def _gelu_tanh(y):
    return 0.5 * y * (1.0 + jnp.tanh(math.sqrt(2.0 / math.pi) * (y + 0.044715 * (y * y * y))))


def _pick_segments(v, first):
    seg = lax.broadcasted_iota(jnp.int32, v[0].shape, 1) // SSM_GROUP
    out = v[(7 + first) % 8]
    for s in range(6, -1, -1):
        out = jnp.where(seg == s, v[(s + first) % 8], out)
    return out


def _s5_main_kernel(u_ref, m_ref, sbr_ref, sbi_ref, car_ref, cai_ref, a16r_ref, a16i_ref,
                    h0r_ref, h0i_ref, d_ref, z_ref, hpr_ref, hpi_ref, hsr_ref, hsi_ref,
                    u2_scr, sr_scr, si_scr, hr_scr, hi_scr, *, bp, nkp, bs, nks):
    t = S5_CHUNK
    npq = bp * nkp // 8
    nq = npq + bs * nks // 8
    sp = nkp * t

    def token0(q):
        return jnp.where(q < npq, (q % bp) * sp + (q // bp) * (8 * t), bp * sp + (q - npq) * (8 * t))

    def gather(q, carry):
        tok, rows = token0(q), pl.ds(pl.multiple_of(q * 8, 8), 8)
        for h in range(2):
            v = [u_ref[pl.ds(tok + 8 * h + i, 8, stride=t), :] for i in range(8)]
            v = [x if i == 0 else pltpu.roll(x, i * SSM_GROUP, 1) for i, x in enumerate(v)]
            for gi in range(8):
                u2_scr[gi, rows, h * LANES:(h + 1) * LANES] = _pick_segments(v, 8 - gi)
        return carry

    lax.fori_loop(0, nq, gather, 0, unroll=8)

    def group(gi, buf):
        sr, si, hrs, his = sr_scr.at[buf], si_scr.at[buf], hr_scr.at[buf], hi_scr.at[buf]
        u = u2_scr[gi]
        ub = u.astype(BF16)
        sr[...] = _dot(ub, sbr_ref[gi])
        si[...] = _dot(ub, sbi_ref[gi])
        ar, ai = a16r_ref[gi], a16i_ref[gi]

        def step(rows, hr, hi):
            hrs[rows, :] = hr
            his[rows, :] = hi
            return (ar * hr - ai * hi + sr[rows, :], ar * hi + ai * hr + si[rows, :])

        hr = hi = jnp.zeros((bp, ar.shape[1]), F32)
        for k in range(nkp):
            hr, hi = step(pl.ds((k // 8) * bp * 8 + k % 8, bp, stride=8), hr, hi)
        hpr_ref[gi] = hr
        hpi_ref[gi] = hi
        hr, hi = h0r_ref[gi], h0i_ref[gi]
        for k in range(nks):
            hr, hi = step(pl.ds(bp * nkp + k, bs, stride=nks), hr, hi)
        hsr_ref[gi] = hr
        hsi_ref[gi] = hi

        y = (_dot(ub, m_ref[gi]) + _dot(hrs[...].astype(BF16), car_ref[gi])
             + _dot(his[...].astype(BF16), cai_ref[gi]) + d_ref[gi] * u)
        u2_scr[gi] = _gelu_tanh(y)

    def group_pair(j, carry):
        group(2 * j, 0)
        group(2 * j + 1, 1)
        return carry

    lax.fori_loop(0, _SLAB_GROUPS // 2, group_pair, 0)

    def scatter(q, carry):
        tok, rows = token0(q), pl.ds(pl.multiple_of(q * 8, 8), 8)
        for h in range(2):
            v = [u2_scr[gi, rows, h * LANES:(h + 1) * LANES] for gi in range(8)]
            for i in range(8):
                w = _pick_segments(v, 8 - i)
                w = w if i == 0 else pltpu.roll(w, LANES - i * SSM_GROUP, 1)
                z_ref[pl.ds(tok + 8 * h + i, 8, stride=t), :] = w
        return carry

    lax.fori_loop(0, nq, scatter, 0, unroll=8)


def _s5_layer(u, n_prompt, bp, bs, h0_re, h0_im, mats, d_skip):
    n, d = u.shape
    t, c = S5_CHUNK, SSM_GROUP
    g, tc = d // c, t * c
    gs = LANES // c
    p = h0_re.shape[-1]
    nkp = n_prompt // (bp * t)
    nks = (n - n_prompt) // (bs * t)
    r = nkp * bp + nks * bs
    assert gs == 8 and t == 16 and bp % 8 == 0 and nkp % 8 == 0 and (bs * nks) % 8 == 0

    m, sbr, sbi, car, cai, a16r, a16i = mats
    d_t = jnp.broadcast_to(d_skip.reshape(g, 1, 1, c), (g, 1, t, c)).reshape(g, 1, tc)
    h0r = jnp.swapaxes(h0_re, 0, 1)
    h0i = jnp.swapaxes(h0_im, 0, 1)

    def spec(*shape):
        return pl.BlockSpec((gs,) + shape, lambda i: (i, 0, 0))

    kern = functools.partial(_s5_main_kernel, bp=bp, nkp=nkp, bs=bs, nks=nks)
    z, hpr, hpi, hsr, hsi = pl.pallas_call(
        kern, grid=(g // gs,),
        in_specs=[pl.BlockSpec((n, LANES), lambda i: (0, i), pipeline_mode=pl.Buffered(1)),
                  spec(tc, tc), spec(tc, p), spec(tc, p), spec(p, tc), spec(p, tc),
                  spec(1, p), spec(1, p), spec(bs, p), spec(bs, p), spec(1, tc)],
        out_specs=[pl.BlockSpec((n, LANES), lambda i: (0, i)),
                   spec(bp, p), spec(bp, p), spec(bs, p), spec(bs, p)],
        out_shape=[jax.ShapeDtypeStruct((n, d), F32),
                   jax.ShapeDtypeStruct((g, bp, p), F32), jax.ShapeDtypeStruct((g, bp, p), F32),
                   jax.ShapeDtypeStruct((g, bs, p), F32), jax.ShapeDtypeStruct((g, bs, p), F32)],
        scratch_shapes=[pltpu.VMEM((gs, r, tc), F32)] + [pltpu.VMEM((2, r, p), F32) for _ in range(4)],
        compiler_params=_cparams(("arbitrary",)), name="s5_main")(
            u, m, sbr, sbi, car, cai, a16r, a16i, h0r, h0i, d_t)
    states = tuple(jnp.swapaxes(h, 0, 1)[:, None] for h in (hpr, hpi, hsr, hsi))
    return z, states


def _glu_kernel(x_ref, z_ref, wa_ref, wb_ref, o_ref):
    z = z_ref[...].astype(BF16)
    o_ref[...] = x_ref[...] + _dot(z, wa_ref[...]) * jax.nn.sigmoid(_dot(z, wb_ref[...]))


def _glu(x, z, wa, wb):
    n, d = x.shape
    tm = _tile(MAT_TM, n)
    row = pl.BlockSpec((tm, d), lambda i: (i, 0))
    mat =pl.BlockSpec((d, d), lambda i: (0, 0))
    return pl.pallas_call(
        _glu_kernel, grid=(n // tm,), in_specs=[row, row, mat, mat], out_specs=row,
        out_shape=jax.ShapeDtypeStruct((n, d), F32),
        compiler_params=_cparams(("parallel",)), name="glu")(x, z, wa, wb)


def _head_norm(x, e_ref, et_ref):
    ms = _dot((x * x).astype(BF16), e_ref[...]) * (1.0 / HEAD_DIM)
    hi, lo = _split2(lax.rsqrt(ms + EPS))
    return _dot(hi, et_ref[...]) + _dot(lo, et_ref[...])


def _head_indicator(d):
    nh = d // HEAD_DIM
    e = (np.arange(d)[:, None] // HEAD_DIM == np.arange(LANES)[None, :]).astype(np.float32)
    assert nh <= LANES
    return jnp.asarray(e, BF16), jnp.asarray(e.T, BF16)


def _kvf_kernel(h_ref, wkv_ref, wf_ref, bf_ref, kn_ref, e_ref, et_ref,
                kp_ref, ks_ref, vp_ref, vs_ref, lp_ref, ls_ref, kb_ref, vb_ref, *, d, nh, npb):
    i = pl.program_id(0)
    h = h_ref[...]
    tm = h.shape[0]
    pkv = _dot(h, wkv_ref[...])
    kraw, v = pkv[:, :d], pkv[:, d:]
    k = kraw * _head_norm(kraw, e_ref, et_ref) * kn_ref[...]
    kb_ref[...] = k.astype(BF16)
    vb_ref[...] = v.astype(BF16)
    pf = _dot(h, wf_ref[...]) + bf_ref[...]
    lft = (jnp.minimum(pf, 0.0) - jnp.log1p(jnp.exp(-jnp.abs(pf)))).T[:nh]

    @pl.when(i < npb)
    def _():
        kp_ref[0] = k.T
        vp_ref[0] = v.T
        lp_ref[0] = lft

    @pl.when(i >= npb)
    def _():
        for hd in range(nh):
            rows = pl.ds(hd, tm, stride=nh)
            ks_ref[rows, :] = k[:, hd * HEAD_DIM:(hd + 1) * HEAD_DIM]
            vs_ref[rows, :] = v[:, hd * HEAD_DIM:(hd + 1) * HEAD_DIM]
        lq = ls_ref.shape[2]
        for j in range(tm // lq):
            ls_ref[j] = lft[:, j * lq:(j + 1) * lq]


def _kvf(hn, bp, s, lq, w_kv, w_f, b_f, k_norm):
    n, d = hn.shape
    nh = d // HEAD_DIM
    tm = min(TOK_TM, s)
    n_prompt = bp * s
    n_s = n - n_prompt
    assert s % tm == 0 and n_s % tm == 0 and tm % lq == 0
    npb, spb = n_prompt // tm, s // tm
    e, et = _head_indicator(d)
    row = pl.BlockSpec((tm, d), lambda i: (i, 0))
    const = lambda *sh: pl.BlockSpec(sh, lambda i: (0, 0))
    prompt_t = lambda rows: pl.BlockSpec(
        (1, rows, tm), lambda i: (jnp.minimum(i, npb - 1) // spb, 0, jnp.minimum(i, npb - 1) % spb))
    second = lambda i: (jnp.maximum(i - npb, 0), 0)
    kern = functools.partial(_kvf_kernel, d=d, nh=nh, npb=npb)
    return pl.pallas_call(
        kern, grid=(n // tm,),
        in_specs=[row, const(d, 2 * d), const(d, LANES), const(1, LANES), const(1, d), const(d, LANES), const(LANES, d)],
        out_specs=[prompt_t(d), pl.BlockSpec((tm * nh, HEAD_DIM), second),
                   prompt_t(d), pl.BlockSpec((tm * nh, HEAD_DIM), second),
                   prompt_t(nh), pl.BlockSpec((tm // lq, nh, lq), lambda i: (jnp.maximum(i - npb, 0), 0, 0)), row, row],
        out_shape=[jax.ShapeDtypeStruct((bp, d, s), F32), jax.ShapeDtypeStruct((n_s * nh, HEAD_DIM), F32),
                   jax.ShapeDtypeStruct((bp, d, s), F32), jax.ShapeDtypeStruct((n_s * nh, HEAD_DIM), F32),
                   jax.ShapeDtypeStruct((bp, nh, s), F32), jax.ShapeDtypeStruct((n_s // lq, nh, lq), F32),
                   jax.ShapeDtypeStruct((n, d), BF16), jax.ShapeDtypeStruct((n, d), BF16)],
        compiler_params=_cparams(("arbitrary",)), name="kvf")(
            hn, w_kv, jnp.pad(w_f, ((0, 0), (0, LANES - nh))), jnp.pad(b_f, (0, LANES - nh)).reshape(1, LANES),
            jnp.tile(k_norm, nh).reshape(1, d), e, et)


def _qproj_kernel(u_ref, wq_ref, qn_ref, e_ref, et_ref, q_ref):
    q = _dot(u_ref[...], wq_ref[...])
    q_ref[...] = (q * _head_norm(q, e_ref, et_ref) * qn_ref[...]).astype(BF16)


def _qproj(un, wq, qn_scaled):
    n, d = un.shape
    tm = _tile(MAT_TM, n)
    e, et = _head_indicator(d)
    row = pl.BlockSpec((tm, d), lambda i: (i, 0))
    const = lambda *s: pl.BlockSpec(s, lambda i: (0, 0))
    return pl.pallas_call(
        _qproj_kernel, grid=(n // tm,),
        in_specs=[row, const(d, d), const(1, d), const(d, LANES), const(LANES, d)], out_specs=row,
        out_shape=jax.ShapeDtypeStruct((n, d), BF16),
        compiler_params=_cparams(("parallel",)), name="qproj")(un, wq, qn_scaled, e, et)


def _oproj_kernel(x_ref, ap_ref, as_ref, wo_ref, o_ref, *, npb):
    a = jnp.where(pl.program_id(0) < npb, ap_ref[...], as_ref[...])
    o_ref[...] = x_ref[...] + _dot(a, wo_ref[...])


def _oproj(x, a_p, a_s, wo):
    n, d = x.shape
    tm = _tile(MAT_TM, a_p.shape[0], a_s.shape[0])
    assert a_p.shape[0] % tm == 0 and a_s.shape[0] % tm == 0
    npb = a_p.shape[0] // tm
    first, second = _part_maps(npb)
    row = pl.BlockSpec((tm, d), lambda i: (i, 0))
    return pl.pallas_call(
        functools.partial(_oproj_kernel, npb=npb), grid=(n // tm,),
        in_specs=[row, pl.BlockSpec((tm, d), first), pl.BlockSpec((tm, d), second),
                  pl.BlockSpec((d, d), lambda i: (0, 0))],
        out_specs=row, out_shape=jax.ShapeDtypeStruct((n, d), F32),
        compiler_params=_cparams(("parallel",)), name="oproj")(x, a_p, a_s, wo)


def _cumsum_kernel(x_ref, init_ref, o_ref, *, nblk, bw):
    tri = (lax.broadcasted_iota(jnp.int32, (bw, bw), 0) <= lax.broadcasted_iota(jnp.int32, (bw, bw), 1)).astype(BF16)
    carry = init_ref[...]
    for j in range(nblk):
        x = x_ref[:, j * bw:(j + 1) * bw]
        hi, mid, lo = _split3(x)
        cs = _dot(hi, tri) + _dot(mid, tri) + _dot(lo, tri) + carry
        o_ref[:, j * bw:(j + 1) * bw] = cs
        carry = cs[:, bw - 1:bw]


def _cumsum_rows(x, init):
    r, w = x.shape
    bw = min(256, w)
    kern = functools.partial(_cumsum_kernel, nblk=w // bw, bw=bw)
    return pl.pallas_call(
        kern, grid=(1,),
        in_specs=[pl.BlockSpec((r, w), lambda i: (0, 0)), pl.BlockSpec((r, 1), lambda i: (0, 0))],
        out_specs=pl.BlockSpec((r, w), lambda i: (0, 0)), out_shape=jax.ShapeDtypeStruct((r, w), F32),
        compiler_params=_cparams(("arbitrary",)), name="cumsum")(x, init)


def _lanes(x, w):
    if w <= LANES:
        return x[:, :w]
    return jnp.concatenate([x] * (w // LANES), axis=1)


def _head_masks(shape):
    head = lax.broadcasted_iota(jnp.int32, shape, 1) // HEAD_DIM
    return [head == i for i in range(shape[1] // HEAD_DIM)]


def _stack_heads(qs):
    zero = jnp.zeros_like(qs)
    return jnp.concatenate([jnp.where(mk, qs, zero) for mk in _head_masks(qs.shape)], axis=0)


def _flash_init(cq, hs, m_scr, acc_scr, cq_scr):
    tq = cq.shape[0]
    cq = cq * LOG2E
    for g in range(cq_scr.shape[0]):
        for i in range(hs):
            h = g * hs + i
            cq_scr[g, i * tq:(i + 1) * tq, :] = jnp.broadcast_to(cq[:, h:h + 1], (tq, LANES))


def _causal_keep(hs, tq, tk, off):
    keep = lax.broadcasted_iota(jnp.int32, (tq, tk), 1) <= lax.broadcasted_iota(jnp.int32, (tq, tk), 0) + off
    return jnp.concatenate([keep] * hs, axis=0)


def _flash_slab(g, hs, q_st, ks, vs, ck_rows, keep, m_scr, acc_scr, cq_scr, kv_t=False, first=False):
    tq, tk = q_st.shape[0] // hs, (ks.shape[1] if kv_t else ks.shape[0])
    nt = (((1,), (1,)), ((), ()))
    t = _dot(q_st, ks) if kv_t else lax.dot_general(q_st, ks, nt, preferred_element_type=F32)
    if ck_rows is not None:
        t = jnp.concatenate([t[i * tq:(i + 1) * tq] - ck_rows[i] for i in range(hs)], axis=0)
    if keep is not None:
        t = jnp.where(keep, t, NEG_BIG)
    cq = cq_scr[g]
    m_new = jnp.max(t, axis=1, keepdims=True) + cq
    if not first:
        m_prev = m_scr[g]
        m_new = jnp.maximum(m_prev, m_new)
        alpha = jnp.exp2(m_prev - m_new)
    pe = jnp.exp2(t - _lanes(m_new - cq, tk))
    w = hs * HEAD_DIM
    if w == LANES and not kv_t:
        pv = _dot(pe.astype(BF16), jnp.concatenate([vs, jnp.ones((tk, LANES), BF16)], axis=1))
    else:
        pv = lax.dot_general(pe.astype(BF16), vs, nt, preferred_element_type=F32) if kv_t else _dot(pe.astype(BF16), vs)
        pv = jnp.concatenate([pv, jnp.broadcast_to(jnp.sum(pe, axis=1, keepdims=True), (hs * tq, LANES))], axis=1)
    acc_scr[g] = pv if first else _lanes(alpha, w + LANES) * acc_scr[g] + pv
    m_scr[g] = m_new


def _flash_finish(o_ref, hs, acc_scr):
    tq = o_ref.shape[0]
    w = hs * HEAD_DIM
    masks = _head_masks((tq, w))
    for g in range(acc_scr.shape[0]):
        acc = acc_scr[g]
        o = acc[:, :w] / _lanes(acc[:, w:], w)
        out = o[(hs - 1) * tq:]
        for i in range(hs - 2, -1, -1):
            out = jnp.where(masks[i], o[i * tq:(i + 1) * tq], out)
        o_ref[:, g * w:(g + 1) * w] = out.astype(o_ref.dtype)


def _flash_scratch(ng, rows, w):
    return [pltpu.VMEM((ng, rows, LANES), F32), pltpu.VMEM((ng, rows, w + LANES), F32),
            pltpu.VMEM((ng, rows, LANES), F32)]


def _attn_prompt_kernel(qi_tab, ki_tab, q_ref, k_ref, v_ref, cq_ref, ck_ref, o_ref, *scr, hs):
    step = pl.program_id(1)
    qi, ki = qi_tab[step], ki_tab[step]
    tq, tk = q_ref.shape[0], k_ref.shape[0]
    w = hs * HEAD_DIM
    ng = q_ref.shape[1] // w
    last = (qi * tq + tq - 1) // tk

    @pl.when(ki == 0)
    def _():
        _flash_init(cq_ref[...], hs, *scr)

    def sweep(causal, nkeys, first):
        ck = ck_ref[0] * LOG2E
        keep = None if causal is None else _causal_keep(hs, tq, nkeys, causal)
        for g in range(ng):
            sl = slice(g * w, (g + 1) * w)
            _flash_slab(g, hs, _stack_heads(q_ref[:, sl]), k_ref[:nkeys, sl], v_ref[:nkeys, sl],
                        [ck[g * hs + i:g * hs + i + 1, :nkeys] for i in range(hs)], keep, *scr, first=first)

    for first in (True, False):
        order = (ki == 0) if first else (ki > 0)

        @pl.when(order & (ki < last))
        def _():
            sweep(None, tk, first)

        for off in range(0, tk, tq):
            @pl.when(order & (ki == last) & (qi * tq - ki * tk == off))
            def _():
                sweep(off, off + tq, first)
                _flash_finish(o_ref, hs, scr[1])


def _attn_prompt(q, kb, vb, c_rows, c_t, nb, s):
    d = q.shape[1]
    nh = d // HEAD_DIM
    hs = ATT_HS
    tq, tk = min(ATT_TQ, s), min(ATT_TK, s)
    assert tk % tq == 0 and s % tk == 0
    nq, nk = s // tq, s // tk
    pairs = [(i, j) for i in range(nq) for j in range((i * tq + tq - 1) // tk + 1)]
    qi_tab = jnp.asarray([p[0] for p in pairs], jnp.int32)
    ki_tab = jnp.asarray([p[1] for p in pairs], jnp.int32)
    grid_spec = pltpu.PrefetchScalarGridSpec(
        num_scalar_prefetch=2, grid=(nb, len(pairs)),
        in_specs=[pl.BlockSpec((tq, d), lambda b, p, qt, kt: (b * nq + qt[p], 0)),
                  pl.BlockSpec((tk, d), lambda b, p, qt, kt: (b * nk + kt[p], 0)),
                  pl.BlockSpec((tk, d), lambda b, p, qt, kt: (b * nk + kt[p], 0)),
                  pl.BlockSpec((tq, nh), lambda b, p, qt, kt: (b * nq + qt[p], 0)),
                  pl.BlockSpec((1, nh, tk), lambda b, p, qt, kt: (b, 0, kt[p]))],
        out_specs=pl.BlockSpec((tq, d), lambda b, p, qt, kt: (b * nq + qt[p], 0)),
        scratch_shapes=_flash_scratch(nh // hs, hs * tq, hs * HEAD_DIM))
    return pl.pallas_call(
        functools.partial(_attn_prompt_kernel, hs=hs), grid_spec=grid_spec,
        out_shape=jax.ShapeDtypeStruct((nb * s, d), BF16),
        compiler_params=_cparams(("parallel", "arbitrary")), name="attn_prompt")(
            qi_tab, ki_tab, q, kb, vb, c_rows, c_t)


def _attn_sample_kernel(q_ref, ck_ref, cv_ref, kn_ref, vn_ref, cq_ref, ckc_ref, ckn_ref, o_ref, *scr, hs, nkb):
    j = pl.program_id(1)
    w = hs * HEAD_DIM
    ng = q_ref.shape[1] // w

    @pl.when(j == 0)
    def _():
        _flash_init(cq_ref[0], hs, *scr)

    def sweep(kslab, vslab, ck, causal, kv_t, first):
        ck = ck * LOG2E
        keep = None if causal is None else _causal_keep(hs, q_ref.shape[0], ck.shape[1], causal)
        for g in range(ng):
            _flash_slab(g, hs, _stack_heads(q_ref[:, g * w:(g + 1) * w]), kslab(g), vslab(g),
                        [ck[g * hs + i:g * hs + i + 1, :] for i in range(hs)], keep, *scr, kv_t=kv_t, first=first)

    for first in (True, False):
        @pl.when((j == 0) if first else (j > 0))
        def _():
            sweep(lambda g: ck_ref[0, g * w:(g + 1) * w, :].astype(BF16),
                  lambda g: cv_ref[0, g * w:(g + 1) * w, :].astype(BF16), ckc_ref[0], None, True, first)

    @pl.when(j == nkb - 1)
    def _():
        sweep(lambda g: kn_ref[:, g * w:(g + 1) * w], lambda g: vn_ref[:, g * w:(g + 1) * w], ckn_ref[0], 0, False,
              False)
        _flash_finish(o_ref, hs, scr[1])


def _attn_sample(q, kb, vb, cache_kt, cache_vt, c_new, c_past_t, c_new_t, n_prompt):
    d = q.shape[1]
    nh = d // HEAD_DIM
    nb, _, plen = cache_kt.shape
    lq = c_new.shape[1]
    hs = DEC_HS
    tk = min(DEC_TK, plen)
    nkb = plen // tk
    r0 = n_prompt // lq
    new_rows = pl.BlockSpec((lq, d), lambda b, j: (r0 + b, 0))
    cache = pl.BlockSpec((1, d, tk), lambda b, j: (b, 0, j))
    kern = functools.partial(_attn_sample_kernel, hs=hs, nkb=nkb)
    return pl.pallas_call(
        kern, grid=(nb, nkb),
        in_specs=[new_rows, cache, cache, new_rows, new_rows,
                  pl.BlockSpec((1, lq, nh), lambda b, j: (b, 0, 0)),
                  pl.BlockSpec((1, nh, tk), lambda b, j: (b, 0, j)),
                  pl.BlockSpec((1, nh, lq), lambda b, j: (b, 0, 0))],
        out_specs=pl.BlockSpec((lq, d), lambda b, j: (b, 0)),
        out_shape=jax.ShapeDtypeStruct((nb * lq, d), BF16),
        scratch_shapes=_flash_scratch(nh // hs, hs * lq, hs * HEAD_DIM),
        compiler_params=_cparams(("parallel", "arbitrary")), name="attn_sample")(
            q, cache_kt, cache_vt, kb, vb, c_new, c_past_t, c_new_t)


def kernel(x_prompt, x_sample, cache_k, cache_v, cache_logf, state_ssm_re, state_ssm_im, ffn_norm, w_ffn_gate, w_ffn_up, w_ffn_down, mix_norm, ssm_a_re, ssm_a_im, ssm_log_dt, ssm_b_re, ssm_b_im, ssm_c_re, ssm_c_im, ssm_d, w_glu_a, w_glu_b, kv_norm, w_kvf, b_f, k_norm, w_q, q_norm, w_o):
    bp, s, d = x_prompt.shape
    bs, lq, _ = x_sample.shape
    plen = cache_k.shape[1]
    nh = d // HEAD_DIM
    assert ffn_norm.shape[0] == 2 and ssm_a_re.shape[0] == 1 and w_q.shape[0] == 1
    n_prompt, n_sample = bp * s, bs * lq
    bf = lambda w: w.astype(BF16)

    def ffn(xs, l, j, **kw):
        return _ffn(xs, ffn_norm[l, j], w_ffn_gate, w_ffn_up, w_ffn_down, (l, j), **kw)

    x, u = ffn([x_prompt.reshape(n_prompt, d), x_sample.reshape(n_sample, d)], 0, 0, g2=mix_norm[0], norm_dtype=F32)
    mats = _s5_params(ssm_log_dt[0], ssm_a_re[0], ssm_a_im[0], ssm_b_re[0], ssm_b_im[0], ssm_c_re[0], ssm_c_im[0])
    z, (p_re, p_im, s_re, s_im) = _s5_layer(u, n_prompt, bp, bs, state_ssm_re[:, 0], state_ssm_im[:, 0], mats, ssm_d[0])
    x = _glu(x, z, bf(w_glu_a[0]), bf(w_glu_b[0]))
    x, hn = ffn([x], 0, 1, g2=kv_norm)

    kt_p, k_s, vt_p, v_s, lft_p, lft_s, kb, vb = _kvf(hn, bp, s, lq, bf(w_kvf[:, :2 * d]), bf(w_kvf[:, 2 * d:]), b_f, k_norm)

    def cumsum_t(lft, init):
        b, _, l = lft.shape
        init = jnp.zeros((b * nh, 1), F32) if init is None else init.reshape(b * nh, 1)
        return _cumsum_rows(lft.reshape(b * nh, l), init).reshape(b, nh, l)

    cp_t = cumsum_t(lft_p, None)
    cpast_t = cumsum_t(jnp.swapaxes(cache_logf.astype(F32), 1, 2), None)
    cnew_t = cumsum_t(lft_s, cpast_t[:, :, -1])

    x, un = ffn([x], 1, 0, g2=mix_norm[1])
    q = _qproj(un, bf(w_q[0]), (jnp.tile(q_norm[0], nh) * (LOG2E / math.sqrt(HEAD_DIM))).reshape(1, d))
    a_p = _attn_prompt(q, kb, vb, jnp.swapaxes(cp_t, 1, 2).reshape(n_prompt, nh), cp_t, bp, s)
    pos_minor = lambda c: jnp.transpose(c, (0, 2, 3, 1)).reshape(bs, d, plen)
    a_s = _attn_sample(q, kb, vb, pos_minor(cache_k), pos_minor(cache_v),
                       jnp.swapaxes(cnew_t, 1, 2), cpast_t, cnew_t, n_prompt)
    x = _oproj(x, a_p, a_s, bf(w_o[0]))
    (y_p, y_s), _ = ffn([x], 1, 1, split_out=(n_prompt, n_sample))

    cache_p = lambda a: jnp.transpose(a.reshape(bp, nh, HEAD_DIM, s), (0, 3, 1, 2))
    cache_s = lambda a: a.reshape(bs, lq, nh, HEAD_DIM)
    return (y_p.reshape(bp, s, d), y_s.reshape(bs, lq, d), p_re, p_im, cache_p(kt_p), cache_p(vt_p),
            jnp.swapaxes(lft_p, 1, 2), s_re, s_im, cache_s(k_s), cache_s(v_s), jnp.swapaxes(lft_s, 1, 2))
```
